```python
import jax, jax.numpy as jnp
from jax import lax
import numpy as np

D_MODEL = 1024
BATCH = 8
SEQ = 2048
DEPTH = 1
DEC_BATCH = 16
DEC_SEQ = 2048
PAST_LEN = 128

N_MOD = 6
D_SGU = D_MODEL
CHUNK = 128
SGU_HEADS = D_SGU // 128
SGU_HEAD_DIM = D_SGU // SGU_HEADS
D_POOL = D_MODEL
POOL_WINDOWS = (2, 4, 8, 16)
POOL_GROUPS = len(POOL_WINDOWS)
POOL_GROUP_DIM = D_POOL // POOL_GROUPS
N_BRANCH = 2
D_IN = 2 * D_SGU + D_POOL + N_BRANCH * D_MODEL
N_GROUPS = 4
EXPERTS_PER_GROUP = 8
N_EXPERTS = N_GROUPS * EXPERTS_PER_GROUP
TOP_K = 2
D_EXPERT = D_MODEL // 2
EXPERT_BLOCK = 128
ALPHA = (2.0 * DEPTH) ** 0.25
BETA = (8.0 * DEPTH) ** -0.25
LN_EPS = 1e-5

kernel_name = "hybrid_sgu_pool_hmoe_encoder"


def _layer_norm(x):
    xf = x.astype(jnp.float32)
    mu = jnp.mean(xf, axis=-1, keepdims=True)
    var = jnp.mean(jnp.square(xf - mu), axis=-1, keepdims=True)
    return (xf - mu) * lax.rsqrt(var + LN_EPS)


def _ln_affine(x, g, b):
    return (_layer_norm(x) * g + b).astype(x.dtype)


def _spatial_gating(u, v, w_spatial, b_spatial, sgu_g, sgu_b):
    B, S, _ = u.shape
    nc = S // CHUNK
    v = _ln_affine(v, sgu_g, sgu_b).reshape(B, nc, CHUNK, SGU_HEADS, SGU_HEAD_DIM)
    mixed = jnp.einsum("hpq,bnqhc->bnphc", w_spatial, v) + b_spatial.T[None, None, :, :, None]
    u = u.reshape(B, nc, CHUNK, SGU_HEADS, SGU_HEAD_DIM)
    return (u * mixed).reshape(B, S, D_SGU)


def _multiscale_pool(p, w_pool, pool_scale):
    B, S, _ = p.shape
    pf = p.astype(jnp.float32)
    cs = jnp.concatenate([jnp.zeros((B, 1, D_POOL), jnp.float32), jnp.cumsum(pf, axis=1)], axis=1)
    pos = jnp.arange(S, dtype=jnp.int32)
    outs = []
    for gi, w in enumerate(POOL_WINDOWS):
        sl = slice(gi * POOL_GROUP_DIM, (gi + 1) * POOL_GROUP_DIM)
        lo = jnp.maximum(pos - w // 2, 0)
        hi = jnp.minimum(pos + (w - 1 - w // 2), S - 1)
        csg = cs[:, :, sl]
        count = (hi - lo + 1).astype(jnp.float32)[None, :, None]
        mean = (jnp.take(csg, hi + 1, axis=1) - jnp.take(csg, lo, axis=1)) / count
        d = (mean - pf[:, :, sl]).astype(p.dtype)
        outs.append(d @ w_pool[gi])
    return jnp.concatenate(outs, axis=-1) * pool_scale


def _token_mixer(h, w_in, b_in, w_spatial, b_spatial, sgu_g, sgu_b, w_pool, pool_scale,
                 w_branch_a, w_branch_b, w_out):
    z = h @ w_in + b_in
    z_u, z_v, z_pool, z_gate = jnp.split(z, [D_SGU, 2 * D_SGU, 2 * D_SGU + D_POOL], axis=-1)
    a = _spatial_gating(jax.nn.gelu(z_u), jax.nn.gelu(z_v), w_spatial, b_spatial, sgu_g, sgu_b)
    p = _multiscale_pool(z_pool, w_pool, pool_scale)
    g_a, g_b = jnp.split(jax.nn.sigmoid(z_gate), N_BRANCH, axis=-1)
    merged = g_a * (a @ w_branch_a) + g_b * (p @ w_branch_b)
    return merged @ w_out


def _hier_moe(h, w_router_group, b_router_group, w_router_expert, b_router_expert,
              w_exp_gate, w_exp_up, w_exp_down):
    T, D = h.shape
    hf = h.astype(jnp.float32)
    logit_g = hf @ w_router_group.astype(jnp.float32) + b_router_group.astype(jnp.float32)
    prob_g = jax.nn.softmax(logit_g, axis=-1)
    g_sel = jnp.argmax(logit_g, axis=-1).astype(jnp.int32)
    p_sel = jnp.take_along_axis(prob_g, g_sel[:, None], axis=-1)
    logit_e = (hf @ w_router_expert.astype(jnp.float32) + b_router_expert.astype(jnp.float32))
    logit_e = logit_e.reshape(T, N_GROUPS, EXPERTS_PER_GROUP)
    logit_e = jnp.take_along_axis(logit_e, g_sel[:, None, None], axis=1)[:, 0]
    top_logit, top_idx = lax.top_k(logit_e, TOP_K)
    gate = p_sel * jax.nn.softmax(top_logit, axis=-1)
    expert_id = (g_sel[:, None] * EXPERTS_PER_GROUP + top_idx.astype(jnp.int32)).reshape(-1)
    weight = gate.reshape(-1)
    token_id = jnp.repeat(jnp.arange(T, dtype=jnp.int32), TOP_K)

    n_assign = T * TOP_K
    n_blocks = -(-n_assign // EXPERT_BLOCK) + N_EXPERTS
    n_rows = n_blocks * EXPERT_BLOCK
    order = jnp.argsort(expert_id)
    eid_sorted = expert_id[order]
    counts = jnp.zeros((N_EXPERTS,), jnp.int32).at[expert_id].add(1)
    padded = (counts + EXPERT_BLOCK - 1) // EXPERT_BLOCK * EXPERT_BLOCK
    padded_end = jnp.cumsum(padded).astype(jnp.int32)
    padded_start = padded_end - padded
    start = (jnp.cumsum(counts) - counts).astype(jnp.int32)
    dest = padded_start[eid_sorted] + (jnp.arange(n_assign, dtype=jnp.int32) - start[eid_sorted])
    row_token = jnp.full((n_rows,), T, jnp.int32).at[dest].set(token_id[order])
    row_weight = jnp.zeros((n_rows,), jnp.float32).at[dest].set(weight[order])
    block_start = jnp.arange(n_blocks, dtype=jnp.int32) * EXPERT_BLOCK
    block_expert = jnp.minimum(jnp.searchsorted(padded_end, block_start, side="right"),
                               N_EXPERTS - 1).astype(jnp.int32)
    h_pad = jnp.concatenate([h, jnp.zeros((1, D), h.dtype)], axis=0)
    xs = h_pad[row_token].reshape(n_blocks, EXPERT_BLOCK, D)

    def expert_block(args):
        xb, e = args
        return (jax.nn.silu(xb @ w_exp_gate[e]) * (xb @ w_exp_up[e])) @ w_exp_down[e]

    ys = lax.map(expert_block, (xs, block_expert)).reshape(n_rows, D)
    out = jnp.zeros((T + 1, D), h.dtype).at[row_token].add(ys * row_weight[:, None].astype(h.dtype))
    return out[:T]


def _encoder_layer(x, c, w_mod, b_mod, w_in, b_in, w_spatial, b_spatial, sgu_g, sgu_b,
                   w_pool, pool_scale, w_branch_a, w_branch_b, w_out, ln1_g, ln1_b,
                   w_router_group, b_router_group, w_router_expert, b_router_expert,
                   w_exp_gate, w_exp_up, w_exp_down, ln2_g, ln2_b):
    B, S, D = x.shape
    mod = (jax.nn.silu(c) @ w_mod + b_mod).reshape(B, N_MOD, 1, D)
    shift1, scale1, gate1 = mod[:, 0], mod[:, 1], mod[:, 2]
    shift2, scale2, gate2 = mod[:, 3], mod[:, 4], mod[:, 5]
    h = (_layer_norm(x) * (1 + scale1) + shift1).astype(x.dtype)
    t = _token_mixer(h, w_in, b_in, w_spatial, b_spatial, sgu_g, sgu_b, w_pool, pool_scale,
                     w_branch_a, w_branch_b, w_out)
    x = _ln_affine(ALPHA * x + gate1 * t, ln1_g, ln1_b)
    h = (_layer_norm(x) * (1 + scale2) + shift2).astype(x.dtype)
    f = _hier_moe(h.reshape(B * S, D), w_router_group, b_router_group, w_router_expert,
                  b_router_expert, w_exp_gate, w_exp_up, w_exp_down).reshape(B, S, D)
    x = _ln_affine(ALPHA * x + gate2 * f, ln2_g, ln2_b)
    return x


def setup_inputs(seed: int = 0) -> dict:
    key = jax.random.key(seed)
    ks = jax.random.split(key, 32)

    def nrm(k, shape, scale):
        return jax.random.normal(k, shape, jnp.float32) * scale

    L, D = DEPTH, D_MODEL
    return {
        "x_prompt": nrm(ks[0], (BATCH, SEQ, D), 1.0),
        "x_sample": nrm(ks[1], (DEC_BATCH, DEC_SEQ, D), 1.0),
        "c_prompt": nrm(ks[2], (BATCH, D), 1.0),
        "c_sample": nrm(ks[3], (DEC_BATCH, D), 1.0),
        "w_mod": nrm(ks[4], (L, D, N_MOD * D), 0.5 * D ** -0.5),
        "b_mod": nrm(ks[5], (L, N_MOD * D), 0.01),
        "w_in": nrm(ks[6], (L, D, D_IN), D ** -0.5),
        "b_in": nrm(ks[7], (L, D_IN), 0.01),
        "w_spatial": nrm(ks[8], (L, SGU_HEADS, CHUNK, CHUNK), CHUNK ** -0.5),
        "b_spatial": 1.0 + nrm(ks[9], (L, SGU_HEADS, CHUNK), 0.1),
        "sgu_g": 1.0 + nrm(ks[10], (L, D_SGU), 0.05),
        "sgu_b": nrm(ks[11], (L, D_SGU), 0.01),
        "w_pool": nrm(ks[12], (L, POOL_GROUPS, POOL_GROUP_DIM, POOL_GROUP_DIM), POOL_GROUP_DIM ** -0.5),
        "pool_scale": 1.0 + nrm(ks[13], (L, D_POOL), 0.05),
        "w_branch_a": nrm(ks[14], (L, D_SGU, D), D_SGU ** -0.5),
        "w_branch_b": nrm(ks[15], (L, D_POOL, D), D_POOL ** -0.5),
        "w_out": nrm(ks[16], (L, D, D), BETA * D ** -0.5),
        "ln1_g": 1.0 + nrm(ks[17], (L, D), 0.05),
        "ln1_b": nrm(ks[18], (L, D), 0.01),
        "w_router_group": nrm(ks[19], (L, D, N_GROUPS), D ** -0.5),
        "b_router_group": nrm(ks[20], (L, N_GROUPS), 0.01),
        "w_router_expert": nrm(ks[21], (L, D, N_EXPERTS), D ** -0.5),
        "b_router_expert": nrm(ks[22], (L, N_EXPERTS), 0.01),
        "w_exp_gate": nrm(ks[23], (L, N_EXPERTS, D, D_EXPERT), D ** -0.5),
        "w_exp_up": nrm(ks[24], (L, N_EXPERTS, D, D_EXPERT), D ** -0.5),
        "w_exp_down": nrm(ks[25], (L, N_EXPERTS, D_EXPERT, D), BETA * D_EXPERT ** -0.5),
        "ln2_g": 1.0 + nrm(ks[26], (L, D), 0.05),
        "ln2_b": nrm(ks[27], (L, D), 0.01),
    }


def reference(x_prompt, x_sample, c_prompt, c_sample, w_mod, b_mod, w_in, b_in, w_spatial,
              b_spatial, sgu_g, sgu_b, w_pool, pool_scale, w_branch_a, w_branch_b, w_out,
              ln1_g, ln1_b, w_router_group, b_router_group, w_router_expert, b_router_expert,
              w_exp_gate, w_exp_up, w_exp_down, ln2_g, ln2_b):
    layer_params = (w_mod, b_mod, w_in, b_in, w_spatial, b_spatial, sgu_g, sgu_b, w_pool,
                    pool_scale, w_branch_a, w_branch_b, w_out, ln1_g, ln1_b, w_router_group,
                    b_router_group, w_router_expert, b_router_expert, w_exp_gate, w_exp_up,
                    w_exp_down, ln2_g, ln2_b)
    y_prompt, y_sample = x_prompt, x_sample
    for l in range(DEPTH):
        p = [w[l] for w in layer_params]
        y_prompt = _encoder_layer(y_prompt, c_prompt, *p)
        y_sample = _encoder_layer(y_sample, c_sample, *p)
    return (y_prompt, y_sample)
```

```python
import functools

import jax
import jax.numpy as jnp
from jax import lax
from jax.experimental import pallas as pl
from jax.experimental.pallas import tpu as pltpu

F32 = jnp.float32
BF16 = jnp.bfloat16
U32 = jnp.uint32

D = 1024
CHUNK = 128
SGU_HEADS = 8
HEAD_DIM = D // SGU_HEADS
POOL_WINDOWS = (2, 4, 8, 16)
POOL_GROUP_DIM = D // len(POOL_WINDOWS)
N_MOD = 6
N_GROUPS = 4
EPG = 8
N_EXPERTS = N_GROUPS * EPG
D_EXPERT = D // 2
LN_EPS = 1e-5

HALO = 16
COLB = 256
ROUTER_LANES = 128
EXPERT_LANE0 = N_GROUPS
MIXER_TS = 256
EXPERT_BM = 256
DISPATCH_TD = 2048
FINAL_TK = 256
VMEM_LIMIT = 56 * 1024 * 1024


def _ln(x):
    mu = jnp.mean(x, axis=-1, keepdims=True)
    xc = x - mu
    var = jnp.mean(xc * xc, axis=-1, keepdims=True)
    return xc * lax.rsqrt(var + LN_EPS)


def _dot(a, b):
    return jnp.dot(a, b, preferred_element_type=F32)


def _mod_kernel(c_ref, w_ref, b_ref, o_ref):
    a = jax.nn.silu(c_ref[...]).astype(BF16)
    o_ref[...] = _dot(a, w_ref[...].astype(BF16)) + b_ref[...]


def _modulation(c_all, w_mod, b_mod):
    nb = c_all.shape[0]
    n_out = w_mod.shape[1]
    cb = 512
    return pl.pallas_call(
        _mod_kernel,
        grid=(n_out // cb,),
        in_specs=[pl.BlockSpec((nb, D), lambda j: (0, 0)),
                  pl.BlockSpec((D, cb), lambda j: (0, j)),
                  pl.BlockSpec((1, cb), lambda j: (0, j))],
        out_specs=pl.BlockSpec((nb, cb), lambda j: (0, j)),
        out_shape=jax.ShapeDtypeStruct((nb, n_out), F32),
        name="mod",
    )(c_all, w_mod, b_mod.reshape(1, n_out))


def _mixer_kernel(nbp, ts, s_len, alpha,
                  xp_ref, xs_ref, xpp_ref, xsp_ref, xpn_ref, xsn_ref, mod_ref,
                  win_ref, bin_ref, ws_ref, bsf_ref, sg_ref, sb_ref, wpool_ref, psc_ref,
                  wa_ref, wb_ref, wo_ref, l1g_ref, l1b_ref, wr_ref, br_ref, tri_ref,
                  x1_ref, h2u_ref, rt_ref, cnt_ref,
                  h_scr, u_scr, v_scr, vb_scr, a_scr, p_scr, m_scr, h2_scr, carry_scr):
    b = pl.program_id(0)
    i = pl.program_id(1)
    is_p = b < nbp
    ncb = D // COLB

    @pl.when((b == 0) & (i == 0))
    def _():
        carry_scr[...] = jnp.zeros_like(carry_scr)

    mod = mod_ref[0]
    shift1, scale1, gate1 = mod[0:1], mod[1:2], mod[2:3]
    shift2, scale2 = mod[3:4], mod[4:5]

    def adaln1(xv):
        return (_ln(xv) * (1.0 + scale1) + shift1).astype(BF16)

    h_scr[0:HALO, :] = adaln1(jnp.where(is_p, xpp_ref[0], xsp_ref[0]))
    h_scr[HALO:HALO + ts, :] = adaln1(jnp.where(is_p, xp_ref[0], xs_ref[0]))
    h_scr[HALO + ts:, :] = adaln1(jnp.where(is_p, xpn_ref[0], xsn_ref[0]))

    def proj(rows, c0):
        return _dot(h_scr[rows, :], win_ref[:, c0:c0 + COLB]) + bin_ref[:, c0:c0 + COLB]

    main = slice(HALO, HALO + ts)

    vsum = jnp.zeros((ts, 1), F32)
    for j in range(ncb):
        cols = slice(j * COLB, (j + 1) * COLB)
        u_scr[:, cols] = jax.nn.gelu(proj(main, j * COLB))
        gv = jax.nn.gelu(proj(main, D + j * COLB))
        v_scr[:, cols] = gv
        vsum = vsum + jnp.sum(gv, axis=-1, keepdims=True)
    vmu = vsum * (1.0 / D)
    vss = jnp.zeros((ts, 1), F32)
    for j in range(ncb):
        cols = slice(j * COLB, (j + 1) * COLB)
        xc = v_scr[:, cols] - vmu
        vss = vss + jnp.sum(xc * xc, axis=-1, keepdims=True)
    vrs = lax.rsqrt(vss * (1.0 / D) + LN_EPS)
    for j in range(ncb):
        cols = slice(j * COLB, (j + 1) * COLB)
        vb_scr[:, cols] = ((v_scr[:, cols] - vmu) * vrs * sg_ref[:, cols] + sb_ref[:, cols]).astype(BF16)

    for c in range(ts // CHUNK):
        rows = slice(c * CHUNK, (c + 1) * CHUNK)
        for hh in range(SGU_HEADS):
            cols = slice(hh * HEAD_DIM, (hh + 1) * HEAD_DIM)
            mixed = _dot(ws_ref[hh], vb_scr[rows, cols]) + bsf_ref[:, cols]
            a_scr[rows, cols] = (u_scr[rows, cols] * mixed).astype(BF16)

    n_ext = ts + 2 * HALO
    ext_pos = lax.broadcasted_iota(jnp.int32, (n_ext, 1), 0) + (i * ts - HALO)
    ext_valid = (ext_pos >= 0) & (ext_pos < s_len)
    pos = lax.broadcasted_iota(jnp.int32, (ts, 1), 0) + i * ts
    for gi, w in enumerate(POOL_WINDOWS):
        cols = slice(gi * POOL_GROUP_DIM, (gi + 1) * POOL_GROUP_DIM)
        zp = jnp.where(ext_valid, proj(slice(None), 2 * D + gi * POOL_GROUP_DIM), 0.0)
        acc = zp + pltpu.roll(zp, 1, 0)
        if w >= 4:
            acc = pltpu.roll(acc, 1, 0) + pltpu.roll(acc, n_ext - 1, 0)
        if w >= 8:
            acc = pltpu.roll(acc, 2, 0) + pltpu.roll(acc, n_ext - 2, 0)
        if w >= 16:
            acc = pltpu.roll(acc, 4, 0) + pltpu.roll(acc, n_ext - 4, 0)
        lo = jnp.maximum(pos - w // 2, 0)
        hi = jnp.minimum(pos + (w - 1 - w // 2), s_len - 1)
        inv_count = 1.0 / (hi - lo + 1).astype(F32)
        dd = (acc[main] * inv_count - zp[main]).astype(BF16)
        p_scr[:, cols] = (_dot(dd, wpool_ref[gi]) * psc_ref[:, cols]).astype(BF16)

    for j in range(ncb):
        cols = slice(j * COLB, (j + 1) * COLB)
        ga = jax.nn.sigmoid(proj(main, 3 * D + j * COLB))
        gb = jax.nn.sigmoid(proj(main, 4 * D + j * COLB))
        ta = _dot(a_scr[...], wa_ref[:, cols])
        tb = _dot(p_scr[...], wb_ref[:, cols])
        m_scr[:, cols] = (ga * ta + gb * tb).astype(BF16)

    ysum = jnp.zeros((ts, 1), F32)
    for j in range(ncb):
        cols = slice(j * COLB, (j + 1) * COLB)
        t = _dot(m_scr[...], wo_ref[:, cols])
        xv = jnp.where(is_p, xp_ref[0, :, cols], xs_ref[0, :, cols])
        y = alpha * xv + gate1[:, cols] * t
        v_scr[:, cols] = y
        ysum = ysum + jnp.sum(y, axis=-1, keepdims=True)
    ymu = ysum * (1.0 / D)
    yss = jnp.zeros((ts, 1), F32)
    for j in range(ncb):
        cols = slice(j * COLB, (j + 1) * COLB)
        xc = v_scr[:, cols] - ymu
        yss = yss + jnp.sum(xc * xc, axis=-1, keepdims=True)
    yrs = lax.rsqrt(yss * (1.0 / D) + LN_EPS)
    xsum = jnp.zeros((ts, 1), F32)
    for j in range(ncb):
        cols = slice(j * COLB, (j + 1) * COLB)
        x1 = (v_scr[:, cols] - ymu) * yrs * l1g_ref[:, cols] + l1b_ref[:, cols]
        x1_ref[0, :, cols] = x1
        v_scr[:, cols] = x1
        xsum = xsum + jnp.sum(x1, axis=-1, keepdims=True)

    xmu = xsum * (1.0 / D)
    xss = jnp.zeros((ts, 1), F32)
    for j in range(ncb):
        cols = slice(j * COLB, (j + 1) * COLB)
        xc = v_scr[:, cols] - xmu
        xss = xss + jnp.sum(xc * xc, axis=-1, keepdims=True)
    xrs = lax.rsqrt(xss * (1.0 / D) + LN_EPS)
    for j in range(ncb):
        cols = slice(j * COLB, (j + 1) * COLB)
        h2 = (v_scr[:, cols] - xmu) * xrs * (1.0 + scale2[:, cols]) + shift2[:, cols]
        h2_scr[:, cols] = h2.astype(BF16)
    half = D // 2
    lo_bits = lax.bitcast_convert_type(h2_scr[:, 0:half].astype(F32), U32)
    hi_bits = lax.bitcast_convert_type(h2_scr[:, half:D].astype(F32), U32)
    h2u_ref[...] = (lo_bits >> 16) | (hi_bits & jnp.uint32(0xFFFF0000))

    logits = _dot(h2_scr[...], wr_ref[...]) + br_ref[...]
    lane = lax.broadcasted_iota(jnp.int32, (ts, ROUTER_LANES), 1)
    lane_f = lane.astype(F32)
    neg = -jnp.inf
    is_g = lane < N_GROUPS
    lg = jnp.where(is_g, logits, neg)
    mg = jnp.max(lg, axis=-1, keepdims=True)
    g_sel = jnp.min(jnp.where(lg == mg, lane_f, float(ROUTER_LANES)), axis=-1, keepdims=True)
    p_sel = 1.0 / jnp.sum(jnp.where(is_g, jnp.exp(logits - mg), 0.0), axis=-1, keepdims=True)
    e_lo = EXPERT_LANE0 + g_sel * EPG
    in_grp = (lane_f >= e_lo) & (lane_f < e_lo + EPG)
    le = jnp.where(in_grp, logits, neg)
    m1 = jnp.max(le, axis=-1, keepdims=True)
    i1 = jnp.min(jnp.where(le == m1, lane_f, float(ROUTER_LANES)), axis=-1, keepdims=True)
    le2 = jnp.where(lane_f == i1, neg, le)
    m2 = jnp.max(le2, axis=-1, keepdims=True)
    i2 = jnp.min(jnp.where(le2 == m2, lane_f, float(ROUTER_LANES)), axis=-1, keepdims=True)
    e2x = jnp.exp(m2 - m1)
    den = 1.0 + e2x
    g1 = p_sel / den
    g2 = p_sel * e2x / den

    hit1 = lane_f == i1
    hit2 = lane_f == i2
    onehot = jnp.where(hit1 | hit2, 1.0, 0.0)
    pre = _dot(tri_ref[...], onehot.astype(BF16)) + carry_scr[0:1, :]
    r1 = jnp.sum(jnp.where(hit1, pre, 0.0), axis=-1, keepdims=True)
    r2 = jnp.sum(jnp.where(hit2, pre, 0.0), axis=-1, keepdims=True)
    carry_scr[0:1, :] = carry_scr[0:1, :] + jnp.sum(onehot, axis=0, keepdims=True)
    cnt_ref[...] = carry_scr[...]

    rt = jnp.where(lane == 0, i1 - EXPERT_LANE0, 0.0)
    rt = jnp.where(lane == 1, i2 - EXPERT_LANE0, rt)
    rt = jnp.where(lane == 2, g1, rt)
    rt = jnp.where(lane == 3, g2, rt)
    rt = jnp.where(lane == 4, r1, rt)
    rt = jnp.where(lane == 5, r2, rt)
    rt_ref[...] = rt


def _const_spec(shape):
    nd = len(shape)
    return pl.BlockSpec(shape, lambda b, i: (0,) * nd, pipeline_mode=pl.Buffered(1))


def _mixer(x_p, x_s, mod, wts, alpha, ts):
    nbp, s_len, _ = x_p.shape
    nbs = x_s.shape[0]
    assert x_s.shape[1] == s_len and s_len % ts == 0 and ts % CHUNK == 0
    nb = nbp + nbs
    n_i = s_len // ts
    hb = ts // HALO
    n_hb = s_len // HALO

    def pick(b, on_p, val, const):
        return jnp.where(b < nbp if on_p else b >= nbp, val, const)

    def main_map(on_p):
        def f(b, i):
            bb = pick(b, on_p, b if on_p else b - nbp, nbp - 1 if on_p else 0)
            ii = pick(b, on_p, i, n_i - 1 if on_p else 0)
            return (bb, ii, 0)
        return f

    def halo_map(on_p, nxt):
        def f(b, i):
            bb = pick(b, on_p, b if on_p else b - nbp, nbp - 1 if on_p else 0)
            idx = jnp.minimum((i + 1) * hb, n_hb - 1) if nxt else jnp.maximum(i * hb - 1, 0)
            ii = pick(b, on_p, idx, n_hb - 1 if on_p else 0)
            return (bb, ii, 0)
        return f

    tri = (lax.broadcasted_iota(jnp.int32, (ts, ts), 1)
           < lax.broadcasted_iota(jnp.int32, (ts, ts), 0)).astype(BF16)
    consts = list(wts) + [tri]
    in_specs = [
        pl.BlockSpec((1, ts, D), main_map(True)),
        pl.BlockSpec((1, ts, D), main_map(False)),
        pl.BlockSpec((1, HALO, D), halo_map(True, False)),
        pl.BlockSpec((1, HALO, D), halo_map(False, False)),
        pl.BlockSpec((1, HALO, D), halo_map(True, True)),
        pl.BlockSpec((1, HALO, D), halo_map(False, True)),
        pl.BlockSpec((1, N_MOD, D), lambda b, i: (b, 0, 0)),
    ] + [_const_spec(w.shape) for w in consts]
    n_tok = nb * s_len
    out_shape = (
        jax.ShapeDtypeStruct((nb, s_len, D), F32),
        jax.ShapeDtypeStruct((n_tok, D // 2), U32),
        jax.ShapeDtypeStruct((n_tok, ROUTER_LANES), F32),
        jax.ShapeDtypeStruct((8, ROUTER_LANES), F32),
    )
    out_specs = (
        pl.BlockSpec((1, ts, D), lambda b, i: (b, i, 0)),
        pl.BlockSpec((ts, D // 2), lambda b, i: (b * n_i + i, 0)),
        pl.BlockSpec((ts, ROUTER_LANES), lambda b, i: (b * n_i + i, 0)),
        pl.BlockSpec((8, ROUTER_LANES), lambda b, i: (0, 0)),
    )
    scratch = [
        pltpu.VMEM((ts + 2 * HALO, D), BF16),
        pltpu.VMEM((ts, D), F32),
        pltpu.VMEM((ts, D), F32),
        pltpu.VMEM((ts, D), BF16),
        pltpu.VMEM((ts, D), BF16),
        pltpu.VMEM((ts, D), BF16),
        pltpu.VMEM((ts, D), BF16),
        pltpu.VMEM((ts, D), BF16),
        pltpu.VMEM((8, ROUTER_LANES), F32),
    ]
    return pl.pallas_call(
        functools.partial(_mixer_kernel, nbp, ts, s_len, alpha),
        grid=(nb, n_i),
        in_specs=in_specs,
        out_specs=out_specs,
        out_shape=out_shape,
        scratch_shapes=scratch,
        compiler_params=pltpu.CompilerParams(
            dimension_semantics=("arbitrary", "arbitrary"), vmem_limit_bytes=VMEM_LIMIT),
        name="mixer",
    )(x_p, x_s, x_p, x_s, x_p, x_s, mod, *consts)


def _dispatch_kernel(td, d1_ref, d2_ref, h2u_hbm, xs_init_hbm, xs_hbm, sem):
    del xs_init_hbm
    base = pl.program_id(0) * td

    def row_copy(t, dest):
        return pltpu.make_async_copy(h2u_hbm.at[pl.ds(t, 1)], xs_hbm.at[pl.ds(dest, 1)], sem)

    def start(r, carry):
        t = base + r
        row_copy(t, d1_ref[t]).start()
        row_copy(t, d2_ref[t]).start()
        return carry

    def wait(r, carry):
        t = base + r
        row_copy(t, d1_ref[t]).wait()
        row_copy(t, d2_ref[t]).wait()
        return carry

    lax.fori_loop(0, td, start, 0, unroll=8)
    lax.fori_loop(0, td, wait, 0, unroll=8)


def _dispatch(h2u, dest1, dest2, n_rows, td):
    n_tok = h2u.shape[0]
    assert n_tok % td == 0
    xs_init = jnp.zeros((n_rows, D // 2), U32)
    return pl.pallas_call(
        functools.partial(_dispatch_kernel, td),
        grid_spec=pltpu.PrefetchScalarGridSpec(
            num_scalar_prefetch=2,
            grid=(n_tok // td,),
            in_specs=[pl.BlockSpec(memory_space=pl.ANY), pl.BlockSpec(memory_space=pl.ANY)],
            out_specs=pl.BlockSpec(memory_space=pl.ANY),
            scratch_shapes=[pltpu.SemaphoreType.DMA(())],
        ),
        out_shape=jax.ShapeDtypeStruct((n_rows, D // 2), U32),
        input_output_aliases={3: 0},
        compiler_params=pltpu.CompilerParams(dimension_semantics=("arbitrary",)),
        name="dispatch",
    )(dest1, dest2, h2u, xs_init)


def _expert_kernel(bm, be_ref, nused_ref, xs_ref, wg_ref, wu_ref, wd_ref, ys_ref,
                   wg_scr, wu_scr, wd_scr, act_scr):
    j = pl.program_id(0)
    used = j < nused_ref[0]
    new_expert = (j == 0) | (be_ref[j] != be_ref[jnp.maximum(j - 1, 0)])
    rb = 256

    @pl.when(used & new_expert)
    def _():
        for r in range(0, D, rb):
            wg_scr[r:r + rb, :] = wg_ref[0, r:r + rb, :].astype(BF16)
            wu_scr[r:r + rb, :] = wu_ref[0, r:r + rb, :].astype(BF16)
        for r in range(0, D_EXPERT, rb):
            wd_scr[r:r + rb, :] = wd_ref[0, r:r + rb, :].astype(BF16)

    @pl.when(used)
    def _():
        half = D // 2
        bits = xs_ref[...]
        lo = lax.bitcast_convert_type(bits << 16, F32).astype(BF16)
        hi = lax.bitcast_convert_type(bits & jnp.uint32(0xFFFF0000), F32).astype(BF16)
        for c in range(D_EXPERT // COLB):
            cols = slice(c * COLB, (c + 1) * COLB)
            g = _dot(lo, wg_scr[0:half, cols]) + _dot(hi, wg_scr[half:D, cols])
            up = _dot(lo, wu_scr[0:half, cols]) + _dot(hi, wu_scr[half:D, cols])
            act_scr[:, cols] = (jax.nn.silu(g) * up).astype(BF16)
        for c in range(D // COLB):
            cols = slice(c * COLB, (c + 1) * COLB)
            ys_ref[:, cols] = _dot(act_scr[...], wd_scr[:, cols])

    @pl.when(jnp.logical_not(used))
    def _():
        ys_ref[...] = jnp.zeros_like(ys_ref)


def _experts(xs, block_expert, n_used, w_gate, w_up, w_down, bm):
    n_rows = xs.shape[0]
    n_blocks = n_rows // bm

    def x_map(j, be, nu):
        return (jnp.minimum(j, nu[0] - 1), 0)

    def w_map(j, be, nu):
        return (be[j], 0, 0)

    return pl.pallas_call(
        functools.partial(_expert_kernel, bm),
        grid_spec=pltpu.PrefetchScalarGridSpec(
            num_scalar_prefetch=2,
            grid=(n_blocks,),
            in_specs=[pl.BlockSpec((bm, D // 2), x_map),
                      pl.BlockSpec((1, D, D_EXPERT), w_map),
                      pl.BlockSpec((1, D, D_EXPERT), w_map),
                      pl.BlockSpec((1, D_EXPERT, D), w_map)],
            out_specs=pl.BlockSpec((bm, D), lambda j, be, nu: (j, 0)),
            scratch_shapes=[pltpu.VMEM((D, D_EXPERT), BF16),
                            pltpu.VMEM((D, D_EXPERT), BF16),
                            pltpu.VMEM((D_EXPERT, D), BF16),
                            pltpu.VMEM((bm, D_EXPERT), BF16)],
        ),
        out_shape=jax.ShapeDtypeStruct((n_rows, D), F32),
        compiler_params=pltpu.CompilerParams(
            dimension_semantics=("arbitrary",), vmem_limit_bytes=VMEM_LIMIT),
        name="experts",
    )(block_expert, n_used, xs, w_gate, w_up, w_down)


def _final_kernel(tk, n_i, alpha, d1_ref, d2_ref, x1_ref, rt_ref, mod_ref, l2g_ref, l2b_ref, ys_hbm,
                  out_ref, ybuf, sem):
    step = pl.program_id(0) * n_i + pl.program_id(1)
    n_steps = pl.num_programs(0) * n_i

    def row_copy(slot, row, src):
        return pltpu.make_async_copy(ys_hbm.at[pl.ds(src, 1)], ybuf.at[slot, pl.ds(row, 1)], sem.at[slot])

    def for_tile(st, slot, fn):
        def body(r, carry):
            t = st * tk + r
            fn(row_copy(slot, r, d1_ref[t]))
            fn(row_copy(slot, tk + r, d2_ref[t]))
            return carry
        lax.fori_loop(0, tk, body, 0, unroll=8)

    @pl.when(step == 0)
    def _():
        for_tile(0, 0, lambda cp: cp.start())

    @pl.when(step + 1 < n_steps)
    def _():
        for_tile(step + 1, (step + 1) % 2, lambda cp: cp.start())

    slot = step % 2
    for_tile(step, slot, lambda cp: cp.wait())

    g1 = rt_ref[:, 2:3]
    g2 = rt_ref[:, 3:4]
    f = ybuf[slot, 0:tk, :] * g1 + ybuf[slot, tk:2 * tk, :] * g2
    gate2 = mod_ref[0][5:6]
    y = alpha * x1_ref[0] + gate2 * f
    out_ref[0] = _ln(y) * l2g_ref[...] + l2b_ref[...]


def _final(x1, rt, mod, ln2_g, ln2_b, ys, dest1, dest2, b_off, nbg, alpha, tk):
    s_len = x1.shape[1]
    n_i = s_len // tk
    row_off = b_off * n_i
    return pl.pallas_call(
        functools.partial(_final_kernel, tk, n_i, alpha),
        grid_spec=pltpu.PrefetchScalarGridSpec(
            num_scalar_prefetch=2,
            grid=(nbg, n_i),
            in_specs=[pl.BlockSpec((1, tk, D), lambda b, i, *_: (b + b_off, i, 0)),
                      pl.BlockSpec((tk, ROUTER_LANES), lambda b, i, *_: (row_off + b * n_i + i, 0)),
                      pl.BlockSpec((1, N_MOD, D), lambda b, i, *_: (b + b_off, 0, 0)),
                      pl.BlockSpec((1, D), lambda b, i, *_: (0, 0)),
                      pl.BlockSpec((1, D), lambda b, i, *_: (0, 0)),
                      pl.BlockSpec(memory_space=pl.ANY)],
            out_specs=pl.BlockSpec((1, tk, D), lambda b, i, *_: (b, i, 0)),
            scratch_shapes=[pltpu.VMEM((2, 2 * tk, D), F32),
                            pltpu.SemaphoreType.DMA((2,))],
        ),
        out_shape=jax.ShapeDtypeStruct((nbg, s_len, D), F32),
        compiler_params=pltpu.CompilerParams(
            dimension_semantics=("arbitrary", "arbitrary"), vmem_limit_bytes=VMEM_LIMIT),
        name="final",
    )(dest1, dest2, x1, rt, mod, ln2_g, ln2_b, ys)


def _layer(x_p, x_s, c_all, p, alpha):
    nbp, s_len, _ = x_p.shape
    nbs = x_s.shape[0]
    nb = nbp + nbs
    n_tok = nb * s_len

    mod = _modulation(c_all, p["w_mod"], p["b_mod"]).reshape(nb, N_MOD, D)

    bsf = jnp.repeat(p["b_spatial"].T, HEAD_DIM, axis=1)
    n_in = p["w_in"].shape[1]
    w_router = jnp.concatenate(
        [p["w_router_group"], p["w_router_expert"],
         jnp.zeros((D, ROUTER_LANES - N_GROUPS - N_EXPERTS), F32)], axis=1)
    b_router = jnp.concatenate(
        [p["b_router_group"], p["b_router_expert"],
         jnp.zeros((ROUTER_LANES - N_GROUPS - N_EXPERTS,), F32)]).reshape(1, ROUTER_LANES)
    wts = [
        p["w_in"].astype(BF16), p["b_in"].reshape(1, n_in),
        p["w_spatial"].astype(BF16), bsf,
        p["sgu_g"].reshape(1, D), p["sgu_b"].reshape(1, D),
        p["w_pool"].astype(BF16), p["pool_scale"].reshape(1, D),
        p["w_branch_a"].astype(BF16), p["w_branch_b"].astype(BF16), p["w_out"].astype(BF16),
        p["ln1_g"].reshape(1, D), p["ln1_b"].reshape(1, D),
        w_router.astype(BF16), b_router,
    ]
    x1, h2u, rt, cnt = _mixer(x_p, x_s, mod, wts, alpha, MIXER_TS)

    bm = EXPERT_BM
    counts = cnt[0, EXPERT_LANE0:EXPERT_LANE0 + N_EXPERTS].astype(jnp.int32)
    padded = (counts + bm - 1) // bm * bm
    seg_end = jnp.cumsum(padded).astype(jnp.int32)
    seg_start = seg_end - padded
    e1 = rt[:, 0].astype(jnp.int32)
    e2 = rt[:, 1].astype(jnp.int32)
    dest1 = seg_start[e1] + rt[:, 4].astype(jnp.int32)
    dest2 = seg_start[e2] + rt[:, 5].astype(jnp.int32)
    n_blocks = -(-(2 * n_tok) // bm) + N_EXPERTS
    block_start = jnp.arange(n_blocks, dtype=jnp.int32) * bm
    block_expert = jnp.minimum(
        jnp.searchsorted(seg_end, block_start, side="right"), N_EXPERTS - 1).astype(jnp.int32)
    n_used = (seg_end[-1:] // bm).astype(jnp.int32)

    xs = _dispatch(h2u, dest1, dest2, n_blocks * bm, DISPATCH_TD)
    ys = _experts(xs, block_expert, n_used, p["w_exp_gate"], p["w_exp_up"], p["w_exp_down"], bm)

    l2g = p["ln2_g"].reshape(1, D)
    l2b = p["ln2_b"].reshape(1, D)
    tp = nbp * s_len
    y_p = _final(x1, rt, mod, l2g, l2b, ys, dest1[:tp], dest2[:tp], 0, nbp, alpha, FINAL_TK)
    y_s = _final(x1, rt, mod, l2g, l2b, ys, dest1[tp:], dest2[tp:], nbp, nbs, alpha, FINAL_TK)
    return y_p, y_s


_PARAM_NAMES = ("w_mod", "b_mod", "w_in", "b_in", "w_spatial", "b_spatial", "sgu_g", "sgu_b", "w_pool",
                "pool_scale", "w_branch_a", "w_branch_b", "w_out", "ln1_g", "ln1_b", "w_router_group",
                "b_router_group", "w_router_expert", "b_router_expert", "w_exp_gate", "w_exp_up",
                "w_exp_down", "ln2_g", "ln2_b")


def kernel(x_prompt, x_sample, c_prompt, c_sample, w_mod, b_mod, w_in, b_in, w_spatial, b_spatial, sgu_g, sgu_b, w_pool, pool_scale, w_branch_a, w_branch_b, w_out, ln1_g, ln1_b, w_router_group, b_router_group, w_router_expert, b_router_expert, w_exp_gate, w_exp_up, w_exp_down, ln2_g, ln2_b):
    params = (w_mod, b_mod, w_in, b_in, w_spatial, b_spatial, sgu_g, sgu_b, w_pool, pool_scale,
              w_branch_a, w_branch_b, w_out, ln1_g, ln1_b, w_router_group, b_router_group,
              w_router_expert, b_router_expert, w_exp_gate, w_exp_up, w_exp_down, ln2_g, ln2_b)
    depth = w_mod.shape[0]
    alpha = (2.0 * depth) ** 0.25
    c_all = jnp.concatenate([c_prompt, c_sample], axis=0)
    y_p, y_s = x_prompt, x_sample
    for l in range(depth):
        p = {name: w[l] for name, w in zip(_PARAM_NAMES, params)}
        y_p, y_s = _layer(y_p, y_s, c_all, p, alpha)
    return (y_p, y_s)
```

```python
import functools

import jax
import jax.numpy as jnp
import numpy as np
from jax import lax
from jax.experimental import pallas as pl
from jax.experimental.pallas import tpu as pltpu

F32 = jnp.float32
BF16 = jnp.bfloat16
U32 = jnp.uint32

D = 1024
CHUNK = 128
SGU_HEADS = 8
HEAD_DIM = D // SGU_HEADS
POOL_WINDOWS = (2, 4, 8, 16)
POOL_GROUP_DIM = D // len(POOL_WINDOWS)
N_MOD = 6
N_GROUPS = 4
EPG = 8
N_EXPERTS = N_GROUPS * EPG
D_EXPERT = D // 2
LN_EPS = 1e-5

HALO = 16
COLB = 256
ROUTER_LANES = 128
EXPERT_LANE0 = N_GROUPS
PAIRS = EPG * (EPG - 1) // 2
N_CLASSES = N_GROUPS * PAIRS
ROW_W = D // 2 + ROUTER_LANES
MIXER_TS = 256
EXPERT_BM = 256
DISPATCH_TD = 1024
FINAL_TK = 256
DEST_LANES = 512
VMEM_LIMIT = 56 * 1024 * 1024


def _ln(x):
    mu = jnp.mean(x, axis=-1, keepdims=True)
    xc = x - mu
    var = jnp.mean(xc * xc, axis=-1, keepdims=True)
    return xc * lax.rsqrt(var + LN_EPS)


def _dot(a, b):
    return jnp.dot(a, b, preferred_element_type=F32)


def _mod_kernel(c_ref, w_ref, b_ref, o_ref):
    a = jax.nn.silu(c_ref[...]).astype(BF16)
    o_ref[...] = _dot(a, w_ref[...].astype(BF16)) + b_ref[...]


def _modulation(c_all, w_mod, b_mod):
    nb = c_all.shape[0]
    n_out = w_mod.shape[1]
    cb = 512
    return pl.pallas_call(
        _mod_kernel,
        grid=(n_out // cb,),
        in_specs=[pl.BlockSpec((nb, D), lambda j: (0, 0)),
                  pl.BlockSpec((D, cb), lambda j: (0, j)),
                  pl.BlockSpec((1, cb), lambda j: (0, j))],
        out_specs=pl.BlockSpec((nb, cb), lambda j: (0, j)),
        out_shape=jax.ShapeDtypeStruct((nb, n_out), F32),
        name="mod",
    )(c_all, w_mod, b_mod.reshape(1, n_out))


def _mixer_kernel(nbp, ts, s_len, alpha,
                  xp_ref, xs_ref, xpp_ref, xsp_ref, xpn_ref, xsn_ref, mod_ref,
                  win_ref, bin_ref, ws_ref, bsf_ref, sg_ref, sb_ref, wpool_ref, psc_ref,
                  wa_ref, wb_ref, wo_ref, l1g_ref, l1b_ref, wr_ref, br_ref, tri_ref,
                  x1_ref, h2u_ref, cr_ref, cnt_ref,
                  h_scr, u_scr, v_scr, vb_scr, a_scr, p_scr, m_scr, h2_scr, carry_scr):
    b = pl.program_id(0)
    i = pl.program_id(1)
    is_p = b < nbp
    ncb = D // COLB

    @pl.when((b == 0) & (i == 0))
    def _():
        carry_scr[...] = jnp.zeros_like(carry_scr)

    mod = mod_ref[0]
    shift1, scale1, gate1 = mod[0:1], mod[1:2], mod[2:3]
    shift2, scale2 = mod[3:4], mod[4:5]

    def adaln1(xv):
        return (_ln(xv) * (1.0 + scale1) + shift1).astype(BF16)

    h_scr[0:HALO, :] = adaln1(jnp.where(is_p, xpp_ref[0], xsp_ref[0]))
    h_scr[HALO:HALO + ts, :] = adaln1(jnp.where(is_p, xp_ref[0], xs_ref[0]))
    h_scr[HALO + ts:, :] = adaln1(jnp.where(is_p, xpn_ref[0], xsn_ref[0]))

    def proj(rows, c0):
        return _dot(h_scr[rows, :], win_ref[:, c0:c0 + COLB]) + bin_ref[:, c0:c0 + COLB]

    main = slice(HALO, HALO + ts)

    vsum = jnp.zeros((ts, 1), F32)
    for j in range(ncb):
        cols = slice(j * COLB, (j + 1) * COLB)
        u_scr[:, cols] = jax.nn.gelu(proj(main, j * COLB))
        gv = jax.nn.gelu(proj(main, D + j * COLB))
        v_scr[:, cols] = gv
        vsum = vsum + jnp.sum(gv, axis=-1, keepdims=True)
    vmu = vsum * (1.0 / D)
    vss = jnp.zeros((ts, 1), F32)
    for j in range(ncb):
        cols = slice(j * COLB, (j + 1) * COLB)
        xc = v_scr[:, cols] - vmu
        vss = vss + jnp.sum(xc * xc, axis=-1, keepdims=True)
    vrs = lax.rsqrt(vss * (1.0 / D) + LN_EPS)
    for j in range(ncb):
        cols = slice(j * COLB, (j + 1) * COLB)
        vb_scr[:, cols] = ((v_scr[:, cols] - vmu) * vrs * sg_ref[:, cols] + sb_ref[:, cols]).astype(BF16)

    for c in range(ts // CHUNK):
        rows = slice(c * CHUNK, (c + 1) * CHUNK)
        for hh in range(SGU_HEADS):
            cols = slice(hh * HEAD_DIM, (hh + 1) * HEAD_DIM)
            mixed = _dot(ws_ref[hh], vb_scr[rows, cols]) + bsf_ref[:, cols]
            a_scr[rows, cols] = (u_scr[rows, cols] * mixed).astype(BF16)

    n_ext = ts + 2 * HALO
    ext_pos = lax.broadcasted_iota(jnp.int32, (n_ext, 1), 0) + (i * ts - HALO)
    ext_valid = (ext_pos >= 0) & (ext_pos < s_len)
    pos = lax.broadcasted_iota(jnp.int32, (ts, 1), 0) + i * ts
    for gi, w in enumerate(POOL_WINDOWS):
        cols = slice(gi * POOL_GROUP_DIM, (gi + 1) * POOL_GROUP_DIM)
        zp = jnp.where(ext_valid, proj(slice(None), 2 * D + gi * POOL_GROUP_DIM), 0.0)
        acc = zp + pltpu.roll(zp, 1, 0)
        if w >= 4:
            acc = pltpu.roll(acc, 1, 0) + pltpu.roll(acc, n_ext - 1, 0)
        if w >= 8:
            acc = pltpu.roll(acc, 2, 0) + pltpu.roll(acc, n_ext - 2, 0)
        if w >= 16:
            acc = pltpu.roll(acc, 4, 0) + pltpu.roll(acc, n_ext - 4, 0)
        lo = jnp.maximum(pos - w // 2, 0)
        hi = jnp.minimum(pos + (w - 1 - w // 2), s_len - 1)
        inv_count = 1.0 / (hi - lo + 1).astype(F32)
        dd = (acc[main] * inv_count - zp[main]).astype(BF16)
        p_scr[:, cols] = (_dot(dd, wpool_ref[gi]) * psc_ref[:, cols]).astype(BF16)

    for j in range(ncb):
        cols = slice(j * COLB, (j + 1) * COLB)
        ga = jax.nn.sigmoid(proj(main, 3 * D + j * COLB))
        gb = jax.nn.sigmoid(proj(main, 4 * D + j * COLB))
        ta = _dot(a_scr[...], wa_ref[:, cols])
        tb = _dot(p_scr[...], wb_ref[:, cols])
        m_scr[:, cols] = (ga * ta + gb * tb).astype(BF16)

    ysum = jnp.zeros((ts, 1), F32)
    for j in range(ncb):
        cols = slice(j * COLB, (j + 1) * COLB)
        t = _dot(m_scr[...], wo_ref[:, cols])
        xv = jnp.where(is_p, xp_ref[0, :, cols], xs_ref[0, :, cols])
        y = alpha * xv + gate1[:, cols] * t
        v_scr[:, cols] = y
        ysum = ysum + jnp.sum(y, axis=-1, keepdims=True)
    ymu = ysum * (1.0 / D)
    yss = jnp.zeros((ts, 1), F32)
    for j in range(ncb):
        cols = slice(j * COLB, (j + 1) * COLB)
        xc = v_scr[:, cols] - ymu
        yss = yss + jnp.sum(xc * xc, axis=-1, keepdims=True)
    yrs = lax.rsqrt(yss * (1.0 / D) + LN_EPS)
    xsum = jnp.zeros((ts, 1), F32)
    for j in range(ncb):
        cols = slice(j * COLB, (j + 1) * COLB)
        x1 = (v_scr[:, cols] - ymu) * yrs * l1g_ref[:, cols] + l1b_ref[:, cols]
        x1_ref[0, :, cols] = x1
        v_scr[:, cols] = x1
        xsum = xsum + jnp.sum(x1, axis=-1, keepdims=True)

    xmu = xsum * (1.0 / D)
    xss = jnp.zeros((ts, 1), F32)
    for j in range(ncb):
        cols = slice(j * COLB, (j + 1) * COLB)
        xc = v_scr[:, cols] - xmu
        xss = xss + jnp.sum(xc * xc, axis=-1, keepdims=True)
    xrs = lax.rsqrt(xss * (1.0 / D) + LN_EPS)
    for j in range(ncb):
        cols = slice(j * COLB, (j + 1) * COLB)
        h2 = (v_scr[:, cols] - xmu) * xrs * (1.0 + scale2[:, cols]) + shift2[:, cols]
        h2_scr[:, cols] = h2.astype(BF16)
    half = D // 2
    lo_bits = lax.bitcast_convert_type(h2_scr[:, 0:half].astype(F32), U32)
    hi_bits = lax.bitcast_convert_type(h2_scr[:, half:D].astype(F32), U32)
    h2u_ref[:, 0:half] = (lo_bits >> 16) | (hi_bits & jnp.uint32(0xFFFF0000))

    logits = _dot(h2_scr[...], wr_ref[...]) + br_ref[...]
    lane = lax.broadcasted_iota(jnp.int32, (ts, ROUTER_LANES), 1)
    lane_f = lane.astype(F32)
    neg = -jnp.inf
    is_g = lane < N_GROUPS
    lg = jnp.where(is_g, logits, neg)
    mg = jnp.max(lg, axis=-1, keepdims=True)
    g_sel = jnp.min(jnp.where(lg == mg, lane_f, float(ROUTER_LANES)), axis=-1, keepdims=True)
    p_sel = 1.0 / jnp.sum(jnp.where(is_g, jnp.exp(logits - mg), 0.0), axis=-1, keepdims=True)
    e_lo = EXPERT_LANE0 + g_sel * EPG
    in_grp = (lane_f >= e_lo) & (lane_f < e_lo + EPG)
    le = jnp.where(in_grp, logits, neg)
    m1 = jnp.max(le, axis=-1, keepdims=True)
    i1 = jnp.min(jnp.where(le == m1, lane_f, float(ROUTER_LANES)), axis=-1, keepdims=True)
    le2 = jnp.where(lane_f == i1, neg, le)
    m2 = jnp.max(le2, axis=-1, keepdims=True)
    i2 = jnp.min(jnp.where(le2 == m2, lane_f, float(ROUTER_LANES)), axis=-1, keepdims=True)
    e2x = jnp.exp(m2 - m1)
    den = 1.0 + e2x
    g1 = p_sel / den
    g2 = p_sel * e2x / den

    j1 = i1 - e_lo
    j2 = i2 - e_lo
    first_is_a = j1 < j2
    ja = jnp.minimum(j1, j2)
    jb = jnp.maximum(j1, j2)
    cls = g_sel * PAIRS + (EPG - 1) * ja - ja * (ja - 1.0) * 0.5 + (jb - ja - 1.0)
    w_a = jnp.where(first_is_a, g1, g2)
    w_b = jnp.where(first_is_a, g2, g1)
    extra = jnp.where(lane == 0, w_a, jnp.where(lane == 1, w_b, 0.0))
    h2u_ref[:, half:half + ROUTER_LANES] = lax.bitcast_convert_type(extra, U32)

    hit = lane_f == cls
    onehot = jnp.where(hit, 1.0, 0.0)
    pre = _dot(tri_ref[...], onehot.astype(BF16)) + carry_scr[0:1, :]
    rank = jnp.sum(jnp.where(hit, pre, 0.0), axis=-1, keepdims=True)
    carry_scr[0:1, :] = carry_scr[0:1, :] + jnp.sum(onehot, axis=0, keepdims=True)
    cnt_ref[...] = carry_scr[...]

    cr = jnp.where(lane == 0, cls, jnp.where(lane == 1, rank, 0.0))
    cr_ref[...] = cr.T[0:8, :]


def _const_spec(shape):
    nd = len(shape)
    return pl.BlockSpec(shape, lambda b, i: (0,) * nd, pipeline_mode=pl.Buffered(1))


def _mixer(x_p, x_s, mod, wts, alpha, ts):
    nbp, s_len, _ = x_p.shape
    nbs = x_s.shape[0]
    assert x_s.shape[1] == s_len and s_len % ts == 0 and ts % CHUNK == 0
    nb = nbp + nbs
    n_i = s_len // ts
    hb = ts // HALO
    n_hb = s_len // HALO

    def pick(b, on_p, val, const):
        return jnp.where(b < nbp if on_p else b >= nbp, val, const)

    def main_map(on_p):
        def f(b, i):
            bb = pick(b, on_p, b if on_p else b - nbp, nbp - 1 if on_p else 0)
            ii = pick(b, on_p, i, n_i - 1 if on_p else 0)
            return (bb, ii, 0)
        return f

    def halo_map(on_p, nxt):
        def f(b, i):
            bb = pick(b, on_p, b if on_p else b - nbp, nbp - 1 if on_p else 0)
            idx = jnp.minimum((i + 1) * hb, n_hb - 1) if nxt else jnp.maximum(i * hb - 1, 0)
            ii = pick(b, on_p, idx, n_hb - 1 if on_p else 0)
            return (bb, ii, 0)
        return f

    tri = (lax.broadcasted_iota(jnp.int32, (ts, ts), 1)
           < lax.broadcasted_iota(jnp.int32, (ts, ts), 0)).astype(BF16)
    consts = list(wts) + [tri]
    in_specs = [
        pl.BlockSpec((1, ts, D), main_map(True)),
        pl.BlockSpec((1, ts, D), main_map(False)),
        pl.BlockSpec((1, HALO, D), halo_map(True, False)),
        pl.BlockSpec((1, HALO, D), halo_map(False, False)),
        pl.BlockSpec((1, HALO, D), halo_map(True, True)),
        pl.BlockSpec((1, HALO, D), halo_map(False, True)),
        pl.BlockSpec((1, N_MOD, D), lambda b, i: (b, 0, 0)),
    ] + [_const_spec(w.shape) for w in consts]
    n_tok = nb * s_len
    out_shape = (
        jax.ShapeDtypeStruct((nb, s_len, D), F32),
        jax.ShapeDtypeStruct((n_tok, ROW_W), U32),
        jax.ShapeDtypeStruct((8, n_tok), F32),
        jax.ShapeDtypeStruct((8, ROUTER_LANES), F32),
    )
    out_specs = (
        pl.BlockSpec((1, ts, D), lambda b, i: (b, i, 0)),
        pl.BlockSpec((ts, ROW_W), lambda b, i: (b * n_i + i, 0)),
        pl.BlockSpec((8, ts), lambda b, i: (0, b * n_i + i)),
        pl.BlockSpec((8, ROUTER_LANES), lambda b, i: (0, 0)),
    )
    scratch = [
        pltpu.VMEM((ts + 2 * HALO, D), BF16),
        pltpu.VMEM((ts, D), F32),
        pltpu.VMEM((ts, D), F32),
        pltpu.VMEM((ts, D), BF16),
        pltpu.VMEM((ts, D), BF16),
        pltpu.VMEM((ts, D), BF16),
        pltpu.VMEM((ts, D), BF16),
        pltpu.VMEM((ts, D), BF16),
        pltpu.VMEM((8, ROUTER_LANES), F32),
    ]
    return pl.pallas_call(
        functools.partial(_mixer_kernel, nbp, ts, s_len, alpha),
        grid=(nb, n_i),
        in_specs=in_specs,
        out_specs=out_specs,
        out_shape=out_shape,
        scratch_shapes=scratch,
        compiler_params=pltpu.CompilerParams(
            dimension_semantics=("arbitrary", "arbitrary"), vmem_limit_bytes=VMEM_LIMIT),
        name="mixer",
    )(x_p, x_s, x_p, x_s, x_p, x_s, mod, *consts)


def _dest_kernel(seg_ref, cls_ref, rank_ref, o_ref):
    cls = cls_ref[...]
    acc = rank_ref[...].astype(jnp.int32)
    for c in range(N_CLASSES):
        acc = acc + jnp.where(cls == float(c), seg_ref[c], 0)
    o_ref[...] = acc


def _dest(seg_start, cls, rank):
    shape = cls.shape
    return pl.pallas_call(
        _dest_kernel,
        grid_spec=pltpu.PrefetchScalarGridSpec(
            num_scalar_prefetch=1,
            grid=(1,),
            in_specs=[pl.BlockSpec(shape, lambda i, s: (0, 0)), pl.BlockSpec(shape, lambda i, s: (0, 0))],
            out_specs=pl.BlockSpec(shape, lambda i, s: (0, 0)),
        ),
        out_shape=jax.ShapeDtypeStruct(shape, jnp.int32),
        name="dest",
    )(seg_start, cls, rank)


def _dispatch_kernel(td, dest_ref, rows_ref, xs_init_hbm, xs_hbm, sem):
    del xs_init_hbm
    base = pl.program_id(0) * td

    def row_copy(r):
        return pltpu.make_async_copy(rows_ref.at[pl.ds(r, 1)], xs_hbm.at[pl.ds(dest_ref[base + r], 1)], sem)

    def start(r, carry):
        row_copy(r).start()
        return carry

    def wait(r, carry):
        row_copy(r).wait()
        return carry

    lax.fori_loop(0, td, start, 0, unroll=8)
    lax.fori_loop(0, td, wait, 0, unroll=8)


def _dispatch(rows, dest, n_rows, td):
    n_tok = rows.shape[0]
    assert n_tok % td == 0
    xs_init = jnp.zeros((n_rows, ROW_W), U32)
    return pl.pallas_call(
        functools.partial(_dispatch_kernel, td),
        grid_spec=pltpu.PrefetchScalarGridSpec(
            num_scalar_prefetch=1,
            grid=(n_tok // td,),
            in_specs=[pl.BlockSpec((td, ROW_W), lambda s, d: (s, 0)), pl.BlockSpec(memory_space=pl.ANY)],
            out_specs=pl.BlockSpec(memory_space=pl.ANY),
            scratch_shapes=[pltpu.SemaphoreType.DMA(())],
        ),
        out_shape=jax.ShapeDtypeStruct((n_rows, ROW_W), U32),
        input_output_aliases={2: 0},
        compiler_params=pltpu.CompilerParams(dimension_semantics=("arbitrary",)),
        name="dispatch",
    )(dest, rows, xs_init)


def _expert_kernel(bm, ea_ref, eb_ref, nused_ref, xs_ref, wga_ref, wua_ref, wda_ref, wgb_ref, wub_ref, wdb_ref,
                   ys_ref, act_scr):
    j = pl.program_id(0)
    used = j < nused_ref[0]
    half = D // 2

    @pl.when(used)
    def _():
        bits = xs_ref[:, 0:half]
        lo = lax.bitcast_convert_type(bits << 16, F32).astype(BF16)
        hi = lax.bitcast_convert_type(bits & jnp.uint32(0xFFFF0000), F32).astype(BF16)
        wts = lax.bitcast_convert_type(xs_ref[:, half:half + ROUTER_LANES], F32)
        for e, (wg_ref, wu_ref) in enumerate(((wga_ref, wua_ref), (wgb_ref, wub_ref))):
            for c in range(D_EXPERT // COLB):
                cols = slice(c * COLB, (c + 1) * COLB)
                g = _dot(lo, wg_ref[0, 0:half, cols]) + _dot(hi, wg_ref[0, half:D, cols])
                up = _dot(lo, wu_ref[0, 0:half, cols]) + _dot(hi, wu_ref[0, half:D, cols])
                act_scr[e, :, cols] = (jax.nn.silu(g) * up).astype(BF16)
        w_a = wts[:, 0:1]
        w_b = wts[:, 1:2]
        for c in range(D // COLB):
            cols = slice(c * COLB, (c + 1) * COLB)
            ys_ref[:, cols] = (w_a * _dot(act_scr[0], wda_ref[0, :, cols])
                               + w_b * _dot(act_scr[1], wdb_ref[0, :, cols]))

    @pl.when(jnp.logical_not(used))
    def _():
        ys_ref[...] = jnp.zeros_like(ys_ref)


def _experts(xs, blk_a, blk_b, n_used, w_gate, w_up, w_down, bm):
    n_rows = xs.shape[0]
    n_blocks = n_rows // bm

    def x_map(j, ea, eb, nu):
        return (jnp.minimum(j, nu[0] - 1), 0)

    def a_map(j, ea, eb, nu):
        return (ea[j], 0, 0)

    def b_map(j, ea, eb, nu):
        return (eb[j], 0, 0)

    return pl.pallas_call(
        functools.partial(_expert_kernel, bm),
        grid_spec=pltpu.PrefetchScalarGridSpec(
            num_scalar_prefetch=3,
            grid=(n_blocks,),
            in_specs=[pl.BlockSpec((bm, ROW_W), x_map),
                      pl.BlockSpec((1, D, D_EXPERT), a_map),
                      pl.BlockSpec((1, D, D_EXPERT), a_map),
                      pl.BlockSpec((1, D_EXPERT, D), a_map),
                      pl.BlockSpec((1, D, D_EXPERT), b_map),
                      pl.BlockSpec((1, D, D_EXPERT), b_map),
                      pl.BlockSpec((1, D_EXPERT, D), b_map)],
            out_specs=pl.BlockSpec((bm, D), lambda j, ea, eb, nu: (j, 0)),
            scratch_shapes=[pltpu.VMEM((2, bm, D_EXPERT), BF16)],
        ),
        out_shape=jax.ShapeDtypeStruct((n_rows, D), F32),
        compiler_params=pltpu.CompilerParams(
            dimension_semantics=("arbitrary",), vmem_limit_bytes=VMEM_LIMIT),
        name="experts",
    )(blk_a, blk_b, n_used, xs, w_gate, w_up, w_down, w_gate, w_up, w_down)


def _final_kernel(tk, n_i, alpha, dest_ref, x1_ref, mod_ref, l2g_ref, l2b_ref, ys_hbm,
                  out_ref, ybuf, sem):
    step = pl.program_id(0) * n_i + pl.program_id(1)
    n_steps = pl.num_programs(0) * n_i

    def for_tile(st, slot, fn):
        def body(r, carry):
            src = dest_ref[st * tk + r]
            fn(pltpu.make_async_copy(ys_hbm.at[pl.ds(src, 1)], ybuf.at[slot, pl.ds(r, 1)], sem.at[slot]))
            return carry
        lax.fori_loop(0, tk, body, 0, unroll=8)

    @pl.when(step == 0)
    def _():
        for_tile(0, 0, lambda cp: cp.start())

    @pl.when(step + 1 < n_steps)
    def _():
        for_tile(step + 1, (step + 1) % 2, lambda cp: cp.start())

    slot = step % 2
    for_tile(step, slot, lambda cp: cp.wait())

    gate2 = mod_ref[0][5:6]
    y = alpha * x1_ref[0] + gate2 * ybuf[slot]
    out_ref[0] = _ln(y) * l2g_ref[...] + l2b_ref[...]


def _final(x1, mod, ln2_g, ln2_b, ys, dest, b_off, nbg, alpha, tk):
    s_len = x1.shape[1]
    n_i = s_len // tk
    return pl.pallas_call(
        functools.partial(_final_kernel, tk, n_i, alpha),
        grid_spec=pltpu.PrefetchScalarGridSpec(
            num_scalar_prefetch=1,
            grid=(nbg, n_i),
            in_specs=[pl.BlockSpec((1, tk, D), lambda b, i, d: (b + b_off, i, 0)),
                      pl.BlockSpec((1, N_MOD, D), lambda b, i, d: (b + b_off, 0, 0)),
                      pl.BlockSpec((1, D), lambda b, i, d: (0, 0)),
                      pl.BlockSpec((1, D), lambda b, i, d: (0, 0)),
                      pl.BlockSpec(memory_space=pl.ANY)],
            out_specs=pl.BlockSpec((1, tk, D), lambda b, i, d: (b, i, 0)),
            scratch_shapes=[pltpu.VMEM((2, tk, D), F32),
                            pltpu.SemaphoreType.DMA((2,))],
        ),
        out_shape=jax.ShapeDtypeStruct((nbg, s_len, D), F32),
        compiler_params=pltpu.CompilerParams(
            dimension_semantics=("arbitrary", "arbitrary"), vmem_limit_bytes=VMEM_LIMIT),
        name="final",
    )(dest, x1, mod, ln2_g, ln2_b, ys)


def _layer(x_p, x_s, c_all, p, alpha):
    nbp, s_len, _ = x_p.shape
    nbs = x_s.shape[0]
    nb = nbp + nbs
    n_tok = nb * s_len

    mod = _modulation(c_all, p["w_mod"], p["b_mod"]).reshape(nb, N_MOD, D)

    bsf = jnp.repeat(p["b_spatial"].T, HEAD_DIM, axis=1)
    n_in = p["w_in"].shape[1]
    w_router = jnp.concatenate(
        [p["w_router_group"], p["w_router_expert"],
         jnp.zeros((D, ROUTER_LANES - N_GROUPS - N_EXPERTS), F32)], axis=1)
    b_router = jnp.concatenate(
        [p["b_router_group"], p["b_router_expert"],
         jnp.zeros((ROUTER_LANES - N_GROUPS - N_EXPERTS,), F32)]).reshape(1, ROUTER_LANES)
    wts = [
        p["w_in"].astype(BF16), p["b_in"].reshape(1, n_in),
        p["w_spatial"].astype(BF16), bsf,
        p["sgu_g"].reshape(1, D), p["sgu_b"].reshape(1, D),
        p["w_pool"].astype(BF16), p["pool_scale"].reshape(1, D),
        p["w_branch_a"].astype(BF16), p["w_branch_b"].astype(BF16), p["w_out"].astype(BF16),
        p["ln1_g"].reshape(1, D), p["ln1_b"].reshape(1, D),
        w_router.astype(BF16), b_router,
    ]
    x1, rows, cr, cnt = _mixer(x_p, x_s, mod, wts, alpha, MIXER_TS)

    bm = EXPERT_BM
    assert n_tok % bm == 0 and n_tok % DEST_LANES == 0
    counts = cnt[0, :N_CLASSES].astype(jnp.int32)
    padded = (counts + bm - 1) // bm * bm
    seg_end = jnp.cumsum(padded).astype(jnp.int32)
    seg_start = seg_end - padded
    n_blocks = n_tok // bm + N_CLASSES
    block_start = jnp.arange(n_blocks, dtype=jnp.int32) * bm
    blk_cls = jnp.minimum(
        jnp.sum((seg_end[None, :] <= block_start[:, None]).astype(jnp.int32), axis=1), N_CLASSES - 1)
    cls_a, cls_b = _class_experts()
    blk_a = jnp.asarray(cls_a)[blk_cls]
    blk_b = jnp.asarray(cls_b)[blk_cls]
    n_used = (seg_end[-1:] // bm).astype(jnp.int32)

    lane_shape = (n_tok // DEST_LANES, DEST_LANES)
    dest = _dest(seg_start, cr[0].reshape(lane_shape), cr[1].reshape(lane_shape)).reshape(n_tok)

    xs = _dispatch(rows, dest, n_blocks * bm, DISPATCH_TD)
    ys = _experts(xs, blk_a, blk_b, n_used,
                  p["w_exp_gate"].astype(BF16), p["w_exp_up"].astype(BF16), p["w_exp_down"].astype(BF16), bm)

    l2g = p["ln2_g"].reshape(1, D)
    l2b = p["ln2_b"].reshape(1, D)
    tp = nbp * s_len
    y_p = _final(x1, mod, l2g, l2b, ys, dest[:tp], 0, nbp, alpha, FINAL_TK)
    y_s = _final(x1, mod, l2g, l2b, ys, dest[tp:], nbp, nbs, alpha, FINAL_TK)
    return y_p, y_s


def _class_experts():
    ea, eb = [], []
    for g in range(N_GROUPS):
        for a in range(EPG):
            for b in range(a + 1, EPG):
                ea.append(g * EPG + a)
                eb.append(g * EPG + b)
    return np.asarray(ea, np.int32), np.asarray(eb, np.int32)


_PARAM_NAMES = ("w_mod", "b_mod", "w_in", "b_in", "w_spatial", "b_spatial", "sgu_g", "sgu_b", "w_pool",
                "pool_scale", "w_branch_a", "w_branch_b", "w_out", "ln1_g", "ln1_b", "w_router_group",
                "b_router_group", "w_router_expert", "b_router_expert", "w_exp_gate", "w_exp_up",
                "w_exp_down", "ln2_g", "ln2_b")


def kernel(x_prompt, x_sample, c_prompt, c_sample, w_mod, b_mod, w_in, b_in, w_spatial, b_spatial, sgu_g, sgu_b, w_pool, pool_scale, w_branch_a, w_branch_b, w_out, ln1_g, ln1_b, w_router_group, b_router_group, w_router_expert, b_router_expert, w_exp_gate, w_exp_up, w_exp_down, ln2_g, ln2_b):
    params = (w_mod, b_mod, w_in, b_in, w_spatial, b_spatial, sgu_g, sgu_b, w_pool, pool_scale,
              w_branch_a, w_branch_b, w_out, ln1_g, ln1_b, w_router_group, b_router_group,
              w_router_expert, b_router_expert, w_exp_gate, w_exp_up, w_exp_down, ln2_g, ln2_b)
    depth = w_mod.shape[0]
    alpha = (2.0 * depth) ** 0.25
    c_all = jnp.concatenate([c_prompt, c_sample], axis=0)
    y_p, y_s = x_prompt, x_sample
    for l in range(depth):
        p = {name: w[l] for name, w in zip(_PARAM_NAMES, params)}
        y_p, y_s = _layer(y_p, y_s, c_all, p, alpha)
    return (y_p, y_s)
```

```python
import functools
import math

import jax
import jax.numpy as jnp
import numpy as np
from jax import lax
from jax.experimental import pallas as pl
from jax.experimental.pallas import tpu as pltpu

F32 = jnp.float32
BF16 = jnp.bfloat16
U32 = jnp.uint32

D = 1024
CHUNK = 128
SGU_HEADS = 8
HEAD_DIM = D // SGU_HEADS
POOL_WINDOWS = (2, 4, 8, 16)
POOL_GROUP_DIM = D // len(POOL_WINDOWS)
N_MOD = 6
N_GROUPS = 4
EPG = 8
N_EXPERTS = N_GROUPS * EPG
D_EXPERT = D // 2
LN_EPS = 1e-5

HALO = 16
COLB = 256
ROUTER_LANES = 128
EXPERT_LANE0 = N_GROUPS
PAIRS = EPG * (EPG - 1) // 2
N_CLASSES = N_GROUPS * PAIRS
ROW_W = D // 2 + ROUTER_LANES
MIXER_TS = 512
MIXER_STREAMS = 1
MIXER_SKEW = 0
EXPERT_BM = 256
DISPATCH_TD = 1024
FINAL_TK = 256
DEST_LANES = 512
VMEM_LIMIT = 56 * 1024 * 1024


def _ln(x):
    mu = jnp.mean(x, axis=-1, keepdims=True)
    xc = x - mu
    var = jnp.mean(xc * xc, axis=-1, keepdims=True)
    return xc * lax.rsqrt(var + LN_EPS)


_GELU_A = -2.0 * math.sqrt(2.0 / math.pi) * math.log2(math.e)
_GELU_B = _GELU_A * 0.044715


def _gelu_tanh(x):
    return x / (1.0 + jnp.exp2(x * (_GELU_A + _GELU_B * (x * x))))


def _dot(a, b):
    return jnp.dot(a, b, preferred_element_type=F32)


def _mod_kernel(c_ref, w_ref, b_ref, o_ref):
    a = jax.nn.silu(c_ref[...]).astype(BF16)
    o_ref[...] = _dot(a, w_ref[...].astype(BF16)) + b_ref[...]


def _modulation(c_all, w_mod, b_mod):
    nb = c_all.shape[0]
    n_out = w_mod.shape[1]
    cb = 512
    return pl.pallas_call(
        _mod_kernel,
        grid=(n_out // cb,),
        in_specs=[pl.BlockSpec((nb, D), lambda j: (0, 0)),
                  pl.BlockSpec((D, cb), lambda j: (0, j)),
                  pl.BlockSpec((1, cb), lambda j: (0, j))],
        out_specs=pl.BlockSpec((nb, cb), lambda j: (0, j)),
        out_shape=jax.ShapeDtypeStruct((nb, n_out), F32),
        name="mod",
    )(c_all, w_mod, b_mod.reshape(1, n_out))


def _mixer_kernel(nbp, ts, s_len, alpha, n_streams, skew,
                   xp_ref, xs_ref, xpp_ref, xsp_ref, xpn_ref, xsn_ref, mod_ref, icnt_ref,
                   win_ref, bin_ref, ws_ref, bsf_ref, sg_ref, sb_ref, wpool_ref, psc_ref,
                   wa_ref, wb_ref, wo_ref, l1g_ref, l1b_ref, wr_ref, br_ref, tri_ref,
                   x1_ref, h2u_ref, cr_ref, cnt_ref,
                   h_scr, u_scr, v_scr, vb_scr, a_scr, p_scr, m_scr, h2_scr, carry_scr):
    b = pl.program_id(0)
    i = pl.program_id(1)
    is_p = b < nbp
    ncb = D // COLB
    hs = ts // n_streams
    n_ext = hs + 2 * HALO
    half = D // 2

    def wmat(ref, rows=slice(None), cols=slice(None)):
        return pltpu.bitcast(ref[rows, cols], BF16)

    @pl.when((b == 0) & (i == 0))
    def _():
        carry_scr[...] = jnp.zeros_like(carry_scr)

    mod = mod_ref[0]
    shift1, scale1, gate1 = mod[0:1], mod[1:2], mod[2:3]
    shift2, scale2 = mod[3:4], mod[4:5]

    def adaln1(xv):
        return (_ln(xv) * (1.0 + scale1) + shift1).astype(BF16)

    def x_rows(lo, hi, cols=slice(None)):
        return jnp.where(is_p, xp_ref[0, lo:hi, cols], xs_ref[0, lo:hi, cols])

    def stream(si):
        r0 = si * hs
        rows = slice(r0, r0 + hs)
        e0 = si * n_ext
        ext = slice(e0, e0 + n_ext)
        main = slice(e0 + HALO, e0 + HALO + hs)
        st = {}
        units = []

        def proj(hrows, c0):
            return _dot(h_scr[hrows, :], wmat(win_ref, cols=slice(c0, c0 + COLB))) + bin_ref[:, c0:c0 + COLB]

        def u_ln():
            top = jnp.where(is_p, xpp_ref[0], xsp_ref[0]) if si == 0 else x_rows(r0 - HALO, r0)
            bot = (jnp.where(is_p, xpn_ref[0], xsn_ref[0]) if si == n_streams - 1
                   else x_rows(r0 + hs, r0 + hs + HALO))
            h_scr[e0:e0 + HALO, :] = adaln1(top)
            h_scr[main, :] = adaln1(x_rows(r0, r0 + hs))
            h_scr[e0 + HALO + hs:e0 + n_ext, :] = adaln1(bot)
            st["vsum"] = jnp.zeros((hs, 1), F32)
        units.append(u_ln)

        def u_uv(j):
            def f():
                cols = slice(j * COLB, (j + 1) * COLB)
                u_scr[rows, cols] = _gelu_tanh(proj(main, j * COLB))
                gv = _gelu_tanh(proj(main, D + j * COLB))
                v_scr[rows, cols] = gv
                st["vsum"] = st["vsum"] + jnp.sum(gv, axis=-1, keepdims=True)
            return f

        def u_vln_a():
            st["vmu"] = st["vsum"] * (1.0 / D)
            vss = jnp.zeros((hs, 1), F32)
            for j in range(ncb):
                cols = slice(j * COLB, (j + 1) * COLB)
                xc = v_scr[rows, cols] - st["vmu"]
                vss = vss + jnp.sum(xc * xc, axis=-1, keepdims=True)
            st["vrs"] = lax.rsqrt(vss * (1.0 / D) + LN_EPS)

        def u_vln_b():
            for j in range(ncb):
                cols = slice(j * COLB, (j + 1) * COLB)
                vb_scr[rows, cols] = ((v_scr[rows, cols] - st["vmu"]) * st["vrs"] * sg_ref[:, cols]
                                      + sb_ref[:, cols]).astype(BF16)

        def u_sgu(c):
            def f():
                crow = slice(r0 + c * CHUNK, r0 + (c + 1) * CHUNK)
                for hh in range(SGU_HEADS):
                    cols = slice(hh * HEAD_DIM, (hh + 1) * HEAD_DIM)
                    mixed = _dot(ws_ref[hh], vb_scr[crow, cols]) + bsf_ref[:, cols]
                    a_scr[crow, cols] = (u_scr[crow, cols] * mixed).astype(BF16)
            return f

        def u_pool(gi):
            w = POOL_WINDOWS[gi]

            def f():
                cols = slice(gi * POOL_GROUP_DIM, (gi + 1) * POOL_GROUP_DIM)
                ext_pos = lax.broadcasted_iota(jnp.int32, (n_ext, 1), 0) + (i * ts + r0 - HALO)
                ext_valid = (ext_pos >= 0) & (ext_pos < s_len)
                zp = jnp.where(ext_valid, proj(ext, 2 * D + gi * POOL_GROUP_DIM), 0.0)
                acc = zp + pltpu.roll(zp, 1, 0)
                if w >= 4:
                    acc = pltpu.roll(acc, 1, 0) + pltpu.roll(acc, n_ext - 1, 0)
                if w >= 8:
                    acc = pltpu.roll(acc, 2, 0) + pltpu.roll(acc, n_ext - 2, 0)
                if w >= 16:
                    acc = pltpu.roll(acc, 4, 0) + pltpu.roll(acc, n_ext - 4, 0)
                inv_count = icnt_ref[rows, gi:gi + 1]
                dd = (acc[HALO:HALO + hs] * inv_count - zp[HALO:HALO + hs]).astype(BF16)
                p_scr[rows, cols] = (_dot(dd, pltpu.bitcast(wpool_ref[gi], BF16)) * psc_ref[:, cols]).astype(BF16)
            return f

        def u_merge(j):
            def f():
                cols = slice(j * COLB, (j + 1) * COLB)
                ga = jax.nn.sigmoid(proj(main, 3 * D + j * COLB))
                gb = jax.nn.sigmoid(proj(main, 4 * D + j * COLB))
                ta = _dot(a_scr[rows, :], wmat(wa_ref, cols=cols))
                tb = _dot(p_scr[rows, :], wmat(wb_ref, cols=cols))
                m_scr[rows, cols] = (ga * ta + gb * tb).astype(BF16)
            return f

        def u_out(j):
            def f():
                cols = slice(j * COLB, (j + 1) * COLB)
                t = _dot(m_scr[rows, :], wmat(wo_ref, cols=cols))
                y = alpha * x_rows(r0, r0 + hs, cols) + gate1[:, cols] * t
                v_scr[rows, cols] = y
                st["ysum"] = (st["ysum"] + jnp.sum(y, axis=-1, keepdims=True)) if j else jnp.sum(
                    y, axis=-1, keepdims=True)
            return f

        def row_rstd(mu):
            ss = jnp.zeros((hs, 1), F32)
            for j in range(ncb):
                cols = slice(j * COLB, (j + 1) * COLB)
                xc = v_scr[rows, cols] - mu
                ss = ss + jnp.sum(xc * xc, axis=-1, keepdims=True)
            return lax.rsqrt(ss * (1.0 / D) + LN_EPS)

        def u_ln1():
            ymu = st["ysum"] * (1.0 / D)
            yrs = row_rstd(ymu)
            xsum = jnp.zeros((hs, 1), F32)
            for j in range(ncb):
                cols = slice(j * COLB, (j + 1) * COLB)
                x1 = (v_scr[rows, cols] - ymu) * yrs * l1g_ref[:, cols] + l1b_ref[:, cols]
                x1_ref[0, rows, cols] = x1
                v_scr[rows, cols] = x1
                xsum = xsum + jnp.sum(x1, axis=-1, keepdims=True)
            st["xmu"] = xsum * (1.0 / D)

        def u_ln2():
            xmu = st["xmu"]
            xrs = row_rstd(xmu)
            for j in range(ncb):
                cols = slice(j * COLB, (j + 1) * COLB)
                h2 = (v_scr[rows, cols] - xmu) * xrs * (1.0 + scale2[:, cols]) + shift2[:, cols]
                h2_scr[rows, cols] = h2.astype(BF16)
            lo_bits = lax.bitcast_convert_type(h2_scr[rows, 0:half].astype(F32), U32)
            hi_bits = lax.bitcast_convert_type(h2_scr[rows, half:D].astype(F32), U32)
            h2u_ref[rows, 0:half] = (lo_bits >> 16) | (hi_bits & jnp.uint32(0xFFFF0000))

        def u_route():
            logits = _dot(h2_scr[rows, :], wmat(wr_ref)) + br_ref[...]
            lane = lax.broadcasted_iota(jnp.int32, (hs, ROUTER_LANES), 1)
            lane_f = lane.astype(F32)
            neg = -jnp.inf
            is_g = lane < N_GROUPS
            lg = jnp.where(is_g, logits, neg)
            mg = jnp.max(lg, axis=-1, keepdims=True)
            g_sel = jnp.min(jnp.where(lg == mg, lane_f, float(ROUTER_LANES)), axis=-1, keepdims=True)
            p_sel = 1.0 / jnp.sum(jnp.where(is_g, jnp.exp(logits - mg), 0.0), axis=-1, keepdims=True)
            e_lo = EXPERT_LANE0 + g_sel * EPG
            in_grp = (lane_f >= e_lo) & (lane_f < e_lo + EPG)
            le = jnp.where(in_grp, logits, neg)
            m1 = jnp.max(le, axis=-1, keepdims=True)
            i1 = jnp.min(jnp.where(le == m1, lane_f, float(ROUTER_LANES)), axis=-1, keepdims=True)
            le2 = jnp.where(lane_f == i1, neg, le)
            m2 = jnp.max(le2, axis=-1, keepdims=True)
            i2 = jnp.min(jnp.where(le2 == m2, lane_f, float(ROUTER_LANES)), axis=-1, keepdims=True)
            e2x = jnp.exp(m2 - m1)
            den = 1.0 + e2x
            g1 = p_sel / den
            g2 = p_sel * e2x / den

            j1 = i1 - e_lo
            j2 = i2 - e_lo
            first_is_a = j1 < j2
            ja = jnp.minimum(j1, j2)
            jb = jnp.maximum(j1, j2)
            cls = g_sel * PAIRS + (EPG - 1) * ja - ja * (ja - 1.0) * 0.5 + (jb - ja - 1.0)
            w_a = jnp.where(first_is_a, g1, g2)
            w_b = jnp.where(first_is_a, g2, g1)
            extra = jnp.where(lane == 0, w_a, jnp.where(lane == 1, w_b, 0.0))
            h2u_ref[rows, half:half + ROUTER_LANES] = lax.bitcast_convert_type(extra, U32)

            hit = lane_f == cls
            onehot = jnp.where(hit, 1.0, 0.0)
            pre = _dot(tri_ref[...], onehot.astype(BF16)) + carry_scr[0:1, :]
            rank = jnp.sum(jnp.where(hit, pre, 0.0), axis=-1, keepdims=True)
            carry_scr[0:1, :] = carry_scr[0:1, :] + jnp.sum(onehot, axis=0, keepdims=True)

            cr = jnp.where(lane == 0, cls, jnp.where(lane == 1, rank, 0.0))
            cr_ref[:, rows] = cr.T[0:8, :]

        for j in range(ncb):
            units.append(u_uv(j))
        units += [u_vln_a, u_vln_b]
        n_chunks = hs // CHUNK
        for k in range(max(n_chunks, len(POOL_WINDOWS))):
            if k < len(POOL_WINDOWS):
                units.append(u_pool(k))
            if k < n_chunks:
                units.append(u_sgu(k))
        for j in range(ncb):
            units.append(u_merge(j))
        for j in range(ncb):
            units.append(u_out(j))
        units += [u_ln1, u_ln2, u_route]
        return units

    unit_lists = [stream(si) for si in range(n_streams)]
    n_units = len(unit_lists[0])
    for k in range(n_units + skew * (n_streams - 1)):
        for si, units in enumerate(unit_lists):
            if 0 <= k - si * skew < n_units:
                units[k - si * skew]()
    cnt_ref[...] = carry_scr[...]


def _const_spec(shape):
    nd = len(shape)
    return pl.BlockSpec(shape, lambda b, i: (0,) * nd, pipeline_mode=pl.Buffered(1))


def _mixer(x_p, x_s, mod, wts, alpha, ts):
    nbp, s_len, _ = x_p.shape
    nbs = x_s.shape[0]
    assert x_s.shape[1] == s_len and s_len % ts == 0 and ts % CHUNK == 0
    nb = nbp + nbs
    n_i = s_len // ts
    hb = ts // HALO
    n_hb = s_len // HALO

    def pick(b, on_p, val, const):
        return jnp.where(b < nbp if on_p else b >= nbp, val, const)

    def main_map(on_p):
        def f(b, i):
            bb = pick(b, on_p, b if on_p else b - nbp, nbp - 1 if on_p else 0)
            ii = pick(b, on_p, i, n_i - 1 if on_p else 0)
            return (bb, ii, 0)
        return f

    def halo_map(on_p, nxt):
        def f(b, i):
            bb = pick(b, on_p, b if on_p else b - nbp, nbp - 1 if on_p else 0)
            idx = jnp.minimum((i + 1) * hb, n_hb - 1) if nxt else jnp.maximum(i * hb - 1, 0)
            ii = pick(b, on_p, idx, n_hb - 1 if on_p else 0)
            return (bb, ii, 0)
        return f

    hs = ts // MIXER_STREAMS
    assert hs % CHUNK == 0
    tri = (lax.broadcasted_iota(jnp.int32, (hs, hs), 1)
           < lax.broadcasted_iota(jnp.int32, (hs, hs), 0)).astype(BF16)
    consts = list(wts) + [tri]
    in_specs = [
        pl.BlockSpec((1, ts, D), main_map(True)),
        pl.BlockSpec((1, ts, D), main_map(False)),
        pl.BlockSpec((1, HALO, D), halo_map(True, False)),
        pl.BlockSpec((1, HALO, D), halo_map(False, False)),
        pl.BlockSpec((1, HALO, D), halo_map(True, True)),
        pl.BlockSpec((1, HALO, D), halo_map(False, True)),
        pl.BlockSpec((1, N_MOD, D), lambda b, i: (b, 0, 0)),
        pl.BlockSpec((ts, ROUTER_LANES), lambda b, i: (i, 0)),
    ] + [_const_spec(w.shape) for w in consts]

    pos = jnp.arange(s_len, dtype=jnp.int32)[:, None]
    win = jnp.asarray(POOL_WINDOWS + (1,) * (ROUTER_LANES - len(POOL_WINDOWS)), jnp.int32)[None, :]
    inv_count = 1.0 / (jnp.minimum(pos + (win - 1 - win // 2), s_len - 1)
                       - jnp.maximum(pos - win // 2, 0) + 1).astype(F32)
    n_tok = nb * s_len
    out_shape = (
        jax.ShapeDtypeStruct((nb, s_len, D), F32),
        jax.ShapeDtypeStruct((n_tok, ROW_W), U32),
        jax.ShapeDtypeStruct((8, n_tok), F32),
        jax.ShapeDtypeStruct((8, ROUTER_LANES), F32),
    )
    out_specs = (
        pl.BlockSpec((1, ts, D), lambda b, i: (b, i, 0)),
        pl.BlockSpec((ts, ROW_W), lambda b, i: (b * n_i + i, 0)),
        pl.BlockSpec((8, ts), lambda b, i: (0, b * n_i + i)),
        pl.BlockSpec((8, ROUTER_LANES), lambda b, i: (0, 0)),
    )
    scratch = [
        pltpu.VMEM((ts + 2 * HALO * MIXER_STREAMS, D), BF16),
        pltpu.VMEM((ts, D), F32),
        pltpu.VMEM((ts, D), F32),
        pltpu.VMEM((ts, D), BF16),
        pltpu.VMEM((ts, D), BF16),
        pltpu.VMEM((ts, D), BF16),
        pltpu.VMEM((ts, D), BF16),
        pltpu.VMEM((ts, D), BF16),
        pltpu.VMEM((8, ROUTER_LANES), F32),
    ]
    return pl.pallas_call(
        functools.partial(_mixer_kernel, nbp, ts, s_len, alpha, MIXER_STREAMS, MIXER_SKEW),
        grid=(nb, n_i),
        in_specs=in_specs,
        out_specs=out_specs,
        out_shape=out_shape,
        scratch_shapes=scratch,
        compiler_params=pltpu.CompilerParams(
            dimension_semantics=("arbitrary", "arbitrary"), vmem_limit_bytes=VMEM_LIMIT),
        name="mixer",
    )(x_p, x_s, x_p, x_s, x_p, x_s, mod, inv_count, *consts)


def _dest_kernel(seg_ref, cls_ref, rank_ref, o_ref):
    cls = cls_ref[...]
    acc = rank_ref[...].astype(jnp.int32)
    for c in range(N_CLASSES):
        acc = acc + jnp.where(cls == float(c), seg_ref[c], 0)
    o_ref[...] = acc


def _dest(seg_start, cls, rank):
    shape = cls.shape
    return pl.pallas_call(
        _dest_kernel,
        grid_spec=pltpu.PrefetchScalarGridSpec(
            num_scalar_prefetch=1,
            grid=(1,),
            in_specs=[pl.BlockSpec(shape, lambda i, s: (0, 0)), pl.BlockSpec(shape, lambda i, s: (0, 0))],
            out_specs=pl.BlockSpec(shape, lambda i, s: (0, 0)),
        ),
        out_shape=jax.ShapeDtypeStruct(shape, jnp.int32),
        name="dest",
    )(seg_start, cls, rank)


def _dispatch_kernel(td, dest_ref, rows_ref, xs_init_hbm, xs_hbm, sem):
    del xs_init_hbm
    base = pl.program_id(0) * td

    def row_copy(r):
        return pltpu.make_async_copy(rows_ref.at[pl.ds(r, 1)], xs_hbm.at[pl.ds(dest_ref[base + r], 1)], sem)

    def start(r, carry):
        row_copy(r).start()
        return carry

    def wait(r, carry):
        row_copy(r).wait()
        return carry

    lax.fori_loop(0, td, start, 0, unroll=8)
    lax.fori_loop(0, td, wait, 0, unroll=8)


def _dispatch(rows, dest, n_rows, td):
    n_tok = rows.shape[0]
    assert n_tok % td == 0
    xs_init = jnp.zeros((n_rows, ROW_W), U32)
    return pl.pallas_call(
        functools.partial(_dispatch_kernel, td),
        grid_spec=pltpu.PrefetchScalarGridSpec(
            num_scalar_prefetch=1,
            grid=(n_tok // td,),
            in_specs=[pl.BlockSpec((td, ROW_W), lambda s, d: (s, 0)), pl.BlockSpec(memory_space=pl.ANY)],
            out_specs=pl.BlockSpec(memory_space=pl.ANY),
            scratch_shapes=[pltpu.SemaphoreType.DMA(())],
        ),
        out_shape=jax.ShapeDtypeStruct((n_rows, ROW_W), U32),
        input_output_aliases={2: 0},
        compiler_params=pltpu.CompilerParams(dimension_semantics=("arbitrary",)),
        name="dispatch",
    )(dest, rows, xs_init)


def _expert_kernel(bm, ea_ref, eb_ref, nused_ref, xs_ref, wga_ref, wua_ref, wda_ref, wgb_ref, wub_ref, wdb_ref,
                   ys_ref, act_scr):
    j = pl.program_id(0)
    used = j < nused_ref[0]
    half = D // 2

    @pl.when(used)
    def _():
        bits = xs_ref[:, 0:half]
        lo = lax.bitcast_convert_type(bits << 16, F32).astype(BF16)
        hi = lax.bitcast_convert_type(bits & jnp.uint32(0xFFFF0000), F32).astype(BF16)
        wts = lax.bitcast_convert_type(xs_ref[:, half:half + ROUTER_LANES], F32)
        for e, (wg_ref, wu_ref) in enumerate(((wga_ref, wua_ref), (wgb_ref, wub_ref))):
            for c in range(D_EXPERT // COLB):
                cols = slice(c * COLB, (c + 1) * COLB)
                g = _dot(lo, wg_ref[0, 0:half, cols]) + _dot(hi, wg_ref[0, half:D, cols])
                up = _dot(lo, wu_ref[0, 0:half, cols]) + _dot(hi, wu_ref[0, half:D, cols])
                act_scr[e, :, cols] = (jax.nn.silu(g) * up).astype(BF16)
        w_a = wts[:, 0:1]
        w_b = wts[:, 1:2]
        for c in range(D // COLB):
            cols = slice(c * COLB, (c + 1) * COLB)
            ys_ref[:, cols] = (w_a * _dot(act_scr[0], wda_ref[0, :, cols])
                               + w_b * _dot(act_scr[1], wdb_ref[0, :, cols]))

    @pl.when(jnp.logical_not(used))
    def _():
        ys_ref[...] = jnp.zeros_like(ys_ref)


def _experts(xs, blk_a, blk_b, n_used, w_gate, w_up, w_down, bm):
    n_rows = xs.shape[0]
    n_blocks = n_rows // bm

    def x_map(j, ea, eb, nu):
        return (jnp.minimum(j, nu[0] - 1), 0)

    def a_map(j, ea, eb, nu):
        return (ea[j], 0, 0)

    def b_map(j, ea, eb, nu):
        return (eb[j], 0, 0)

    return pl.pallas_call(
        functools.partial(_expert_kernel, bm),
        grid_spec=pltpu.PrefetchScalarGridSpec(
            num_scalar_prefetch=3,
            grid=(n_blocks,),
            in_specs=[pl.BlockSpec((bm, ROW_W), x_map),
                      pl.BlockSpec((1, D, D_EXPERT), a_map),
                      pl.BlockSpec((1, D, D_EXPERT), a_map),
                      pl.BlockSpec((1, D_EXPERT, D), a_map),
                      pl.BlockSpec((1, D, D_EXPERT), b_map),
                      pl.BlockSpec((1, D, D_EXPERT), b_map),
                      pl.BlockSpec((1, D_EXPERT, D), b_map)],
            out_specs=pl.BlockSpec((bm, D), lambda j, ea, eb, nu: (j, 0)),
            scratch_shapes=[pltpu.VMEM((2, bm, D_EXPERT), BF16)],
        ),
        out_shape=jax.ShapeDtypeStruct((n_rows, D), F32),
        compiler_params=pltpu.CompilerParams(
            dimension_semantics=("arbitrary",), vmem_limit_bytes=VMEM_LIMIT),
        name="experts",
    )(blk_a, blk_b, n_used, xs, w_gate, w_up, w_down, w_gate, w_up, w_down)


def _final_kernel(tk, n_i, alpha, dest_ref, x1_ref, mod_ref, l2g_ref, l2b_ref, ys_hbm,
                  out_ref, ybuf, sem):
    step = pl.program_id(0) * n_i + pl.program_id(1)
    n_steps = pl.num_programs(0) * n_i

    def for_tile(st, slot, fn):
        def body(r, carry):
            src = dest_ref[st * tk + r]
            fn(pltpu.make_async_copy(ys_hbm.at[pl.ds(src, 1)], ybuf.at[slot, pl.ds(r, 1)], sem.at[slot]))
            return carry
        lax.fori_loop(0, tk, body, 0, unroll=8)

    @pl.when(step == 0)
    def _():
        for_tile(0, 0, lambda cp: cp.start())

    @pl.when(step + 1 < n_steps)
    def _():
        for_tile(step + 1, (step + 1) % 2, lambda cp: cp.start())

    slot = step % 2
    for_tile(step, slot, lambda cp: cp.wait())

    gate2 = mod_ref[0][5:6]
    y = alpha * x1_ref[0] + gate2 * ybuf[slot]
    out_ref[0] = _ln(y) * l2g_ref[...] + l2b_ref[...]


def _final(x1, mod, ln2_g, ln2_b, ys, dest, b_off, nbg, alpha, tk):
    s_len = x1.shape[1]
    n_i = s_len // tk
    return pl.pallas_call(
        functools.partial(_final_kernel, tk, n_i, alpha),
        grid_spec=pltpu.PrefetchScalarGridSpec(
            num_scalar_prefetch=1,
            grid=(nbg, n_i),
            in_specs=[pl.BlockSpec((1, tk, D), lambda b, i, d: (b + b_off, i, 0)),
                      pl.BlockSpec((1, N_MOD, D), lambda b, i, d: (b + b_off, 0, 0)),
                      pl.BlockSpec((1, D), lambda b, i, d: (0, 0)),
                      pl.BlockSpec((1, D), lambda b, i, d: (0, 0)),
                      pl.BlockSpec(memory_space=pl.ANY)],
            out_specs=pl.BlockSpec((1, tk, D), lambda b, i, d: (b, i, 0)),
            scratch_shapes=[pltpu.VMEM((2, tk, D), F32),
                            pltpu.SemaphoreType.DMA((2,))],
        ),
        out_shape=jax.ShapeDtypeStruct((nbg, s_len, D), F32),
        compiler_params=pltpu.CompilerParams(
            dimension_semantics=("arbitrary", "arbitrary"), vmem_limit_bytes=VMEM_LIMIT),
        name="final",
    )(dest, x1, mod, ln2_g, ln2_b, ys)


def _layer(x_p, x_s, c_all, p, alpha):
    nbp, s_len, _ = x_p.shape
    nbs = x_s.shape[0]
    nb = nbp + nbs
    n_tok = nb * s_len

    mod = _modulation(c_all, p["w_mod"], p["b_mod"]).reshape(nb, N_MOD, D)

    bsf = jnp.repeat(p["b_spatial"].T, HEAD_DIM, axis=1)
    n_in = p["w_in"].shape[1]
    w_router = jnp.concatenate(
        [p["w_router_group"], p["w_router_expert"],
         jnp.zeros((D, ROUTER_LANES - N_GROUPS - N_EXPERTS), F32)], axis=1)
    b_router = jnp.concatenate(
        [p["b_router_group"], p["b_router_expert"],
         jnp.zeros((ROUTER_LANES - N_GROUPS - N_EXPERTS,), F32)]).reshape(1, ROUTER_LANES)
    wts = [
        _pack_rows(p["w_in"]), p["b_in"].reshape(1, n_in),
        p["w_spatial"].astype(BF16), bsf,
        p["sgu_g"].reshape(1, D), p["sgu_b"].reshape(1, D),
        _pack_rows(p["w_pool"]), p["pool_scale"].reshape(1, D),
        _pack_rows(p["w_branch_a"]), _pack_rows(p["w_branch_b"]), _pack_rows(p["w_out"]),
        p["ln1_g"].reshape(1, D), p["ln1_b"].reshape(1, D),
        _pack_rows(w_router), b_router,
    ]
    x1, rows, cr, cnt = _mixer(x_p, x_s, mod, wts, alpha, MIXER_TS)

    bm = EXPERT_BM
    assert n_tok % bm == 0 and n_tok % DEST_LANES == 0
    counts = cnt[0, :N_CLASSES].astype(jnp.int32)
    padded = (counts + bm - 1) // bm * bm
    seg_end = jnp.cumsum(padded).astype(jnp.int32)
    seg_start = seg_end - padded
    n_blocks = n_tok // bm + N_CLASSES
    block_start = jnp.arange(n_blocks, dtype=jnp.int32) * bm
    blk_cls = jnp.minimum(
        jnp.sum((seg_end[None, :] <= block_start[:, None]).astype(jnp.int32), axis=1), N_CLASSES - 1)
    cls_a, cls_b = _class_experts()
    blk_a = jnp.asarray(cls_a)[blk_cls]
    blk_b = jnp.asarray(cls_b)[blk_cls]
    n_used = (seg_end[-1:] // bm).astype(jnp.int32)

    lane_shape = (n_tok // DEST_LANES, DEST_LANES)
    dest = _dest(seg_start, cr[0].reshape(lane_shape), cr[1].reshape(lane_shape)).reshape(n_tok)

    xs = _dispatch(rows, dest, n_blocks * bm, DISPATCH_TD)
    ys = _experts(xs, blk_a, blk_b, n_used,
                  p["w_exp_gate"].astype(BF16), p["w_exp_up"].astype(BF16), p["w_exp_down"].astype(BF16), bm)

    l2g = p["ln2_g"].reshape(1, D)
    l2b = p["ln2_b"].reshape(1, D)
    tp = nbp * s_len
    y_p = _final(x1, mod, l2g, l2b, ys, dest[:tp], 0, nbp, alpha, FINAL_TK)
    y_s = _final(x1, mod, l2g, l2b, ys, dest[tp:], nbp, nbs, alpha, FINAL_TK)
    return y_p, y_s


def _pack_rows(w):
    bits = lax.bitcast_convert_type(w.astype(BF16), jnp.uint16).astype(U32)
    return bits[..., 0::2, :] | (bits[..., 1::2, :] << 16)


def _class_experts():
    ea, eb = [], []
    for g in range(N_GROUPS):
        for a in range(EPG):
            for b in range(a + 1, EPG):
                ea.append(g * EPG + a)
                eb.append(g * EPG + b)
    return np.asarray(ea, np.int32), np.asarray(eb, np.int32)


_PARAM_NAMES = ("w_mod", "b_mod", "w_in", "b_in", "w_spatial", "b_spatial", "sgu_g", "sgu_b", "w_pool",
                "pool_scale", "w_branch_a", "w_branch_b", "w_out", "ln1_g", "ln1_b", "w_router_group",
                "b_router_group", "w_router_expert", "b_router_expert", "w_exp_gate", "w_exp_up",
                "w_exp_down", "ln2_g", "ln2_b")


def kernel(x_prompt, x_sample, c_prompt, c_sample, w_mod, b_mod, w_in, b_in, w_spatial, b_spatial, sgu_g, sgu_b, w_pool, pool_scale, w_branch_a, w_branch_b, w_out, ln1_g, ln1_b, w_router_group, b_router_group, w_router_expert, b_router_expert, w_exp_gate, w_exp_up, w_exp_down, ln2_g, ln2_b):
    params = (w_mod, b_mod, w_in, b_in, w_spatial, b_spatial, sgu_g, sgu_b, w_pool, pool_scale,
              w_branch_a, w_branch_b, w_out, ln1_g, ln1_b, w_router_group, b_router_group,
              w_router_expert, b_router_expert, w_exp_gate, w_exp_up, w_exp_down, ln2_g, ln2_b)
    depth = w_mod.shape[0]
    alpha = (2.0 * depth) ** 0.25
    c_all = jnp.concatenate([c_prompt, c_sample], axis=0)
    y_p, y_s = x_prompt, x_sample
    for l in range(depth):
        p = {name: w[l] for name, w in zip(_PARAM_NAMES, params)}
        y_p, y_s = _layer(y_p, y_s, c_all, p, alpha)
    return (y_p, y_s)
```

```python
import functools
import math

import jax
import jax.numpy as jnp
import numpy as np
from jax import lax
from jax.experimental import pallas as pl
from jax.experimental.pallas import tpu as pltpu
from jax.experimental.pallas import tpu_sc as plsc

F32 = jnp.float32
BF16 = jnp.bfloat16
U32 = jnp.uint32

D = 1024
CHUNK = 128
SGU_HEADS = 8
HEAD_DIM = D // SGU_HEADS
POOL_WINDOWS = (2, 4, 8, 16)
POOL_GROUP_DIM = D // len(POOL_WINDOWS)
N_MOD = 6
N_GROUPS = 4
EPG = 8
N_EXPERTS = N_GROUPS * EPG
D_EXPERT = D // 2
LN_EPS = 1e-5

HALO = 16
COLB = 256
ROUTER_LANES = 128
EXPERT_LANE0 = N_GROUPS
PAIRS = EPG * (EPG - 1) // 2
N_CLASSES = N_GROUPS * PAIRS
ROW_W = D // 2 + ROUTER_LANES
MIXER_TS = 512
MIXER_STREAMS = 1
MIXER_SKEW = 0
EXPERT_BM = 256
DISPATCH_TD = 1024
FINAL_TK = 256
DEST_LANES = 512
SC_SCATTER_WIN = 128
SC_GATHER_WIN = 128
VMEM_LIMIT = 56 * 1024 * 1024


def _ln(x):
    mu = jnp.mean(x, axis=-1, keepdims=True)
    xc = x - mu
    var = jnp.mean(xc * xc, axis=-1, keepdims=True)
    return xc * lax.rsqrt(var + LN_EPS)


_GELU_A = -2.0 * math.sqrt(2.0 / math.pi) * math.log2(math.e)
_GELU_B = _GELU_A * 0.044715


def _gelu_tanh(x):
    return x / (1.0 + jnp.exp2(x * (_GELU_A + _GELU_B * (x * x))))


def _dot(a, b):
    return jnp.dot(a, b, preferred_element_type=F32)


def _mod_kernel(c_ref, w_ref, b_ref, o_ref):
    a = jax.nn.silu(c_ref[...]).astype(BF16)
    o_ref[...] = _dot(a, w_ref[...].astype(BF16)) + b_ref[...]


def _modulation(c_all, w_mod, b_mod):
    nb = c_all.shape[0]
    n_out = w_mod.shape[1]
    cb = 512
    return pl.pallas_call(
        _mod_kernel,
        grid=(n_out // cb,),
        in_specs=[pl.BlockSpec((nb, D), lambda j: (0, 0)),
                  pl.BlockSpec((D, cb), lambda j: (0, j)),
                  pl.BlockSpec((1, cb), lambda j: (0, j))],
        out_specs=pl.BlockSpec((nb, cb), lambda j: (0, j)),
        out_shape=jax.ShapeDtypeStruct((nb, n_out), F32),
        name="mod",
    )(c_all, w_mod, b_mod.reshape(1, n_out))


def _mixer_kernel(nbp, ts, s_len, alpha, n_streams, skew,
                   xp_ref, xs_ref, xpp_ref, xsp_ref, xpn_ref, xsn_ref, mod_ref, icnt_ref,
                   win_ref, bin_ref, ws_ref, bsf_ref, sg_ref, sb_ref, wpool_ref, psc_ref,
                   wa_ref, wb_ref, wo_ref, l1g_ref, l1b_ref, wr_ref, br_ref, tri_ref,
                   x1_ref, h2u_ref, cr_ref, cnt_ref,
                   h_scr, u_scr, v_scr, vb_scr, a_scr, p_scr, m_scr, h2_scr, carry_scr):
    b = pl.program_id(0)
    i = pl.program_id(1)
    is_p = b < nbp
    ncb = D // COLB
    hs = ts // n_streams
    n_ext = hs + 2 * HALO
    half = D // 2

    def wmat(ref, rows=slice(None), cols=slice(None)):
        return pltpu.bitcast(ref[rows, cols], BF16)

    @pl.when((b == 0) & (i == 0))
    def _():
        carry_scr[...] = jnp.zeros_like(carry_scr)

    mod = mod_ref[0]
    shift1, scale1, gate1 = mod[0:1], mod[1:2], mod[2:3]
    shift2, scale2 = mod[3:4], mod[4:5]

    def adaln1(xv):
        return (_ln(xv) * (1.0 + scale1) + shift1).astype(BF16)

    def x_rows(lo, hi, cols=slice(None)):
        return jnp.where(is_p, xp_ref[0, lo:hi, cols], xs_ref[0, lo:hi, cols])

    def stream(si):
        r0 = si * hs
        rows = slice(r0, r0 + hs)
        e0 = si * n_ext
        ext = slice(e0, e0 + n_ext)
        main = slice(e0 + HALO, e0 + HALO + hs)
        st = {}
        units = []

        def proj(hrows, c0):
            return _dot(h_scr[hrows, :], wmat(win_ref, cols=slice(c0, c0 + COLB))) + bin_ref[:, c0:c0 + COLB]

        def u_ln():
            top = jnp.where(is_p, xpp_ref[0], xsp_ref[0]) if si == 0 else x_rows(r0 - HALO, r0)
            bot = (jnp.where(is_p, xpn_ref[0], xsn_ref[0]) if si == n_streams - 1
                   else x_rows(r0 + hs, r0 + hs + HALO))
            h_scr[e0:e0 + HALO, :] = adaln1(top)
            h_scr[main, :] = adaln1(x_rows(r0, r0 + hs))
            h_scr[e0 + HALO + hs:e0 + n_ext, :] = adaln1(bot)
            st["vsum"] = jnp.zeros((hs, 1), F32)
        units.append(u_ln)

        def u_uv(j):
            def f():
                cols = slice(j * COLB, (j + 1) * COLB)
                u_scr[rows, cols] = _gelu_tanh(proj(main, j * COLB))
                gv = _gelu_tanh(proj(main, D + j * COLB))
                v_scr[rows, cols] = gv
                st["vsum"] = st["vsum"] + jnp.sum(gv, axis=-1, keepdims=True)
            return f

        def u_vln_a():
            st["vmu"] = st["vsum"] * (1.0 / D)
            vss = jnp.zeros((hs, 1), F32)
            for j in range(ncb):
                cols = slice(j * COLB, (j + 1) * COLB)
                xc = v_scr[rows, cols] - st["vmu"]
                vss = vss + jnp.sum(xc * xc, axis=-1, keepdims=True)
            st["vrs"] = lax.rsqrt(vss * (1.0 / D) + LN_EPS)

        def u_vln_b():
            for j in range(ncb):
                cols = slice(j * COLB, (j + 1) * COLB)
                vb_scr[rows, cols] = ((v_scr[rows, cols] - st["vmu"]) * st["vrs"] * sg_ref[:, cols]
                                      + sb_ref[:, cols]).astype(BF16)

        def u_sgu(c):
            def f():
                crow = slice(r0 + c * CHUNK, r0 + (c + 1) * CHUNK)
                for hh in range(SGU_HEADS):
                    cols = slice(hh * HEAD_DIM, (hh + 1) * HEAD_DIM)
                    mixed = _dot(ws_ref[hh], vb_scr[crow, cols]) + bsf_ref[:, cols]
                    a_scr[crow, cols] = (u_scr[crow, cols] * mixed).astype(BF16)
            return f

        def u_pool(gi):
            w = POOL_WINDOWS[gi]

            def f():
                cols = slice(gi * POOL_GROUP_DIM, (gi + 1) * POOL_GROUP_DIM)
                ext_pos = lax.broadcasted_iota(jnp.int32, (n_ext, 1), 0) + (i * ts + r0 - HALO)
                ext_valid = (ext_pos >= 0) & (ext_pos < s_len)
                zp = jnp.where(ext_valid, proj(ext, 2 * D + gi * POOL_GROUP_DIM), 0.0)
                acc = zp + pltpu.roll(zp, 1, 0)
                if w >= 4:
                    acc = pltpu.roll(acc, 1, 0) + pltpu.roll(acc, n_ext - 1, 0)
                if w >= 8:
                    acc = pltpu.roll(acc, 2, 0) + pltpu.roll(acc, n_ext - 2, 0)
                if w >= 16:
                    acc = pltpu.roll(acc, 4, 0) + pltpu.roll(acc, n_ext - 4, 0)
                inv_count = icnt_ref[rows, gi:gi + 1]
                dd = (acc[HALO:HALO + hs] * inv_count - zp[HALO:HALO + hs]).astype(BF16)
                p_scr[rows, cols] = (_dot(dd, pltpu.bitcast(wpool_ref[gi], BF16)) * psc_ref[:, cols]).astype(BF16)
            return f

        def u_merge(j):
            def f():
                cols = slice(j * COLB, (j + 1) * COLB)
                ga = jax.nn.sigmoid(proj(main, 3 * D + j * COLB))
                gb = jax.nn.sigmoid(proj(main, 4 * D + j * COLB))
                ta = _dot(a_scr[rows, :], wmat(wa_ref, cols=cols))
                tb = _dot(p_scr[rows, :], wmat(wb_ref, cols=cols))
                m_scr[rows, cols] = (ga * ta + gb * tb).astype(BF16)
            return f

        def u_out(j):
            def f():
                cols = slice(j * COLB, (j + 1) * COLB)
                t = _dot(m_scr[rows, :], wmat(wo_ref, cols=cols))
                y = alpha * x_rows(r0, r0 + hs, cols) + gate1[:, cols] * t
                v_scr[rows, cols] = y
                st["ysum"] = (st["ysum"] + jnp.sum(y, axis=-1, keepdims=True)) if j else jnp.sum(
                    y, axis=-1, keepdims=True)
            return f

        def row_rstd(mu):
            ss = jnp.zeros((hs, 1), F32)
            for j in range(ncb):
                cols = slice(j * COLB, (j + 1) * COLB)
                xc = v_scr[rows, cols] - mu
                ss = ss + jnp.sum(xc * xc, axis=-1, keepdims=True)
            return lax.rsqrt(ss * (1.0 / D) + LN_EPS)

        def u_ln1():
            ymu = st["ysum"] * (1.0 / D)
            yrs = row_rstd(ymu)
            xsum = jnp.zeros((hs, 1), F32)
            for j in range(ncb):
                cols = slice(j * COLB, (j + 1) * COLB)
                x1 = (v_scr[rows, cols] - ymu) * yrs * l1g_ref[:, cols] + l1b_ref[:, cols]
                x1_ref[0, rows, cols] = x1
                v_scr[rows, cols] = x1
                xsum = xsum + jnp.sum(x1, axis=-1, keepdims=True)
            st["xmu"] = xsum * (1.0 / D)

        def u_ln2():
            xmu = st["xmu"]
            xrs = row_rstd(xmu)
            for j in range(ncb):
                cols = slice(j * COLB, (j + 1) * COLB)
                h2 = (v_scr[rows, cols] - xmu) * xrs * (1.0 + scale2[:, cols]) + shift2[:, cols]
                h2_scr[rows, cols] = h2.astype(BF16)
            lo_bits = lax.bitcast_convert_type(h2_scr[rows, 0:half].astype(F32), U32)
            hi_bits = lax.bitcast_convert_type(h2_scr[rows, half:D].astype(F32), U32)
            h2u_ref[rows, 0:half] = (lo_bits >> 16) | (hi_bits & jnp.uint32(0xFFFF0000))

        def u_route():
            logits = _dot(h2_scr[rows, :], wmat(wr_ref)) + br_ref[...]
            lane = lax.broadcasted_iota(jnp.int32, (hs, ROUTER_LANES), 1)
            lane_f = lane.astype(F32)
            neg = -jnp.inf
            is_g = lane < N_GROUPS
            lg = jnp.where(is_g, logits, neg)
            mg = jnp.max(lg, axis=-1, keepdims=True)
            g_sel = jnp.min(jnp.where(lg == mg, lane_f, float(ROUTER_LANES)), axis=-1, keepdims=True)
            p_sel = 1.0 / jnp.sum(jnp.where(is_g, jnp.exp(logits - mg), 0.0), axis=-1, keepdims=True)
            e_lo = EXPERT_LANE0 + g_sel * EPG
            in_grp = (lane_f >= e_lo) & (lane_f < e_lo + EPG)
            le = jnp.where(in_grp, logits, neg)
            m1 = jnp.max(le, axis=-1, keepdims=True)
            i1 = jnp.min(jnp.where(le == m1, lane_f, float(ROUTER_LANES)), axis=-1, keepdims=True)
            le2 = jnp.where(lane_f == i1, neg, le)
            m2 = jnp.max(le2, axis=-1, keepdims=True)
            i2 = jnp.min(jnp.where(le2 == m2, lane_f, float(ROUTER_LANES)), axis=-1, keepdims=True)
            e2x = jnp.exp(m2 - m1)
            den = 1.0 + e2x
            g1 = p_sel / den
            g2 = p_sel * e2x / den

            j1 = i1 - e_lo
            j2 = i2 - e_lo
            first_is_a = j1 < j2
            ja = jnp.minimum(j1, j2)
            jb = jnp.maximum(j1, j2)
            cls = g_sel * PAIRS + (EPG - 1) * ja - ja * (ja - 1.0) * 0.5 + (jb - ja - 1.0)
            w_a = jnp.where(first_is_a, g1, g2)
            w_b = jnp.where(first_is_a, g2, g1)
            extra = jnp.where(lane == 0, w_a, jnp.where(lane == 1, w_b, 0.0))
            h2u_ref[rows, half:half + ROUTER_LANES] = lax.bitcast_convert_type(extra, U32)

            hit = lane_f == cls
            onehot = jnp.where(hit, 1.0, 0.0)
            pre = _dot(tri_ref[...], onehot.astype(BF16)) + carry_scr[0:1, :]
            rank = jnp.sum(jnp.where(hit, pre, 0.0), axis=-1, keepdims=True)
            carry_scr[0:1, :] = carry_scr[0:1, :] + jnp.sum(onehot, axis=0, keepdims=True)

            cr = jnp.where(lane == 0, cls, jnp.where(lane == 1, rank, 0.0))
            cr_ref[:, rows] = cr.T[0:8, :]

        for j in range(ncb):
            units.append(u_uv(j))
        units += [u_vln_a, u_vln_b]
        n_chunks = hs // CHUNK
        for k in range(max(n_chunks, len(POOL_WINDOWS))):
            if k < len(POOL_WINDOWS):
                units.append(u_pool(k))
            if k < n_chunks:
                units.append(u_sgu(k))
        for j in range(ncb):
            units.append(u_merge(j))
        for j in range(ncb):
            units.append(u_out(j))
        units += [u_ln1, u_ln2, u_route]
        return units

    unit_lists = [stream(si) for si in range(n_streams)]
    n_units = len(unit_lists[0])
    for k in range(n_units + skew * (n_streams - 1)):
        for si, units in enumerate(unit_lists):
            if 0 <= k - si * skew < n_units:
                units[k - si * skew]()
    cnt_ref[...] = carry_scr[...]


def _const_spec(shape):
    nd = len(shape)
    return pl.BlockSpec(shape, lambda b, i: (0,) * nd, pipeline_mode=pl.Buffered(1))


def _mixer(x_p, x_s, mod, wts, alpha, ts):
    nbp, s_len, _ = x_p.shape
    nbs = x_s.shape[0]
    assert x_s.shape[1] == s_len and s_len % ts == 0 and ts % CHUNK == 0
    nb = nbp + nbs
    n_i = s_len // ts
    hb = ts // HALO
    n_hb = s_len // HALO

    def pick(b, on_p, val, const):
        return jnp.where(b < nbp if on_p else b >= nbp, val, const)

    def main_map(on_p):
        def f(b, i):
            bb = pick(b, on_p, b if on_p else b - nbp, nbp - 1 if on_p else 0)
            ii = pick(b, on_p, i, n_i - 1 if on_p else 0)
            return (bb, ii, 0)
        return f

    def halo_map(on_p, nxt):
        def f(b, i):
            bb = pick(b, on_p, b if on_p else b - nbp, nbp - 1 if on_p else 0)
            idx = jnp.minimum((i + 1) * hb, n_hb - 1) if nxt else jnp.maximum(i * hb - 1, 0)
            ii = pick(b, on_p, idx, n_hb - 1 if on_p else 0)
            return (bb, ii, 0)
        return f

    hs = ts // MIXER_STREAMS
    assert hs % CHUNK == 0
    tri = (lax.broadcasted_iota(jnp.int32, (hs, hs), 1)
           < lax.broadcasted_iota(jnp.int32, (hs, hs), 0)).astype(BF16)
    consts = list(wts) + [tri]
    in_specs = [
        pl.BlockSpec((1, ts, D), main_map(True)),
        pl.BlockSpec((1, ts, D), main_map(False)),
        pl.BlockSpec((1, HALO, D), halo_map(True, False)),
        pl.BlockSpec((1, HALO, D), halo_map(False, False)),
        pl.BlockSpec((1, HALO, D), halo_map(True, True)),
        pl.BlockSpec((1, HALO, D), halo_map(False, True)),
        pl.BlockSpec((1, N_MOD, D), lambda b, i: (b, 0, 0)),
        pl.BlockSpec((ts, ROUTER_LANES), lambda b, i: (i, 0)),
    ] + [_const_spec(w.shape) for w in consts]

    pos = jnp.arange(s_len, dtype=jnp.int32)[:, None]
    win = jnp.asarray(POOL_WINDOWS + (1,) * (ROUTER_LANES - len(POOL_WINDOWS)), jnp.int32)[None, :]
    inv_count = 1.0 / (jnp.minimum(pos + (win - 1 - win // 2), s_len - 1)
                       - jnp.maximum(pos - win // 2, 0) + 1).astype(F32)
    n_tok = nb * s_len
    out_shape = (
        jax.ShapeDtypeStruct((nb, s_len, D), F32),
        jax.ShapeDtypeStruct((n_tok, ROW_W), U32),
        jax.ShapeDtypeStruct((8, n_tok), F32),
        jax.ShapeDtypeStruct((8, ROUTER_LANES), F32),
    )
    out_specs = (
        pl.BlockSpec((1, ts, D), lambda b, i: (b, i, 0)),
        pl.BlockSpec((ts, ROW_W), lambda b, i: (b * n_i + i, 0)),
        pl.BlockSpec((8, ts), lambda b, i: (0, b * n_i + i)),
        pl.BlockSpec((8, ROUTER_LANES), lambda b, i: (0, 0)),
    )
    scratch = [
        pltpu.VMEM((ts + 2 * HALO * MIXER_STREAMS, D), BF16),
        pltpu.VMEM((ts, D), F32),
        pltpu.VMEM((ts, D), F32),
        pltpu.VMEM((ts, D), BF16),
        pltpu.VMEM((ts, D), BF16),
        pltpu.VMEM((ts, D), BF16),
        pltpu.VMEM((ts, D), BF16),
        pltpu.VMEM((ts, D), BF16),
        pltpu.VMEM((8, ROUTER_LANES), F32),
    ]
    return pl.pallas_call(
        functools.partial(_mixer_kernel, nbp, ts, s_len, alpha, MIXER_STREAMS, MIXER_SKEW),
        grid=(nb, n_i),
        in_specs=in_specs,
        out_specs=out_specs,
        out_shape=out_shape,
        scratch_shapes=scratch,
        compiler_params=pltpu.CompilerParams(
            dimension_semantics=("arbitrary", "arbitrary"), vmem_limit_bytes=VMEM_LIMIT),
        name="mixer",
    )(x_p, x_s, x_p, x_s, x_p, x_s, mod, inv_count, *consts)


def _dest_kernel(seg_ref, cls_ref, rank_ref, o_ref):
    cls = cls_ref[...]
    acc = rank_ref[...].astype(jnp.int32)
    for c in range(N_CLASSES):
        acc = acc + jnp.where(cls == float(c), seg_ref[c], 0)
    o_ref[...] = acc


def _dest(seg_start, cls, rank):
    shape = cls.shape
    return pl.pallas_call(
        _dest_kernel,
        grid_spec=pltpu.PrefetchScalarGridSpec(
            num_scalar_prefetch=1,
            grid=(1,),
            in_specs=[pl.BlockSpec(shape, lambda i, s: (0, 0)), pl.BlockSpec(shape, lambda i, s: (0, 0))],
            out_specs=pl.BlockSpec(shape, lambda i, s: (0, 0)),
        ),
        out_shape=jax.ShapeDtypeStruct(shape, jnp.int32),
        name="dest",
    )(seg_start, cls, rank)


def _dispatch_kernel(td, dest_ref, rows_ref, xs_init_hbm, xs_hbm, sem):
    del xs_init_hbm
    base = pl.program_id(0) * td

    def row_copy(r):
        return pltpu.make_async_copy(rows_ref.at[pl.ds(r, 1)], xs_hbm.at[pl.ds(dest_ref[base + r], 1)], sem)

    def start(r, carry):
        row_copy(r).start()
        return carry

    def wait(r, carry):
        row_copy(r).wait()
        return carry

    lax.fori_loop(0, td, start, 0, unroll=8)
    lax.fori_loop(0, td, wait, 0, unroll=8)


def _dispatch(rows, dest, n_rows, td):
    n_tok = rows.shape[0]
    assert n_tok % td == 0
    xs_init = jnp.zeros((n_rows, ROW_W), U32)
    return pl.pallas_call(
        functools.partial(_dispatch_kernel, td),
        grid_spec=pltpu.PrefetchScalarGridSpec(
            num_scalar_prefetch=1,
            grid=(n_tok // td,),
            in_specs=[pl.BlockSpec((td, ROW_W), lambda s, d: (s, 0)), pl.BlockSpec(memory_space=pl.ANY)],
            out_specs=pl.BlockSpec(memory_space=pl.ANY),
            scratch_shapes=[pltpu.SemaphoreType.DMA(())],
        ),
        out_shape=jax.ShapeDtypeStruct((n_rows, ROW_W), U32),
        input_output_aliases={2: 0},
        compiler_params=pltpu.CompilerParams(dimension_semantics=("arbitrary",)),
        name="dispatch",
    )(dest, rows, xs_init)


def _sc_mesh():
    return plsc.VectorSubcoreMesh(core_axis_name="core", subcore_axis_name="subcore")


def _sc_scatter_rows(rows, dest, n_rows, win):
    n_tok, row_w = rows.shape
    mesh = _sc_mesh()
    n_workers = mesh.num_cores * mesh.num_subcores
    n_steps = n_tok // win
    assert n_tok % win == 0 and n_steps % n_workers == 0
    per_worker = n_steps // n_workers

    @pl.kernel(out_type=jax.ShapeDtypeStruct((n_rows, row_w), rows.dtype), mesh=mesh,
               scratch_types=[pltpu.VMEM((win,), jnp.int32), pltpu.VMEM((win, row_w), rows.dtype)])
    def scatter(rows_hbm, idx_hbm, out_hbm, idx_vmem, buf):
        worker = lax.axis_index("core") * mesh.num_subcores + lax.axis_index("subcore")

        @pl.loop(0, per_worker)
        def _(k):
            step = worker * per_worker + k
            pltpu.sync_copy(idx_hbm.at[step], idx_vmem)
            pltpu.sync_copy(rows_hbm.at[pl.ds(step * win, win)], buf)
            pltpu.sync_copy(buf, out_hbm.at[idx_vmem])

    return scatter(rows, dest.reshape(n_steps, win))


def _sc_gather_rows(table, idx, win):
    n_tok = idx.shape[0]
    row_w = table.shape[1]
    mesh = _sc_mesh()
    n_workers = mesh.num_cores * mesh.num_subcores
    n_steps = n_tok // win
    assert n_tok % win == 0 and n_steps % n_workers == 0
    per_worker = n_steps // n_workers

    @pl.kernel(out_type=jax.ShapeDtypeStruct((n_tok, row_w), table.dtype), mesh=mesh,
               scratch_types=[pltpu.VMEM((win,), jnp.int32), pltpu.VMEM((win, row_w), table.dtype)])
    def gather(table_hbm, idx_hbm, out_hbm, idx_vmem, buf):
        worker = lax.axis_index("core") * mesh.num_subcores + lax.axis_index("subcore")

        @pl.loop(0, per_worker)
        def _(k):
            step = worker * per_worker + k
            pltpu.sync_copy(idx_hbm.at[step], idx_vmem)
            pltpu.sync_copy(table_hbm.at[idx_vmem], buf)
            pltpu.sync_copy(buf, out_hbm.at[pl.ds(step * win, win)])

    return gather(table, idx.reshape(n_steps, win))


def _expert_kernel(bm, ea_ref, eb_ref, nused_ref, xs_ref, wga_ref, wua_ref, wda_ref, wgb_ref, wub_ref, wdb_ref,
                   ys_ref, act_scr):
    j = pl.program_id(0)
    used = j < nused_ref[0]
    half = D // 2

    @pl.when(used)
    def _():
        bits = xs_ref[:, 0:half]
        lo = lax.bitcast_convert_type(bits << 16, F32).astype(BF16)
        hi = lax.bitcast_convert_type(bits & jnp.uint32(0xFFFF0000), F32).astype(BF16)
        wts = lax.bitcast_convert_type(xs_ref[:, half:half + ROUTER_LANES], F32)
        for e, (wg_ref, wu_ref) in enumerate(((wga_ref, wua_ref), (wgb_ref, wub_ref))):
            for c in range(D_EXPERT // COLB):
                cols = slice(c * COLB, (c + 1) * COLB)
                g = _dot(lo, wg_ref[0, 0:half, cols]) + _dot(hi, wg_ref[0, half:D, cols])
                up = _dot(lo, wu_ref[0, 0:half, cols]) + _dot(hi, wu_ref[0, half:D, cols])
                act_scr[e, :, cols] = (jax.nn.silu(g) * up).astype(BF16)
        w_a = wts[:, 0:1]
        w_b = wts[:, 1:2]

        def y_block(c0):
            cols = slice(c0, c0 + COLB)
            return w_a * _dot(act_scr[0], wda_ref[0, :, cols]) + w_b * _dot(act_scr[1], wdb_ref[0, :, cols])

        for c0 in range(0, half, COLB):
            lo_bits = lax.bitcast_convert_type(y_block(c0).astype(BF16).astype(F32), U32)
            hi_bits = lax.bitcast_convert_type(y_block(half + c0).astype(BF16).astype(F32), U32)
            ys_ref[:, c0:c0 + COLB] = (lo_bits >> 16) | (hi_bits & jnp.uint32(0xFFFF0000))

    @pl.when(jnp.logical_not(used))
    def _():
        ys_ref[...] = jnp.zeros_like(ys_ref)


def _experts(xs, blk_a, blk_b, n_used, w_gate, w_up, w_down, bm):
    n_rows = xs.shape[0]
    n_blocks = n_rows // bm

    def x_map(j, ea, eb, nu):
        return (jnp.minimum(j, nu[0] - 1), 0)

    def a_map(j, ea, eb, nu):
        return (ea[j], 0, 0)

    def b_map(j, ea, eb, nu):
        return (eb[j], 0, 0)

    return pl.pallas_call(
        functools.partial(_expert_kernel, bm),
        grid_spec=pltpu.PrefetchScalarGridSpec(
            num_scalar_prefetch=3,
            grid=(n_blocks,),
            in_specs=[pl.BlockSpec((bm, ROW_W), x_map),
                      pl.BlockSpec((1, D, D_EXPERT), a_map),
                      pl.BlockSpec((1, D, D_EXPERT), a_map),
                      pl.BlockSpec((1, D_EXPERT, D), a_map),
                      pl.BlockSpec((1, D, D_EXPERT), b_map),
                      pl.BlockSpec((1, D, D_EXPERT), b_map),
                      pl.BlockSpec((1, D_EXPERT, D), b_map)],
            out_specs=pl.BlockSpec((bm, D // 2), lambda j, ea, eb, nu: (j, 0)),
            scratch_shapes=[pltpu.VMEM((2, bm, D_EXPERT), BF16)],
        ),
        out_shape=jax.ShapeDtypeStruct((n_rows, D // 2), U32),
        compiler_params=pltpu.CompilerParams(
            dimension_semantics=("arbitrary",), vmem_limit_bytes=VMEM_LIMIT),
        name="experts",
    )(blk_a, blk_b, n_used, xs, w_gate, w_up, w_down, w_gate, w_up, w_down)


def _final_kernel(tk, n_i, alpha, dest_ref, x1_ref, mod_ref, l2g_ref, l2b_ref, ys_hbm,
                  out_ref, ybuf, sem):
    step = pl.program_id(0) * n_i + pl.program_id(1)
    n_steps = pl.num_programs(0) * n_i

    def for_tile(st, slot, fn):
        def body(r, carry):
            src = dest_ref[st * tk + r]
            fn(pltpu.make_async_copy(ys_hbm.at[pl.ds(src, 1)], ybuf.at[slot, pl.ds(r, 1)], sem.at[slot]))
            return carry
        lax.fori_loop(0, tk, body, 0, unroll=8)

    @pl.when(step == 0)
    def _():
        for_tile(0, 0, lambda cp: cp.start())

    @pl.when(step + 1 < n_steps)
    def _():
        for_tile(step + 1, (step + 1) % 2, lambda cp: cp.start())

    slot = step % 2
    for_tile(step, slot, lambda cp: cp.wait())

    gate2 = mod_ref[0][5:6]
    y = alpha * x1_ref[0] + gate2 * ybuf[slot]
    out_ref[0] = _ln(y) * l2g_ref[...] + l2b_ref[...]


def _final(x1, mod, ln2_g, ln2_b, ys, dest, b_off, nbg, alpha, tk):
    s_len = x1.shape[1]
    n_i = s_len // tk
    return pl.pallas_call(
        functools.partial(_final_kernel, tk, n_i, alpha),
        grid_spec=pltpu.PrefetchScalarGridSpec(
            num_scalar_prefetch=1,
            grid=(nbg, n_i),
            in_specs=[pl.BlockSpec((1, tk, D), lambda b, i, d: (b + b_off, i, 0)),
                      pl.BlockSpec((1, N_MOD, D), lambda b, i, d: (b + b_off, 0, 0)),
                      pl.BlockSpec((1, D), lambda b, i, d: (0, 0)),
                      pl.BlockSpec((1, D), lambda b, i, d: (0, 0)),
                      pl.BlockSpec(memory_space=pl.ANY)],
            out_specs=pl.BlockSpec((1, tk, D), lambda b, i, d: (b, i, 0)),
            scratch_shapes=[pltpu.VMEM((2, tk, D), F32),
                            pltpu.SemaphoreType.DMA((2,))],
        ),
        out_shape=jax.ShapeDtypeStruct((nbg, s_len, D), F32),
        compiler_params=pltpu.CompilerParams(
            dimension_semantics=("arbitrary", "arbitrary"), vmem_limit_bytes=VMEM_LIMIT),
        name="final",
    )(dest, x1, mod, ln2_g, ln2_b, ys)


def _final_dense_kernel(alpha, x1_ref, f_ref, mod_ref, l2g_ref, l2b_ref, out_ref):
    half = D // 2
    gate2 = mod_ref[0][5:6]
    bits = f_ref[...]
    f_lo = lax.bitcast_convert_type(bits << 16, F32)
    f_hi = lax.bitcast_convert_type(bits & jnp.uint32(0xFFFF0000), F32)
    y_lo = alpha * x1_ref[0, :, 0:half] + gate2[:, 0:half] * f_lo
    y_hi = alpha * x1_ref[0, :, half:D] + gate2[:, half:D] * f_hi
    mu = (jnp.sum(y_lo, axis=-1, keepdims=True) + jnp.sum(y_hi, axis=-1, keepdims=True)) * (1.0 / D)
    c_lo = y_lo - mu
    c_hi = y_hi - mu
    var = (jnp.sum(c_lo * c_lo, axis=-1, keepdims=True) + jnp.sum(c_hi * c_hi, axis=-1, keepdims=True)) * (1.0 / D)
    rs = lax.rsqrt(var + LN_EPS)
    out_ref[0, :, 0:half] = c_lo * rs * l2g_ref[:, 0:half] + l2b_ref[:, 0:half]
    out_ref[0, :, half:D] = c_hi * rs * l2g_ref[:, half:D] + l2b_ref[:, half:D]


def _final_dense(x1, f, mod, ln2_g, ln2_b, b_off, nbg, alpha, tk):
    s_len = x1.shape[1]
    n_i = s_len // tk
    return pl.pallas_call(
        functools.partial(_final_dense_kernel, alpha),
        grid=(nbg, n_i),
        in_specs=[pl.BlockSpec((1, tk, D), lambda b, i: (b + b_off, i, 0)),
                  pl.BlockSpec((tk, D // 2), lambda b, i: ((b + b_off) * n_i + i, 0)),
                  pl.BlockSpec((1, N_MOD, D), lambda b, i: (b + b_off, 0, 0)),
                  pl.BlockSpec((1, D), lambda b, i: (0, 0)),
                  pl.BlockSpec((1, D), lambda b, i: (0, 0))],
        out_specs=pl.BlockSpec((1, tk, D), lambda b, i: (b, i, 0)),
        out_shape=jax.ShapeDtypeStruct((nbg, s_len, D), F32),
        compiler_params=pltpu.CompilerParams(
            dimension_semantics=("arbitrary", "arbitrary"), vmem_limit_bytes=VMEM_LIMIT),
        name="final",
    )(x1, f, mod, ln2_g, ln2_b)


def _layer(x_p, x_s, c_all, p, alpha):
    nbp, s_len, _ = x_p.shape
    nbs = x_s.shape[0]
    nb = nbp + nbs
    n_tok = nb * s_len

    mod = _modulation(c_all, p["w_mod"], p["b_mod"]).reshape(nb, N_MOD, D)

    bsf = jnp.repeat(p["b_spatial"].T, HEAD_DIM, axis=1)
    n_in = p["w_in"].shape[1]
    w_router = jnp.concatenate(
        [p["w_router_group"], p["w_router_expert"],
         jnp.zeros((D, ROUTER_LANES - N_GROUPS - N_EXPERTS), F32)], axis=1)
    b_router = jnp.concatenate(
        [p["b_router_group"], p["b_router_expert"],
         jnp.zeros((ROUTER_LANES - N_GROUPS - N_EXPERTS,), F32)]).reshape(1, ROUTER_LANES)
    wts = [
        _pack_rows(p["w_in"]), p["b_in"].reshape(1, n_in),
        p["w_spatial"].astype(BF16), bsf,
        p["sgu_g"].reshape(1, D), p["sgu_b"].reshape(1, D),
        _pack_rows(p["w_pool"]), p["pool_scale"].reshape(1, D),
        _pack_rows(p["w_branch_a"]), _pack_rows(p["w_branch_b"]), _pack_rows(p["w_out"]),
        p["ln1_g"].reshape(1, D), p["ln1_b"].reshape(1, D),
        _pack_rows(w_router), b_router,
    ]
    x1, rows, cr, cnt = _mixer(x_p, x_s, mod, wts, alpha, MIXER_TS)

    bm = EXPERT_BM
    assert n_tok % bm == 0 and n_tok % DEST_LANES == 0
    counts = cnt[0, :N_CLASSES].astype(jnp.int32)
    padded = (counts + bm - 1) // bm * bm
    seg_end = jnp.cumsum(padded).astype(jnp.int32)
    seg_start = seg_end - padded
    n_blocks = n_tok // bm + N_CLASSES
    block_start = jnp.arange(n_blocks, dtype=jnp.int32) * bm
    blk_cls = jnp.minimum(
        jnp.sum((seg_end[None, :] <= block_start[:, None]).astype(jnp.int32), axis=1), N_CLASSES - 1)
    cls_a, cls_b = _class_experts()
    blk_a = jnp.asarray(cls_a)[blk_cls]
    blk_b = jnp.asarray(cls_b)[blk_cls]
    n_used = (seg_end[-1:] // bm).astype(jnp.int32)

    lane_shape = (n_tok // DEST_LANES, DEST_LANES)
    dest = _dest(seg_start, cr[0].reshape(lane_shape), cr[1].reshape(lane_shape)).reshape(n_tok)

    xs = _sc_scatter_rows(rows, dest, n_blocks * bm, SC_SCATTER_WIN)
    ys = _experts(xs, blk_a, blk_b, n_used,
                  p["w_exp_gate"].astype(BF16), p["w_exp_up"].astype(BF16), p["w_exp_down"].astype(BF16), bm)
    f = _sc_gather_rows(ys, dest, SC_GATHER_WIN)

    l2g = p["ln2_g"].reshape(1, D)
    l2b = p["ln2_b"].reshape(1, D)
    y_p = _final_dense(x1, f, mod, l2g, l2b, 0, nbp, alpha, FINAL_TK)
    y_s = _final_dense(x1, f, mod, l2g, l2b, nbp, nbs, alpha, FINAL_TK)
    return y_p, y_s


def _pack_rows(w):
    bits = lax.bitcast_convert_type(w.astype(BF16), jnp.uint16).astype(U32)
    pairs = bits.reshape(*bits.shape[:-2], bits.shape[-2] // 2, 2, bits.shape[-1])
    return pairs[..., 0, :] | (pairs[..., 1, :] << 16)


def _class_experts():
    ea, eb = [], []
    for g in range(N_GROUPS):
        for a in range(EPG):
            for b in range(a + 1, EPG):
                ea.append(g * EPG + a)
                eb.append(g * EPG + b)
    return np.asarray(ea, np.int32), np.asarray(eb, np.int32)


_PARAM_NAMES = ("w_mod", "b_mod", "w_in", "b_in", "w_spatial", "b_spatial", "sgu_g", "sgu_b", "w_pool",
                "pool_scale", "w_branch_a", "w_branch_b", "w_out", "ln1_g", "ln1_b", "w_router_group",
                "b_router_group", "w_router_expert", "b_router_expert", "w_exp_gate", "w_exp_up",
                "w_exp_down", "ln2_g", "ln2_b")


def kernel(x_prompt, x_sample, c_prompt, c_sample, w_mod, b_mod, w_in, b_in, w_spatial, b_spatial, sgu_g, sgu_b, w_pool, pool_scale, w_branch_a, w_branch_b, w_out, ln1_g, ln1_b, w_router_group, b_router_group, w_router_expert, b_router_expert, w_exp_gate, w_exp_up, w_exp_down, ln2_g, ln2_b):
    params = (w_mod, b_mod, w_in, b_in, w_spatial, b_spatial, sgu_g, sgu_b, w_pool, pool_scale,
              w_branch_a, w_branch_b, w_out, ln1_g, ln1_b, w_router_group, b_router_group,
              w_router_expert, b_router_expert, w_exp_gate, w_exp_up, w_exp_down, ln2_g, ln2_b)
    depth = w_mod.shape[0]
    alpha = (2.0 * depth) ** 0.25
    c_all = jnp.concatenate([c_prompt, c_sample], axis=0)
    y_p, y_s = x_prompt, x_sample
    for l in range(depth):
        p = {name: w[l] for name, w in zip(_PARAM_NAMES, params)}
        y_p, y_s = _layer(y_p, y_s, c_all, p, alpha)
    return (y_p, y_s)
```

```python
import functools
import math

import jax
import jax.numpy as jnp
import numpy as np
from jax import lax
from jax.experimental import pallas as pl
from jax.experimental.pallas import tpu as pltpu
from jax.experimental.pallas import tpu_sc as plsc

F32 = jnp.float32
BF16 = jnp.bfloat16
U32 = jnp.uint32

D = 1024
CHUNK = 128
SGU_HEADS = 8
HEAD_DIM = D // SGU_HEADS
POOL_WINDOWS = (2, 4, 8, 16)
POOL_GROUP_DIM = D // len(POOL_WINDOWS)
N_MOD = 6
N_GROUPS = 4
EPG = 8
N_EXPERTS = N_GROUPS * EPG
D_EXPERT = D // 2
LN_EPS = 1e-5

HALO = 16
COLB = 256
ROUTER_LANES = 128
EXPERT_LANE0 = N_GROUPS
PAIRS = EPG * (EPG - 1) // 2
N_CLASSES = N_GROUPS * PAIRS
ROW_W = D // 2 + ROUTER_LANES
MIXER_TS = 512
MIXER_STREAMS = 1
MIXER_SKEW = 0
EXPERT_BM = 256
FINAL_TK = 1024
DEST_LANES = 512
SC_SCATTER_WIN = 128
SC_GATHER_WIN = 128
VMEM_LIMIT = 56 * 1024 * 1024


def _ln(x):
    mu = jnp.mean(x, axis=-1, keepdims=True)
    xc = x - mu
    var = jnp.mean(xc * xc, axis=-1, keepdims=True)
    return xc * lax.rsqrt(var + LN_EPS)


_GELU_A = -2.0 * math.sqrt(2.0 / math.pi) * math.log2(math.e)
_GELU_B = _GELU_A * 0.044715


def _gelu_tanh(x):
    return x / (1.0 + jnp.exp2(x * (_GELU_A + _GELU_B * (x * x))))


def _dot(a, b):
    return jnp.dot(a, b, preferred_element_type=F32)


def _mod_kernel(c_ref, w_ref, b_ref, o_ref):
    a = jax.nn.silu(c_ref[...]).astype(BF16)
    o_ref[...] = _dot(a, w_ref[...].astype(BF16)) + b_ref[...]


def _modulation(c_all, w_mod, b_mod):
    nb = c_all.shape[0]
    n_out = w_mod.shape[1]
    cb = 512
    return pl.pallas_call(
        _mod_kernel,
        grid=(n_out // cb,),
        in_specs=[pl.BlockSpec((nb, D), lambda j: (0, 0)),
                  pl.BlockSpec((D, cb), lambda j: (0, j)),
                  pl.BlockSpec((1, cb), lambda j: (0, j))],
        out_specs=pl.BlockSpec((nb, cb), lambda j: (0, j)),
        out_shape=jax.ShapeDtypeStruct((nb, n_out), F32),
        name="mod",
    )(c_all, w_mod, b_mod.reshape(1, n_out))


def _mixer_kernel(nbp, ts, s_len, alpha, n_streams, skew,
                   xp_ref, xs_ref, xpp_ref, xsp_ref, xpn_ref, xsn_ref, mod_ref, icnt_ref,
                   win_ref, bin_ref, ws_ref, bsf_ref, sg_ref, sb_ref, wpool_ref, psc_ref,
                   wa_ref, wb_ref, wo_ref, l1g_ref, l1b_ref, wr_ref, br_ref, tri_ref,
                   x1_ref, h2u_ref, cr_ref, cnt_ref,
                   h_scr, u_scr, v_scr, vb_scr, a_scr, p_scr, m_scr, h2_scr, carry_scr):
    b = pl.program_id(0)
    i = pl.program_id(1)
    is_p = b < nbp
    ncb = D // COLB
    hs = ts // n_streams
    n_ext = hs + 2 * HALO
    half = D // 2

    def wmat(ref, rows=slice(None), cols=slice(None)):
        return pltpu.bitcast(ref[rows, cols], BF16)

    @pl.when((b == 0) & (i == 0))
    def _():
        carry_scr[...] = jnp.zeros_like(carry_scr)

    mod = mod_ref[0]
    shift1, scale1, gate1 = mod[0:1], mod[1:2], mod[2:3]
    shift2, scale2 = mod[3:4], mod[4:5]

    def adaln1(xv):
        return (_ln(xv) * (1.0 + scale1) + shift1).astype(BF16)

    def x_rows(lo, hi, cols=slice(None)):
        return jnp.where(is_p, xp_ref[0, lo:hi, cols], xs_ref[0, lo:hi, cols])

    def stream(si):
        r0 = si * hs
        rows = slice(r0, r0 + hs)
        e0 = si * n_ext
        ext = slice(e0, e0 + n_ext)
        main = slice(e0 + HALO, e0 + HALO + hs)
        st = {}
        units = []

        def proj(hrows, c0):
            return _dot(h_scr[hrows, :], wmat(win_ref, cols=slice(c0, c0 + COLB))) + bin_ref[:, c0:c0 + COLB]

        n_ln = 1
        qs = hs // n_ln

        def u_ln(q):
            def f():
                if q == 0:
                    top = jnp.where(is_p, xpp_ref[0], xsp_ref[0]) if si == 0 else x_rows(r0 - HALO, r0)
                    h_scr[e0:e0 + HALO, :] = adaln1(top)
                    st["vsum"] = jnp.zeros((hs, 1), F32)
                if q == n_ln - 1:
                    bot = (jnp.where(is_p, xpn_ref[0], xsn_ref[0]) if si == n_streams - 1
                           else x_rows(r0 + hs, r0 + hs + HALO))
                    h_scr[e0 + HALO + hs:e0 + n_ext, :] = adaln1(bot)
                h_scr[e0 + HALO + q * qs:e0 + HALO + (q + 1) * qs, :] = adaln1(
                    x_rows(r0 + q * qs, r0 + (q + 1) * qs))
            return f
        for q in range(n_ln):
            units.append(u_ln(q))

        def u_uv(j):
            def f():
                cols = slice(j * COLB, (j + 1) * COLB)
                u_scr[rows, cols] = _gelu_tanh(proj(main, j * COLB))
                gv = _gelu_tanh(proj(main, D + j * COLB))
                v_scr[rows, cols] = gv
                st["vsum"] = st["vsum"] + jnp.sum(gv, axis=-1, keepdims=True)
            return f

        def u_vln_a():
            st["vmu"] = st["vsum"] * (1.0 / D)
            vss = jnp.zeros((hs, 1), F32)
            for j in range(ncb):
                cols = slice(j * COLB, (j + 1) * COLB)
                xc = v_scr[rows, cols] - st["vmu"]
                vss = vss + jnp.sum(xc * xc, axis=-1, keepdims=True)
            st["vrs"] = lax.rsqrt(vss * (1.0 / D) + LN_EPS)

        def u_vln_b():
            for j in range(ncb):
                cols = slice(j * COLB, (j + 1) * COLB)
                vb_scr[rows, cols] = ((v_scr[rows, cols] - st["vmu"]) * st["vrs"] * sg_ref[:, cols]
                                      + sb_ref[:, cols]).astype(BF16)

        def u_sgu(c):
            def f():
                crow = slice(r0 + c * CHUNK, r0 + (c + 1) * CHUNK)
                for hh in range(SGU_HEADS):
                    cols = slice(hh * HEAD_DIM, (hh + 1) * HEAD_DIM)
                    mixed = _dot(ws_ref[hh], vb_scr[crow, cols]) + bsf_ref[:, cols]
                    a_scr[crow, cols] = (u_scr[crow, cols] * mixed).astype(BF16)
            return f

        def u_pool(gi):
            w = POOL_WINDOWS[gi]

            def f():
                cols = slice(gi * POOL_GROUP_DIM, (gi + 1) * POOL_GROUP_DIM)
                ext_pos = lax.broadcasted_iota(jnp.int32, (n_ext, 1), 0) + (i * ts + r0 - HALO)
                ext_valid = (ext_pos >= 0) & (ext_pos < s_len)
                zp = jnp.where(ext_valid, proj(ext, 2 * D + gi * POOL_GROUP_DIM), 0.0)
                acc = zp + pltpu.roll(zp, 1, 0)
                if w >= 4:
                    acc = pltpu.roll(acc, 1, 0) + pltpu.roll(acc, n_ext - 1, 0)
                if w >= 8:
                    acc = pltpu.roll(acc, 2, 0) + pltpu.roll(acc, n_ext - 2, 0)
                if w >= 16:
                    acc = pltpu.roll(acc, 4, 0) + pltpu.roll(acc, n_ext - 4, 0)
                inv_count = icnt_ref[rows, gi:gi + 1]
                dd = (acc[HALO:HALO + hs] * inv_count - zp[HALO:HALO + hs]).astype(BF16)
                p_scr[rows, cols] = (_dot(dd, pltpu.bitcast(wpool_ref[gi], BF16)) * psc_ref[:, cols]).astype(BF16)
            return f

        def u_merge(j):
            def f():
                cols = slice(j * COLB, (j + 1) * COLB)
                ga = jax.nn.sigmoid(proj(main, 3 * D + j * COLB))
                gb = jax.nn.sigmoid(proj(main, 4 * D + j * COLB))
                ta = _dot(a_scr[rows, :], wmat(wa_ref, cols=cols))
                tb = _dot(p_scr[rows, :], wmat(wb_ref, cols=cols))
                m_scr[rows, cols] = (ga * ta + gb * tb).astype(BF16)
            return f

        def u_out(j):
            def f():
                cols = slice(j * COLB, (j + 1) * COLB)
                t = _dot(m_scr[rows, :], wmat(wo_ref, cols=cols))
                y = alpha * x_rows(r0, r0 + hs, cols) + gate1[:, cols] * t
                v_scr[rows, cols] = y
                st["ysum"] = (st["ysum"] + jnp.sum(y, axis=-1, keepdims=True)) if j else jnp.sum(
                    y, axis=-1, keepdims=True)
            return f

        def row_rstd(mu):
            ss = jnp.zeros((hs, 1), F32)
            for j in range(ncb):
                cols = slice(j * COLB, (j + 1) * COLB)
                xc = v_scr[rows, cols] - mu
                ss = ss + jnp.sum(xc * xc, axis=-1, keepdims=True)
            return lax.rsqrt(ss * (1.0 / D) + LN_EPS)

        def u_ln1():
            ymu = st["ysum"] * (1.0 / D)
            yrs = row_rstd(ymu)
            xsum = jnp.zeros((hs, 1), F32)
            for j in range(ncb):
                cols = slice(j * COLB, (j + 1) * COLB)
                x1 = (v_scr[rows, cols] - ymu) * yrs * l1g_ref[:, cols] + l1b_ref[:, cols]
                x1_ref[0, rows, cols] = x1
                v_scr[rows, cols] = x1
                xsum = xsum + jnp.sum(x1, axis=-1, keepdims=True)
            st["xmu"] = xsum * (1.0 / D)

        def u_ln2():
            xmu = st["xmu"]
            xrs = row_rstd(xmu)
            for j in range(ncb):
                cols = slice(j * COLB, (j + 1) * COLB)
                h2 = (v_scr[rows, cols] - xmu) * xrs * (1.0 + scale2[:, cols]) + shift2[:, cols]
                h2_scr[rows, cols] = h2.astype(BF16)
            lo_bits = lax.bitcast_convert_type(h2_scr[rows, 0:half].astype(F32), U32)
            hi_bits = lax.bitcast_convert_type(h2_scr[rows, half:D].astype(F32), U32)
            h2u_ref[rows, 0:half] = (lo_bits >> 16) | (hi_bits & jnp.uint32(0xFFFF0000))

        def u_route():
            logits = _dot(h2_scr[rows, :], wmat(wr_ref)) + br_ref[...]
            lane = lax.broadcasted_iota(jnp.int32, (hs, ROUTER_LANES), 1)
            lane_f = lane.astype(F32)
            neg = -jnp.inf
            is_g = lane < N_GROUPS
            lg = jnp.where(is_g, logits, neg)
            mg = jnp.max(lg, axis=-1, keepdims=True)
            g_sel = jnp.min(jnp.where(lg == mg, lane_f, float(ROUTER_LANES)), axis=-1, keepdims=True)
            p_sel = 1.0 / jnp.sum(jnp.where(is_g, jnp.exp(logits - mg), 0.0), axis=-1, keepdims=True)
            e_lo = EXPERT_LANE0 + g_sel * EPG
            in_grp = (lane_f >= e_lo) & (lane_f < e_lo + EPG)
            le = jnp.where(in_grp, logits, neg)
            m1 = jnp.max(le, axis=-1, keepdims=True)
            i1 = jnp.min(jnp.where(le == m1, lane_f, float(ROUTER_LANES)), axis=-1, keepdims=True)
            le2 = jnp.where(lane_f == i1, neg, le)
            m2 = jnp.max(le2, axis=-1, keepdims=True)
            i2 = jnp.min(jnp.where(le2 == m2, lane_f, float(ROUTER_LANES)), axis=-1, keepdims=True)
            e2x = jnp.exp(m2 - m1)
            den = 1.0 + e2x
            g1 = p_sel / den
            g2 = p_sel * e2x / den

            j1 = i1 - e_lo
            j2 = i2 - e_lo
            first_is_a = j1 < j2
            ja = jnp.minimum(j1, j2)
            jb = jnp.maximum(j1, j2)
            cls = g_sel * PAIRS + (EPG - 1) * ja - ja * (ja - 1.0) * 0.5 + (jb - ja - 1.0)
            w_a = jnp.where(first_is_a, g1, g2)
            w_b = jnp.where(first_is_a, g2, g1)
            extra = jnp.where(lane == 0, w_a, jnp.where(lane == 1, w_b, 0.0))
            h2u_ref[rows, half:half + ROUTER_LANES] = lax.bitcast_convert_type(extra, U32)

            hit = lane_f == cls
            onehot = jnp.where(hit, 1.0, 0.0)
            pre = _dot(tri_ref[...], onehot.astype(BF16)) + carry_scr[0:1, :]
            rank = jnp.sum(jnp.where(hit, pre, 0.0), axis=-1, keepdims=True)
            carry_scr[0:1, :] = carry_scr[0:1, :] + jnp.sum(onehot, axis=0, keepdims=True)

            cr = jnp.where(lane == 0, cls, jnp.where(lane == 1, rank, 0.0))
            cr_ref[:, rows] = cr.T[0:8, :]

        for j in range(ncb):
            units.append(u_uv(j))
        units += [u_vln_a, u_vln_b]
        n_chunks = hs // CHUNK
        for k in range(max(n_chunks, len(POOL_WINDOWS))):
            if k < len(POOL_WINDOWS):
                units.append(u_pool(k))
            if k < n_chunks:
                units.append(u_sgu(k))
        for j in range(ncb):
            units.append(u_merge(j))
        for j in range(ncb):
            units.append(u_out(j))
        units += [u_ln1, u_ln2, u_route]
        return units

    unit_lists = [stream(si) for si in range(n_streams)]
    n_units = len(unit_lists[0])
    for k in range(n_units + skew * (n_streams - 1)):
        for si, units in enumerate(unit_lists):
            if 0 <= k - si * skew < n_units:
                units[k - si * skew]()
    cnt_ref[...] = carry_scr[...]


def _const_spec(shape):
    nd = len(shape)
    return pl.BlockSpec(shape, lambda b, i: (0,) * nd, pipeline_mode=pl.Buffered(1))


def _mixer(x_p, x_s, mod, wts, alpha, ts):
    nbp, s_len, _ = x_p.shape
    nbs = x_s.shape[0]
    assert x_s.shape[1] == s_len and s_len % ts == 0 and ts % CHUNK == 0
    nb = nbp + nbs
    n_i = s_len // ts
    hb = ts // HALO
    n_hb = s_len // HALO

    def pick(b, on_p, val, const):
        return jnp.where(b < nbp if on_p else b >= nbp, val, const)

    def main_map(on_p):
        def f(b, i):
            bb = pick(b, on_p, b if on_p else b - nbp, nbp - 1 if on_p else 0)
            ii = pick(b, on_p, i, n_i - 1 if on_p else 0)
            return (bb, ii, 0)
        return f

    def halo_map(on_p, nxt):
        def f(b, i):
            bb = pick(b, on_p, b if on_p else b - nbp, nbp - 1 if on_p else 0)
            idx = jnp.minimum((i + 1) * hb, n_hb - 1) if nxt else jnp.maximum(i * hb - 1, 0)
            ii = pick(b, on_p, idx, n_hb - 1 if on_p else 0)
            return (bb, ii, 0)
        return f

    hs = ts // MIXER_STREAMS
    assert hs % CHUNK == 0
    tri = (lax.broadcasted_iota(jnp.int32, (hs, hs), 1)
           < lax.broadcasted_iota(jnp.int32, (hs, hs), 0)).astype(BF16)
    consts = list(wts) + [tri]
    in_specs = [
        pl.BlockSpec((1, ts, D), main_map(True)),
        pl.BlockSpec((1, ts, D), main_map(False)),
        pl.BlockSpec((1, HALO, D), halo_map(True, False)),
        pl.BlockSpec((1, HALO, D), halo_map(False, False)),
        pl.BlockSpec((1, HALO, D), halo_map(True, True)),
        pl.BlockSpec((1, HALO, D), halo_map(False, True)),
        pl.BlockSpec((1, N_MOD, D), lambda b, i: (b, 0, 0)),
        pl.BlockSpec((ts, ROUTER_LANES), lambda b, i: (i, 0)),
    ] + [_const_spec(w.shape) for w in consts]

    pos = jnp.arange(s_len, dtype=jnp.int32)[:, None]
    win = jnp.asarray(POOL_WINDOWS + (1,) * (ROUTER_LANES - len(POOL_WINDOWS)), jnp.int32)[None, :]
    inv_count = 1.0 / (jnp.minimum(pos + (win - 1 - win // 2), s_len - 1)
                       - jnp.maximum(pos - win // 2, 0) + 1).astype(F32)
    n_tok = nb * s_len
    out_shape = (
        jax.ShapeDtypeStruct((nb, s_len, D), F32),
        jax.ShapeDtypeStruct((n_tok, ROW_W), U32),
        jax.ShapeDtypeStruct((8, n_tok), F32),
        jax.ShapeDtypeStruct((8, ROUTER_LANES), F32),
    )
    out_specs = (
        pl.BlockSpec((1, ts, D), lambda b, i: (b, i, 0)),
        pl.BlockSpec((ts, ROW_W), lambda b, i: (b * n_i + i, 0)),
        pl.BlockSpec((8, ts), lambda b, i: (0, b * n_i + i)),
        pl.BlockSpec((8, ROUTER_LANES), lambda b, i: (0, 0)),
    )
    scratch = [
        pltpu.VMEM((ts + 2 * HALO * MIXER_STREAMS, D), BF16),
        pltpu.VMEM((ts, D), F32),
        pltpu.VMEM((ts, D), F32),
        pltpu.VMEM((ts, D), BF16),
        pltpu.VMEM((ts, D), BF16),
        pltpu.VMEM((ts, D), BF16),
        pltpu.VMEM((ts, D), BF16),
        pltpu.VMEM((ts, D), BF16),
        pltpu.VMEM((8, ROUTER_LANES), F32),
    ]
    return pl.pallas_call(
        functools.partial(_mixer_kernel, nbp, ts, s_len, alpha, MIXER_STREAMS, MIXER_SKEW),
        grid=(nb, n_i),
        in_specs=in_specs,
        out_specs=out_specs,
        out_shape=out_shape,
        scratch_shapes=scratch,
        compiler_params=pltpu.CompilerParams(
            dimension_semantics=("arbitrary", "arbitrary"), vmem_limit_bytes=VMEM_LIMIT),
        name="mixer",
    )(x_p, x_s, x_p, x_s, x_p, x_s, mod, inv_count, *consts)


def _dest_kernel(seg_ref, cls_ref, rank_ref, o_ref):
    cls = cls_ref[...]
    acc = rank_ref[...].astype(jnp.int32)
    for c in range(N_CLASSES):
        acc = acc + jnp.where(cls == float(c), seg_ref[c], 0)
    o_ref[...] = acc


def _dest(seg_start, cls, rank):
    shape = cls.shape
    return pl.pallas_call(
        _dest_kernel,
        grid_spec=pltpu.PrefetchScalarGridSpec(
            num_scalar_prefetch=1,
            grid=(1,),
            in_specs=[pl.BlockSpec(shape, lambda i, s: (0, 0)), pl.BlockSpec(shape, lambda i, s: (0, 0))],
            out_specs=pl.BlockSpec(shape, lambda i, s: (0, 0)),
        ),
        out_shape=jax.ShapeDtypeStruct(shape, jnp.int32),
        name="dest",
    )(seg_start, cls, rank)


def _sc_mesh():
    return plsc.VectorSubcoreMesh(core_axis_name="core", subcore_axis_name="subcore")


def _sc_scatter_rows(rows, dest, n_rows, win):
    n_tok, row_w = rows.shape
    mesh = _sc_mesh()
    n_workers = mesh.num_cores * mesh.num_subcores
    n_steps = n_tok // win
    assert n_tok % win == 0 and n_steps % n_workers == 0
    per_worker = n_steps // n_workers

    @pl.kernel(out_type=jax.ShapeDtypeStruct((n_rows, row_w), rows.dtype), mesh=mesh,
               scratch_types=[pltpu.VMEM((win,), jnp.int32), pltpu.VMEM((win, row_w), rows.dtype)])
    def scatter(rows_hbm, idx_hbm, out_hbm, idx_vmem, buf):
        worker = lax.axis_index("core") * mesh.num_subcores + lax.axis_index("subcore")

        @pl.loop(0, per_worker)
        def _(k):
            step = worker * per_worker + k
            pltpu.sync_copy(idx_hbm.at[step], idx_vmem)
            pltpu.sync_copy(rows_hbm.at[pl.ds(step * win, win)], buf)
            pltpu.sync_copy(buf, out_hbm.at[idx_vmem])

    return scatter(rows, dest.reshape(n_steps, win))


def _sc_gather_rows(table, idx, win):
    n_tok = idx.shape[0]
    row_w = table.shape[1]
    mesh = _sc_mesh()
    n_workers = mesh.num_cores * mesh.num_subcores
    n_steps = n_tok // win
    assert n_tok % win == 0 and n_steps % n_workers == 0
    per_worker = n_steps // n_workers

    @pl.kernel(out_type=jax.ShapeDtypeStruct((n_tok, row_w), table.dtype), mesh=mesh,
               scratch_types=[pltpu.VMEM((win,), jnp.int32), pltpu.VMEM((win, row_w), table.dtype)])
    def gather(table_hbm, idx_hbm, out_hbm, idx_vmem, buf):
        worker = lax.axis_index("core") * mesh.num_subcores + lax.axis_index("subcore")

        @pl.loop(0, per_worker)
        def _(k):
            step = worker * per_worker + k
            pltpu.sync_copy(idx_hbm.at[step], idx_vmem)
            pltpu.sync_copy(table_hbm.at[idx_vmem], buf)
            pltpu.sync_copy(buf, out_hbm.at[pl.ds(step * win, win)])

    return gather(table, idx.reshape(n_steps, win))


def _expert_kernel(bm, ea_ref, eb_ref, nused_ref, xs_ref, wga_ref, wua_ref, wda_ref, wgb_ref, wub_ref, wdb_ref,
                   ys_ref, act_scr):
    j = pl.program_id(0)
    used = j < nused_ref[0]
    half = D // 2

    def wmat(ref, k0, k1, cols):
        return pltpu.bitcast(ref[0, k0 // 2:k1 // 2, cols], BF16)

    @pl.when(used)
    def _():
        bits = xs_ref[:, 0:half]
        lo = lax.bitcast_convert_type(bits << 16, F32).astype(BF16)
        hi = lax.bitcast_convert_type(bits & jnp.uint32(0xFFFF0000), F32).astype(BF16)
        wts = lax.bitcast_convert_type(xs_ref[:, half:half + ROUTER_LANES], F32)
        for e, (wg_ref, wu_ref) in enumerate(((wga_ref, wua_ref), (wgb_ref, wub_ref))):
            for c in range(D_EXPERT // COLB):
                cols = slice(c * COLB, (c + 1) * COLB)
                g = _dot(lo, wmat(wg_ref, 0, half, cols)) + _dot(hi, wmat(wg_ref, half, D, cols))
                up = _dot(lo, wmat(wu_ref, 0, half, cols)) + _dot(hi, wmat(wu_ref, half, D, cols))
                act_scr[e, :, cols] = (jax.nn.silu(g) * up).astype(BF16)
        w_a = wts[:, 0:1]
        w_b = wts[:, 1:2]

        def y_block(c0):
            cols = slice(c0, c0 + COLB)
            return (w_a * _dot(act_scr[0], wmat(wda_ref, 0, D_EXPERT, cols))
                    + w_b * _dot(act_scr[1], wmat(wdb_ref, 0, D_EXPERT, cols)))

        for c0 in range(0, half, COLB):
            lo_bits = lax.bitcast_convert_type(y_block(c0).astype(BF16).astype(F32), U32)
            hi_bits = lax.bitcast_convert_type(y_block(half + c0).astype(BF16).astype(F32), U32)
            ys_ref[:, c0:c0 + COLB] = (lo_bits >> 16) | (hi_bits & jnp.uint32(0xFFFF0000))

    @pl.when(jnp.logical_not(used))
    def _():
        ys_ref[...] = jnp.zeros_like(ys_ref)


def _experts(xs, blk_a, blk_b, n_used, w_gate, w_up, w_down, bm):
    n_rows = xs.shape[0]
    n_blocks = n_rows // bm

    def x_map(j, ea, eb, nu):
        return (jnp.minimum(j, nu[0] - 1), 0)

    def a_map(j, ea, eb, nu):
        return (ea[j], 0, 0)

    def b_map(j, ea, eb, nu):
        return (eb[j], 0, 0)

    return pl.pallas_call(
        functools.partial(_expert_kernel, bm),
        grid_spec=pltpu.PrefetchScalarGridSpec(
            num_scalar_prefetch=3,
            grid=(n_blocks,),
            in_specs=[pl.BlockSpec((bm, ROW_W), x_map),
                      pl.BlockSpec((1, D // 2, D_EXPERT), a_map),
                      pl.BlockSpec((1, D // 2, D_EXPERT), a_map),
                      pl.BlockSpec((1, D_EXPERT // 2, D), a_map),
                      pl.BlockSpec((1, D // 2, D_EXPERT), b_map),
                      pl.BlockSpec((1, D // 2, D_EXPERT), b_map),
                      pl.BlockSpec((1, D_EXPERT // 2, D), b_map)],
            out_specs=pl.BlockSpec((bm, D // 2), lambda j, ea, eb, nu: (j, 0)),
            scratch_shapes=[pltpu.VMEM((2, bm, D_EXPERT), BF16)],
        ),
        out_shape=jax.ShapeDtypeStruct((n_rows, D // 2), U32),
        compiler_params=pltpu.CompilerParams(
            dimension_semantics=("arbitrary",), vmem_limit_bytes=VMEM_LIMIT),
        name="experts",
    )(blk_a, blk_b, n_used, xs, w_gate, w_up, w_down, w_gate, w_up, w_down)


def _final_dense_kernel(alpha, x1_ref, f_ref, mod_ref, l2g_ref, l2b_ref, out_ref):
    half = D // 2
    gate2 = mod_ref[0][5:6]
    bits = f_ref[...]
    f_lo = lax.bitcast_convert_type(bits << 16, F32)
    f_hi = lax.bitcast_convert_type(bits & jnp.uint32(0xFFFF0000), F32)
    y_lo = alpha * x1_ref[0, :, 0:half] + gate2[:, 0:half] * f_lo
    y_hi = alpha * x1_ref[0, :, half:D] + gate2[:, half:D] * f_hi
    mu = (jnp.sum(y_lo, axis=-1, keepdims=True) + jnp.sum(y_hi, axis=-1, keepdims=True)) * (1.0 / D)
    c_lo = y_lo - mu
    c_hi = y_hi - mu
    var = (jnp.sum(c_lo * c_lo, axis=-1, keepdims=True) + jnp.sum(c_hi * c_hi, axis=-1, keepdims=True)) * (1.0 / D)
    rs = lax.rsqrt(var + LN_EPS)
    out_ref[0, :, 0:half] = c_lo * rs * l2g_ref[:, 0:half] + l2b_ref[:, 0:half]
    out_ref[0, :, half:D] = c_hi * rs * l2g_ref[:, half:D] + l2b_ref[:, half:D]


def _final_dense(x1, f, mod, ln2_g, ln2_b, b_off, nbg, alpha, tk):
    s_len = x1.shape[1]
    n_i = s_len // tk
    return pl.pallas_call(
        functools.partial(_final_dense_kernel, alpha),
        grid=(nbg, n_i),
        in_specs=[pl.BlockSpec((1, tk, D), lambda b, i: (b + b_off, i, 0)),
                  pl.BlockSpec((tk, D // 2), lambda b, i: (b * n_i + i, 0)),
                  pl.BlockSpec((1, N_MOD, D), lambda b, i: (b + b_off, 0, 0)),
                  pl.BlockSpec((1, D), lambda b, i: (0, 0)),
                  pl.BlockSpec((1, D), lambda b, i: (0, 0))],
        out_specs=pl.BlockSpec((1, tk, D), lambda b, i: (b, i, 0)),
        out_shape=jax.ShapeDtypeStruct((nbg, s_len, D), F32),
        compiler_params=pltpu.CompilerParams(
            dimension_semantics=("arbitrary", "arbitrary"), vmem_limit_bytes=VMEM_LIMIT),
        name="final",
    )(x1, f, mod, ln2_g, ln2_b)


def _layer(x_p, x_s, c_all, p, alpha):
    nbp, s_len, _ = x_p.shape
    nbs = x_s.shape[0]
    nb = nbp + nbs
    n_tok = nb * s_len

    mod = _modulation(c_all, p["w_mod"], p["b_mod"]).reshape(nb, N_MOD, D)

    bsf = jnp.repeat(p["b_spatial"].T, HEAD_DIM, axis=1)
    n_in = p["w_in"].shape[1]
    w_router = jnp.concatenate(
        [p["w_router_group"], p["w_router_expert"],
         jnp.zeros((D, ROUTER_LANES - N_GROUPS - N_EXPERTS), F32)], axis=1)
    b_router = jnp.concatenate(
        [p["b_router_group"], p["b_router_expert"],
         jnp.zeros((ROUTER_LANES - N_GROUPS - N_EXPERTS,), F32)]).reshape(1, ROUTER_LANES)
    wts = [
        _pack_rows(p["w_in"]), p["b_in"].reshape(1, n_in),
        p["w_spatial"].astype(BF16), bsf,
        p["sgu_g"].reshape(1, D), p["sgu_b"].reshape(1, D),
        _pack_rows(p["w_pool"]), p["pool_scale"].reshape(1, D),
        _pack_rows(p["w_branch_a"]), _pack_rows(p["w_branch_b"]), _pack_rows(p["w_out"]),
        p["ln1_g"].reshape(1, D), p["ln1_b"].reshape(1, D),
        _pack_rows(w_router), b_router,
    ]
    x1, rows, cr, cnt = _mixer(x_p, x_s, mod, wts, alpha, MIXER_TS)

    bm = EXPERT_BM
    assert n_tok % bm == 0 and n_tok % DEST_LANES == 0
    counts = cnt[0, :N_CLASSES].astype(jnp.int32)
    padded = (counts + bm - 1) // bm * bm
    seg_end = jnp.cumsum(padded).astype(jnp.int32)
    seg_start = seg_end - padded
    n_blocks = n_tok // bm + N_CLASSES
    block_start = jnp.arange(n_blocks, dtype=jnp.int32) * bm
    blk_cls = jnp.minimum(
        jnp.sum((seg_end[None, :] <= block_start[:, None]).astype(jnp.int32), axis=1), N_CLASSES - 1)
    cls_a, cls_b = _class_experts()
    blk_a = jnp.asarray(cls_a)[blk_cls]
    blk_b = jnp.asarray(cls_b)[blk_cls]
    n_used = (seg_end[-1:] // bm).astype(jnp.int32)

    lane_shape = (n_tok // DEST_LANES, DEST_LANES)
    dest = _dest(seg_start, cr[0].reshape(lane_shape), cr[1].reshape(lane_shape)).reshape(n_tok)

    xs = _sc_scatter_rows(rows, dest, n_blocks * bm, SC_SCATTER_WIN)
    ys = _experts(xs, blk_a, blk_b, n_used,
                  _pack_rows(p["w_exp_gate"]), _pack_rows(p["w_exp_up"]), _pack_rows(p["w_exp_down"]), bm)

    l2g = p["ln2_g"].reshape(1, D)
    l2b = p["ln2_b"].reshape(1, D)
    tp = nbp * s_len
    f_p = _sc_gather_rows(ys, dest[:tp], SC_GATHER_WIN)
    f_s = _sc_gather_rows(ys, dest[tp:], SC_GATHER_WIN)
    y_p = _final_dense(x1, f_p, mod, l2g, l2b, 0, nbp, alpha, FINAL_TK)
    y_s = _final_dense(x1, f_s, mod, l2g, l2b, nbp, nbs, alpha, FINAL_TK)
    return y_p, y_s


def _pack_kernel(w_ref, o_ref):
    o_ref[...] = pltpu.bitcast(w_ref[...].astype(BF16), U32)


def _pack_rows(w):
    *lead, k, n = w.shape
    rows = math.prod(lead) * k
    rb = min(rows, 1024)
    nb = min(n, 1024)
    assert k % 2 == 0 and rows % rb == 0 and n % nb == 0
    packed = pl.pallas_call(
        _pack_kernel,
        grid=(rows // rb, n // nb),
        in_specs=[pl.BlockSpec((rb, nb), lambda i, j: (i, j))],
        out_specs=pl.BlockSpec((rb // 2, nb), lambda i, j: (i, j)),
        out_shape=jax.ShapeDtypeStruct((rows // 2, n), U32),
        name="pack",
    )(w.reshape(rows, n))
    return packed.reshape(*lead, k // 2, n)


def _class_experts():
    ea, eb = [], []
    for g in range(N_GROUPS):
        for a in range(EPG):
            for b in range(a + 1, EPG):
                ea.append(g * EPG + a)
                eb.append(g * EPG + b)
    return np.asarray(ea, np.int32), np.asarray(eb, np.int32)


_PARAM_NAMES = ("w_mod", "b_mod", "w_in", "b_in", "w_spatial", "b_spatial", "sgu_g", "sgu_b", "w_pool",
                "pool_scale", "w_branch_a", "w_branch_b", "w_out", "ln1_g", "ln1_b", "w_router_group",
                "b_router_group", "w_router_expert", "b_router_expert", "w_exp_gate", "w_exp_up",
                "w_exp_down", "ln2_g", "ln2_b")


def kernel(x_prompt, x_sample, c_prompt, c_sample, w_mod, b_mod, w_in, b_in, w_spatial, b_spatial, sgu_g, sgu_b, w_pool, pool_scale, w_branch_a, w_branch_b, w_out, ln1_g, ln1_b, w_router_group, b_router_group, w_router_expert, b_router_expert, w_exp_gate, w_exp_up, w_exp_down, ln2_g, ln2_b):
    params = (w_mod, b_mod, w_in, b_in, w_spatial, b_spatial, sgu_g, sgu_b, w_pool, pool_scale,
              w_branch_a, w_branch_b, w_out, ln1_g, ln1_b, w_router_group, b_router_group,
              w_router_expert, b_router_expert, w_exp_gate, w_exp_up, w_exp_down, ln2_g, ln2_b)
    depth = w_mod.shape[0]
    alpha = (2.0 * depth) ** 0.25
    c_all = jnp.concatenate([c_prompt, c_sample], axis=0)
    y_p, y_s = x_prompt, x_sample
    for l in range(depth):
        p = {name: w[l] for name, w in zip(_PARAM_NAMES, params)}
        y_p, y_s = _layer(y_p, y_s, c_all, p, alpha)
    return (y_p, y_s)
```

```python
import functools
import math

import jax
import jax.numpy as jnp
from jax import lax
from jax.experimental import pallas as pl
from jax.experimental.pallas import tpu as pltpu
from jax.experimental.pallas import tpu_sc as plsc

F32 = jnp.float32
BF16 = jnp.bfloat16
U32 = jnp.uint32

D = 1024
CHUNK = 128
SGU_HEADS = 8
HEAD_DIM = D // SGU_HEADS
POOL_WINDOWS = (2, 4, 8, 16)
POOL_GROUP_DIM = D // len(POOL_WINDOWS)
N_MOD = 6
N_GROUPS = 4
EPG = 8
N_EXPERTS = N_GROUPS * EPG
D_EXPERT = D // 2
LN_EPS = 1e-5

HALO = 16
COLB = 256
ROUTER_LANES = 128
EXPERT_LANE0 = N_GROUPS
ROW_W = D // 2
MIXER_TS = 512
MIXER_STREAMS = 1
MIXER_SKEW = 0
EXPERT_BM = 256
FINAL_TK = 1024
DEST_LANES = 512
PACK_BLOCK_ELEMS = 1024 * 1024
SC_SCATTER_WIN = 128
SC_GATHER_WIN = 128
VMEM_LIMIT = 56 * 1024 * 1024


def _ln(x):
    mu = jnp.mean(x, axis=-1, keepdims=True)
    xc = x - mu
    var = jnp.mean(xc * xc, axis=-1, keepdims=True)
    return xc * lax.rsqrt(var + LN_EPS)


_GELU_A = -2.0 * math.sqrt(2.0 / math.pi) * math.log2(math.e)
_GELU_B = _GELU_A * 0.044715


def _gelu_tanh(x):
    return x / (1.0 + jnp.exp2(x * (_GELU_A + _GELU_B * (x * x))))


def _dot(a, b):
    return jnp.dot(a, b, preferred_element_type=F32)


def _mod_kernel(c_ref, w_ref, b_ref, o_ref):
    a = jax.nn.silu(c_ref[...]).astype(BF16)
    o_ref[...] = _dot(a, w_ref[...].astype(BF16)) + b_ref[...]


def _modulation(c_all, w_mod, b_mod):
    nb = c_all.shape[0]
    n_out = w_mod.shape[1]
    cb = 512
    return pl.pallas_call(
        _mod_kernel,
        grid=(n_out // cb,),
        in_specs=[pl.BlockSpec((nb, D), lambda j: (0, 0)),
                  pl.BlockSpec((D, cb), lambda j: (0, j)),
                  pl.BlockSpec((1, cb), lambda j: (0, j))],
        out_specs=pl.BlockSpec((nb, cb), lambda j: (0, j)),
        out_shape=jax.ShapeDtypeStruct((nb, n_out), F32),
        name="mod",
    )(c_all, w_mod, b_mod.reshape(1, n_out))


def _mixer_kernel(nbp, ts, s_len, alpha, n_streams, skew,
                   xp_ref, xs_ref, xpp_ref, xsp_ref, xpn_ref, xsn_ref, mod_ref, icnt_ref,
                   win_ref, bin_ref, ws_ref, bsf_ref, sg_ref, sb_ref, wpool_ref, psc_ref,
                   wa_ref, wb_ref, wo_ref, l1g_ref, l1b_ref, wr_ref, br_ref, tri_ref,
                   x1_ref, h2u_ref, gw_ref, cr_ref, cnt_ref,
                   h_scr, u_scr, v_scr, vb_scr, a_scr, p_scr, m_scr, h2_scr, carry_scr):
    b = pl.program_id(0)
    i = pl.program_id(1)
    is_p = b < nbp
    ncb = D // COLB
    hs = ts // n_streams
    n_ext = hs + 2 * HALO
    half = D // 2

    def wmat(ref, rows=slice(None), cols=slice(None)):
        return pltpu.bitcast(ref[rows, cols], BF16)

    @pl.when((b == 0) & (i == 0))
    def _():
        carry_scr[...] = jnp.zeros_like(carry_scr)

    mod = mod_ref[0]
    shift1, scale1, gate1 = mod[0:1], mod[1:2], mod[2:3]
    shift2, scale2 = mod[3:4], mod[4:5]

    def adaln1(xv):
        return (_ln(xv) * (1.0 + scale1) + shift1).astype(BF16)

    def x_rows(lo, hi, cols=slice(None)):
        return jnp.where(is_p, xp_ref[0, lo:hi, cols], xs_ref[0, lo:hi, cols])

    def stream(si):
        r0 = si * hs
        rows = slice(r0, r0 + hs)
        e0 = si * n_ext
        ext = slice(e0, e0 + n_ext)
        main = slice(e0 + HALO, e0 + HALO + hs)
        st = {}
        units = []

        def proj(hrows, c0):
            return _dot(h_scr[hrows, :], wmat(win_ref, cols=slice(c0, c0 + COLB))) + bin_ref[:, c0:c0 + COLB]

        n_ln = 1
        qs = hs // n_ln

        def u_ln(q):
            def f():
                if q == 0:
                    top = jnp.where(is_p, xpp_ref[0], xsp_ref[0]) if si == 0 else x_rows(r0 - HALO, r0)
                    h_scr[e0:e0 + HALO, :] = adaln1(top)
                    st["vsum"] = jnp.zeros((hs, 1), F32)
                if q == n_ln - 1:
                    bot = (jnp.where(is_p, xpn_ref[0], xsn_ref[0]) if si == n_streams - 1
                           else x_rows(r0 + hs, r0 + hs + HALO))
                    h_scr[e0 + HALO + hs:e0 + n_ext, :] = adaln1(bot)
                h_scr[e0 + HALO + q * qs:e0 + HALO + (q + 1) * qs, :] = adaln1(
                    x_rows(r0 + q * qs, r0 + (q + 1) * qs))
            return f
        for q in range(n_ln):
            units.append(u_ln(q))

        def u_uv(j):
            def f():
                cols = slice(j * COLB, (j + 1) * COLB)
                u_scr[rows, cols] = _gelu_tanh(proj(main, j * COLB))
                gv = _gelu_tanh(proj(main, D + j * COLB))
                v_scr[rows, cols] = gv
                st["vsum"] = st["vsum"] + jnp.sum(gv, axis=-1, keepdims=True)
            return f

        def u_vln_a():
            st["vmu"] = st["vsum"] * (1.0 / D)
            vss = jnp.zeros((hs, 1), F32)
            for j in range(ncb):
                cols = slice(j * COLB, (j + 1) * COLB)
                xc = v_scr[rows, cols] - st["vmu"]
                vss = vss + jnp.sum(xc * xc, axis=-1, keepdims=True)
            st["vrs"] = lax.rsqrt(vss * (1.0 / D) + LN_EPS)

        def u_vln_b():
            for j in range(ncb):
                cols = slice(j * COLB, (j + 1) * COLB)
                vb_scr[rows, cols] = ((v_scr[rows, cols] - st["vmu"]) * st["vrs"] * sg_ref[:, cols]
                                      + sb_ref[:, cols]).astype(BF16)

        def u_sgu(c):
            def f():
                crow = slice(r0 + c * CHUNK, r0 + (c + 1) * CHUNK)
                for hh in range(SGU_HEADS):
                    cols = slice(hh * HEAD_DIM, (hh + 1) * HEAD_DIM)
                    mixed = _dot(ws_ref[hh], vb_scr[crow, cols]) + bsf_ref[:, cols]
                    a_scr[crow, cols] = (u_scr[crow, cols] * mixed).astype(BF16)
            return f

        def u_pool(gi):
            w = POOL_WINDOWS[gi]

            def f():
                cols = slice(gi * POOL_GROUP_DIM, (gi + 1) * POOL_GROUP_DIM)
                ext_pos = lax.broadcasted_iota(jnp.int32, (n_ext, 1), 0) + (i * ts + r0 - HALO)
                ext_valid = (ext_pos >= 0) & (ext_pos < s_len)
                zp = jnp.where(ext_valid, proj(ext, 2 * D + gi * POOL_GROUP_DIM), 0.0)
                acc = zp + pltpu.roll(zp, 1, 0)
                if w >= 4:
                    acc = pltpu.roll(acc, 1, 0) + pltpu.roll(acc, n_ext - 1, 0)
                if w >= 8:
                    acc = pltpu.roll(acc, 2, 0) + pltpu.roll(acc, n_ext - 2, 0)
                if w >= 16:
                    acc = pltpu.roll(acc, 4, 0) + pltpu.roll(acc, n_ext - 4, 0)
                inv_count = icnt_ref[rows, gi:gi + 1]
                dd = (acc[HALO:HALO + hs] * inv_count - zp[HALO:HALO + hs]).astype(BF16)
                p_scr[rows, cols] = (_dot(dd, pltpu.bitcast(wpool_ref[gi], BF16)) * psc_ref[:, cols]).astype(BF16)
            return f

        def u_merge(j):
            def f():
                cols = slice(j * COLB, (j + 1) * COLB)
                ga = jax.nn.sigmoid(proj(main, 3 * D + j * COLB))
                gb = jax.nn.sigmoid(proj(main, 4 * D + j * COLB))
                ta = _dot(a_scr[rows, :], wmat(wa_ref, cols=cols))
                tb = _dot(p_scr[rows, :], wmat(wb_ref, cols=cols))
                m_scr[rows, cols] = (ga * ta + gb * tb).astype(BF16)
            return f

        def u_out(j):
            def f():
                cols = slice(j * COLB, (j + 1) * COLB)
                t = _dot(m_scr[rows, :], wmat(wo_ref, cols=cols))
                y = alpha * x_rows(r0, r0 + hs, cols) + gate1[:, cols] * t
                v_scr[rows, cols] = y
                st["ysum"] = (st["ysum"] + jnp.sum(y, axis=-1, keepdims=True)) if j else jnp.sum(
                    y, axis=-1, keepdims=True)
            return f

        def row_rstd(mu):
            ss = jnp.zeros((hs, 1), F32)
            for j in range(ncb):
                cols = slice(j * COLB, (j + 1) * COLB)
                xc = v_scr[rows, cols] - mu
                ss = ss + jnp.sum(xc * xc, axis=-1, keepdims=True)
            return lax.rsqrt(ss * (1.0 / D) + LN_EPS)

        def u_ln1():
            ymu = st["ysum"] * (1.0 / D)
            yrs = row_rstd(ymu)
            xsum = jnp.zeros((hs, 1), F32)
            for j in range(ncb):
                cols = slice(j * COLB, (j + 1) * COLB)
                x1 = (v_scr[rows, cols] - ymu) * yrs * l1g_ref[:, cols] + l1b_ref[:, cols]
                x1_ref[0, rows, cols] = x1
                v_scr[rows, cols] = x1
                xsum = xsum + jnp.sum(x1, axis=-1, keepdims=True)
            st["xmu"] = xsum * (1.0 / D)

        def u_ln2():
            xmu = st["xmu"]
            xrs = row_rstd(xmu)
            for j in range(ncb):
                cols = slice(j * COLB, (j + 1) * COLB)
                h2 = (v_scr[rows, cols] - xmu) * xrs * (1.0 + scale2[:, cols]) + shift2[:, cols]
                h2_scr[rows, cols] = h2.astype(BF16)
            lo_bits = lax.bitcast_convert_type(h2_scr[rows, 0:half].astype(F32), U32)
            hi_bits = lax.bitcast_convert_type(h2_scr[rows, half:D].astype(F32), U32)
            h2u_ref[rows, :] = (lo_bits >> 16) | (hi_bits & jnp.uint32(0xFFFF0000))

        def u_route():
            logits = _dot(h2_scr[rows, :], wmat(wr_ref)) + br_ref[...]
            lane = lax.broadcasted_iota(jnp.int32, (hs, ROUTER_LANES), 1)
            lane_f = lane.astype(F32)
            neg = -jnp.inf
            is_g = lane < N_GROUPS
            lg = jnp.where(is_g, logits, neg)
            mg = jnp.max(lg, axis=-1, keepdims=True)
            g_sel = jnp.min(jnp.where(lg == mg, lane_f, float(ROUTER_LANES)), axis=-1, keepdims=True)
            p_sel = 1.0 / jnp.sum(jnp.where(is_g, jnp.exp(logits - mg), 0.0), axis=-1, keepdims=True)
            e_lo = EXPERT_LANE0 + g_sel * EPG
            in_grp = (lane_f >= e_lo) & (lane_f < e_lo + EPG)
            le = jnp.where(in_grp, logits, neg)
            m1 = jnp.max(le, axis=-1, keepdims=True)
            i1 = jnp.min(jnp.where(le == m1, lane_f, float(ROUTER_LANES)), axis=-1, keepdims=True)
            le2 = jnp.where(lane_f == i1, neg, le)
            m2 = jnp.max(le2, axis=-1, keepdims=True)
            i2 = jnp.min(jnp.where(le2 == m2, lane_f, float(ROUTER_LANES)), axis=-1, keepdims=True)
            e2x = jnp.exp(m2 - m1)
            den = 1.0 + e2x
            g1 = p_sel / den
            g2 = p_sel * e2x / den

            gw_ref[rows, :] = jnp.where(lane == 0, g1, jnp.where(lane == 1, g2, 0.0))

            hit1 = lane_f == i1
            hit2 = lane_f == i2
            onehot = jnp.where(hit1 | hit2, 1.0, 0.0)
            pre = _dot(tri_ref[...], onehot.astype(BF16)) + carry_scr[0:1, :]
            r1 = jnp.sum(jnp.where(hit1, pre, 0.0), axis=-1, keepdims=True)
            r2 = jnp.sum(jnp.where(hit2, pre, 0.0), axis=-1, keepdims=True)
            carry_scr[0:1, :] = carry_scr[0:1, :] + jnp.sum(onehot, axis=0, keepdims=True)

            er = jnp.where(lane == 0, i1 - EXPERT_LANE0, jnp.where(lane == 1, i2 - EXPERT_LANE0, 0.0))
            er = jnp.where(lane == 2, r1, jnp.where(lane == 3, r2, er))
            cr_ref[:, rows] = er.T[0:8, :]

        for j in range(ncb):
            units.append(u_uv(j))
        units += [u_vln_a, u_vln_b]
        n_chunks = hs // CHUNK
        for k in range(max(n_chunks, len(POOL_WINDOWS))):
            if k < len(POOL_WINDOWS):
                units.append(u_pool(k))
            if k < n_chunks:
                units.append(u_sgu(k))
        for j in range(ncb):
            units.append(u_merge(j))
        for j in range(ncb):
            units.append(u_out(j))
        units += [u_ln1, u_ln2, u_route]
        return units

    unit_lists = [stream(si) for si in range(n_streams)]
    n_units = len(unit_lists[0])
    for k in range(n_units + skew * (n_streams - 1)):
        for si, units in enumerate(unit_lists):
            if 0 <= k - si * skew < n_units:
                units[k - si * skew]()
    cnt_ref[...] = carry_scr[...]


def _const_spec(shape):
    nd = len(shape)
    return pl.BlockSpec(shape, lambda b, i: (0,) * nd, pipeline_mode=pl.Buffered(1))


def _mixer(x_p, x_s, mod, wts, alpha, ts):
    nbp, s_len, _ = x_p.shape
    nbs = x_s.shape[0]
    assert x_s.shape[1] == s_len and s_len % ts == 0 and ts % CHUNK == 0
    nb = nbp + nbs
    n_i = s_len // ts
    hb = ts // HALO
    n_hb = s_len // HALO

    def pick(b, on_p, val, const):
        return jnp.where(b < nbp if on_p else b >= nbp, val, const)

    def main_map(on_p):
        def f(b, i):
            bb = pick(b, on_p, b if on_p else b - nbp, nbp - 1 if on_p else 0)
            ii = pick(b, on_p, i, n_i - 1 if on_p else 0)
            return (bb, ii, 0)
        return f

    def halo_map(on_p, nxt):
        def f(b, i):
            bb = pick(b, on_p, b if on_p else b - nbp, nbp - 1 if on_p else 0)
            idx = jnp.minimum((i + 1) * hb, n_hb - 1) if nxt else jnp.maximum(i * hb - 1, 0)
            ii = pick(b, on_p, idx, n_hb - 1 if on_p else 0)
            return (bb, ii, 0)
        return f

    hs = ts // MIXER_STREAMS
    assert hs % CHUNK == 0
    tri = (lax.broadcasted_iota(jnp.int32, (hs, hs), 1)
           < lax.broadcasted_iota(jnp.int32, (hs, hs), 0)).astype(BF16)
    consts = list(wts) + [tri]
    in_specs = [
        pl.BlockSpec((1, ts, D), main_map(True)),
        pl.BlockSpec((1, ts, D), main_map(False)),
        pl.BlockSpec((1, HALO, D), halo_map(True, False)),
        pl.BlockSpec((1, HALO, D), halo_map(False, False)),
        pl.BlockSpec((1, HALO, D), halo_map(True, True)),
        pl.BlockSpec((1, HALO, D), halo_map(False, True)),
        pl.BlockSpec((1, N_MOD, D), lambda b, i: (b, 0, 0)),
        pl.BlockSpec((ts, ROUTER_LANES), lambda b, i: (i, 0)),
    ] + [_const_spec(w.shape) for w in consts]

    pos = jnp.arange(s_len, dtype=jnp.int32)[:, None]
    win = jnp.asarray(POOL_WINDOWS + (1,) * (ROUTER_LANES - len(POOL_WINDOWS)), jnp.int32)[None, :]
    inv_count = 1.0 / (jnp.minimum(pos + (win - 1 - win // 2), s_len - 1)
                       - jnp.maximum(pos - win // 2, 0) + 1).astype(F32)
    n_tok = nb * s_len
    out_shape = (
        jax.ShapeDtypeStruct((nb, s_len, D), F32),
        jax.ShapeDtypeStruct((n_tok, ROW_W), U32),
        jax.ShapeDtypeStruct((n_tok, ROUTER_LANES), F32),
        jax.ShapeDtypeStruct((8, n_tok), F32),
        jax.ShapeDtypeStruct((8, ROUTER_LANES), F32),
    )
    out_specs = (
        pl.BlockSpec((1, ts, D), lambda b, i: (b, i, 0)),
        pl.BlockSpec((ts, ROW_W), lambda b, i: (b * n_i + i, 0)),
        pl.BlockSpec((ts, ROUTER_LANES), lambda b, i: (b * n_i + i, 0)),
        pl.BlockSpec((8, ts), lambda b, i: (0, b * n_i + i)),
        pl.BlockSpec((8, ROUTER_LANES), lambda b, i: (0, 0)),
    )
    scratch = [
        pltpu.VMEM((ts + 2 * HALO * MIXER_STREAMS, D), BF16),
        pltpu.VMEM((ts, D), F32),
        pltpu.VMEM((ts, D), F32),
        pltpu.VMEM((ts, D), BF16),
        pltpu.VMEM((ts, D), BF16),
        pltpu.VMEM((ts, D), BF16),
        pltpu.VMEM((ts, D), BF16),
        pltpu.VMEM((ts, D), BF16),
        pltpu.VMEM((8, ROUTER_LANES), F32),
    ]
    return pl.pallas_call(
        functools.partial(_mixer_kernel, nbp, ts, s_len, alpha, MIXER_STREAMS, MIXER_SKEW),
        grid=(nb, n_i),
        in_specs=in_specs,
        out_specs=out_specs,
        out_shape=out_shape,
        scratch_shapes=scratch,
        compiler_params=pltpu.CompilerParams(
            dimension_semantics=("arbitrary", "arbitrary"), vmem_limit_bytes=VMEM_LIMIT),
        name="mixer",
    )(x_p, x_s, x_p, x_s, x_p, x_s, mod, inv_count, *consts)


def _dest_kernel(seg_ref, eid_ref, rank_ref, o_ref):
    eid = eid_ref[...]
    acc = rank_ref[...].astype(jnp.int32)
    for e in range(N_EXPERTS):
        acc = acc + jnp.where(eid == float(e), seg_ref[e], 0)
    o_ref[...] = acc


def _dest(seg_start, cls, rank):
    shape = cls.shape
    return pl.pallas_call(
        _dest_kernel,
        grid_spec=pltpu.PrefetchScalarGridSpec(
            num_scalar_prefetch=1,
            grid=(1,),
            in_specs=[pl.BlockSpec(shape, lambda i, s: (0, 0)), pl.BlockSpec(shape, lambda i, s: (0, 0))],
            out_specs=pl.BlockSpec(shape, lambda i, s: (0, 0)),
        ),
        out_shape=jax.ShapeDtypeStruct(shape, jnp.int32),
        name="dest",
    )(seg_start, cls, rank)


def _sc_mesh():
    return plsc.VectorSubcoreMesh(core_axis_name="core", subcore_axis_name="subcore")


def _sc_scatter_rows(rows, dest1, dest2, n_rows, win):
    n_tok, row_w = rows.shape
    mesh = _sc_mesh()
    n_workers = mesh.num_cores * mesh.num_subcores
    n_steps = n_tok // win
    assert n_tok % win == 0 and n_steps % n_workers == 0
    per_worker = n_steps // n_workers

    @pl.kernel(out_type=jax.ShapeDtypeStruct((n_rows, row_w), rows.dtype), mesh=mesh,
               scratch_types=[pltpu.VMEM((win,), jnp.int32), pltpu.VMEM((win, row_w), rows.dtype)])
    def scatter(rows_hbm, idx1_hbm, idx2_hbm, out_hbm, idx_vmem, buf):
        worker = lax.axis_index("core") * mesh.num_subcores + lax.axis_index("subcore")

        @pl.loop(0, per_worker)
        def _(k):
            step = worker * per_worker + k
            pltpu.sync_copy(rows_hbm.at[pl.ds(step * win, win)], buf)
            for idx_hbm in (idx1_hbm, idx2_hbm):
                pltpu.sync_copy(idx_hbm.at[step], idx_vmem)
                pltpu.sync_copy(buf, out_hbm.at[idx_vmem])

    return scatter(rows, dest1.reshape(n_steps, win), dest2.reshape(n_steps, win))


def _sc_gather_rows(table, idx1, idx2, win):
    n_tok = idx1.shape[0]
    row_w = table.shape[1]
    mesh = _sc_mesh()
    n_workers = mesh.num_cores * mesh.num_subcores
    n_steps = n_tok // win
    assert n_tok % win == 0 and n_steps % n_workers == 0
    per_worker = n_steps // n_workers
    out = jax.ShapeDtypeStruct((n_tok, row_w), table.dtype)

    @pl.kernel(out_type=(out, out), mesh=mesh,
               scratch_types=[pltpu.VMEM((win,), jnp.int32), pltpu.VMEM((win, row_w), table.dtype)])
    def gather(table_hbm, idx1_hbm, idx2_hbm, out1_hbm, out2_hbm, idx_vmem, buf):
        worker = lax.axis_index("core") * mesh.num_subcores + lax.axis_index("subcore")

        @pl.loop(0, per_worker)
        def _(k):
            step = worker * per_worker + k
            for idx_hbm, out_hbm in ((idx1_hbm, out1_hbm), (idx2_hbm, out2_hbm)):
                pltpu.sync_copy(idx_hbm.at[step], idx_vmem)
                pltpu.sync_copy(table_hbm.at[idx_vmem], buf)
                pltpu.sync_copy(buf, out_hbm.at[pl.ds(step * win, win)])

    return gather(table, idx1.reshape(n_steps, win), idx2.reshape(n_steps, win))


def _expert_kernel(bm, be_ref, nused_ref, xs_ref, wg_ref, wu_ref, wd_ref, ys_ref, act_scr):
    j = pl.program_id(0)
    used = j < nused_ref[0]
    half = D // 2

    def wmat(ref, k0, k1, cols):
        return pltpu.bitcast(ref[0, k0 // 2:k1 // 2, cols], BF16)

    @pl.when(used)
    def _():
        bits = xs_ref[...]
        lo = lax.bitcast_convert_type(bits << 16, F32).astype(BF16)
        hi = lax.bitcast_convert_type(bits & jnp.uint32(0xFFFF0000), F32).astype(BF16)
        for c in range(D_EXPERT // COLB):
            cols = slice(c * COLB, (c + 1) * COLB)
            g = _dot(lo, wmat(wg_ref, 0, half, cols)) + _dot(hi, wmat(wg_ref, half, D, cols))
            up = _dot(lo, wmat(wu_ref, 0, half, cols)) + _dot(hi, wmat(wu_ref, half, D, cols))
            act_scr[:, cols] = (jax.nn.silu(g) * up).astype(BF16)

        def y_block(c0):
            return _dot(act_scr[...], wmat(wd_ref, 0, D_EXPERT, slice(c0, c0 + COLB)))

        for c0 in range(0, half, COLB):
            lo_bits = lax.bitcast_convert_type(y_block(c0).astype(BF16).astype(F32), U32)
            hi_bits = lax.bitcast_convert_type(y_block(half + c0).astype(BF16).astype(F32), U32)
            ys_ref[:, c0:c0 + COLB] = (lo_bits >> 16) | (hi_bits & jnp.uint32(0xFFFF0000))

    @pl.when(jnp.logical_not(used))
    def _():
        ys_ref[...] = jnp.zeros_like(ys_ref)


def _experts(xs, blk_expert, n_used, w_gate, w_up, w_down, bm):
    n_rows = xs.shape[0]
    n_blocks = n_rows // bm

    def x_map(j, be, nu):
        return (jnp.minimum(j, nu[0] - 1), 0)

    def w_map(j, be, nu):
        return (be[j], 0, 0)

    return pl.pallas_call(
        functools.partial(_expert_kernel, bm),
        grid_spec=pltpu.PrefetchScalarGridSpec(
            num_scalar_prefetch=2,
            grid=(n_blocks,),
            in_specs=[pl.BlockSpec((bm, ROW_W), x_map),
                      pl.BlockSpec((1, D // 2, D_EXPERT), w_map),
                      pl.BlockSpec((1, D // 2, D_EXPERT), w_map),
                      pl.BlockSpec((1, D_EXPERT // 2, D), w_map)],
            out_specs=pl.BlockSpec((bm, D // 2), lambda j, be, nu: (j, 0)),
            scratch_shapes=[pltpu.VMEM((bm, D_EXPERT), BF16)],
        ),
        out_shape=jax.ShapeDtypeStruct((n_rows, D // 2), U32),
        compiler_params=pltpu.CompilerParams(
            dimension_semantics=("arbitrary",), vmem_limit_bytes=VMEM_LIMIT),
        name="experts",
    )(blk_expert, n_used, xs, w_gate, w_up, w_down)


def _final_dense_kernel(alpha, x1_ref, f1_ref, f2_ref, gw_ref, mod_ref, l2g_ref, l2b_ref, out_ref):
    half = D // 2
    gate2 = mod_ref[0][5:6]
    g1 = gw_ref[:, 0:1]
    g2 = gw_ref[:, 1:2]
    bits1 = f1_ref[...]
    bits2 = f2_ref[...]
    f_lo = (g1 * lax.bitcast_convert_type(bits1 << 16, F32)
            + g2 * lax.bitcast_convert_type(bits2 << 16, F32))
    f_hi = (g1 * lax.bitcast_convert_type(bits1 & jnp.uint32(0xFFFF0000), F32)
            + g2 * lax.bitcast_convert_type(bits2 & jnp.uint32(0xFFFF0000), F32))
    y_lo = alpha * x1_ref[0, :, 0:half] + gate2[:, 0:half] * f_lo
    y_hi = alpha * x1_ref[0, :, half:D] + gate2[:, half:D] * f_hi
    mu = (jnp.sum(y_lo, axis=-1, keepdims=True) + jnp.sum(y_hi, axis=-1, keepdims=True)) * (1.0 / D)
    c_lo = y_lo - mu
    c_hi = y_hi - mu
    var = (jnp.sum(c_lo * c_lo, axis=-1, keepdims=True) + jnp.sum(c_hi * c_hi, axis=-1, keepdims=True)) * (1.0 / D)
    rs = lax.rsqrt(var + LN_EPS)
    out_ref[0, :, 0:half] = c_lo * rs * l2g_ref[:, 0:half] + l2b_ref[:, 0:half]
    out_ref[0, :, half:D] = c_hi * rs * l2g_ref[:, half:D] + l2b_ref[:, half:D]


def _final_dense(x1, f1, f2, gw, mod, ln2_g, ln2_b, b_off, nbg, alpha, tk):
    s_len = x1.shape[1]
    n_i = s_len // tk
    return pl.pallas_call(
        functools.partial(_final_dense_kernel, alpha),
        grid=(nbg, n_i),
        in_specs=[pl.BlockSpec((1, tk, D), lambda b, i: (b + b_off, i, 0)),
                  pl.BlockSpec((tk, D // 2), lambda b, i: (b * n_i + i, 0)),
                  pl.BlockSpec((tk, D // 2), lambda b, i: (b * n_i + i, 0)),
                  pl.BlockSpec((tk, ROUTER_LANES), lambda b, i: ((b + b_off) * n_i + i, 0)),
                  pl.BlockSpec((1, N_MOD, D), lambda b, i: (b + b_off, 0, 0)),
                  pl.BlockSpec((1, D), lambda b, i: (0, 0)),
                  pl.BlockSpec((1, D), lambda b, i: (0, 0))],
        out_specs=pl.BlockSpec((1, tk, D), lambda b, i: (b, i, 0)),
        out_shape=jax.ShapeDtypeStruct((nbg, s_len, D), F32),
        compiler_params=pltpu.CompilerParams(
            dimension_semantics=("arbitrary", "arbitrary"), vmem_limit_bytes=VMEM_LIMIT),
        name="final",
    )(x1, f1, f2, gw, mod, ln2_g, ln2_b)


def _layer(x_p, x_s, c_all, p, alpha):
    nbp, s_len, _ = x_p.shape
    nbs = x_s.shape[0]
    nb = nbp + nbs
    n_tok = nb * s_len

    mod = _modulation(c_all, p["w_mod"], p["b_mod"]).reshape(nb, N_MOD, D)

    bsf = jnp.repeat(p["b_spatial"].T, HEAD_DIM, axis=1)
    n_in = p["w_in"].shape[1]
    w_router = jnp.concatenate(
        [p["w_router_group"], p["w_router_expert"],
         jnp.zeros((D, ROUTER_LANES - N_GROUPS - N_EXPERTS), F32)], axis=1)
    b_router = jnp.concatenate(
        [p["b_router_group"], p["b_router_expert"],
         jnp.zeros((ROUTER_LANES - N_GROUPS - N_EXPERTS,), F32)]).reshape(1, ROUTER_LANES)
    wts = [
        _pack_rows(p["w_in"]), p["b_in"].reshape(1, n_in),
        p["w_spatial"].astype(BF16), bsf,
        p["sgu_g"].reshape(1, D), p["sgu_b"].reshape(1, D),
        _pack_rows(p["w_pool"]), p["pool_scale"].reshape(1, D),
        _pack_rows(p["w_branch_a"]), _pack_rows(p["w_branch_b"]), _pack_rows(p["w_out"]),
        p["ln1_g"].reshape(1, D), p["ln1_b"].reshape(1, D),
        _pack_rows(w_router), b_router,
    ]
    x1, rows, gw, er, cnt = _mixer(x_p, x_s, mod, wts, alpha, MIXER_TS)

    bm = EXPERT_BM
    n_assign = 2 * n_tok
    assert n_assign % bm == 0 and n_tok % DEST_LANES == 0
    counts = cnt[0, EXPERT_LANE0:EXPERT_LANE0 + N_EXPERTS].astype(jnp.int32)
    padded = (counts + bm - 1) // bm * bm
    seg_end = jnp.cumsum(padded).astype(jnp.int32)
    seg_start = seg_end - padded
    n_blocks = n_assign // bm + N_EXPERTS
    block_start = jnp.arange(n_blocks, dtype=jnp.int32) * bm
    blk_expert = jnp.minimum(
        jnp.sum((seg_end[None, :] <= block_start[:, None]).astype(jnp.int32), axis=1), N_EXPERTS - 1)
    n_used = (seg_end[-1:] // bm).astype(jnp.int32)

    n_lane_rows = n_tok // DEST_LANES
    eid = jnp.concatenate([er[0].reshape(n_lane_rows, DEST_LANES), er[1].reshape(n_lane_rows, DEST_LANES)])
    rank = jnp.concatenate([er[2].reshape(n_lane_rows, DEST_LANES), er[3].reshape(n_lane_rows, DEST_LANES)])
    dest = _dest(seg_start, eid, rank)
    dest1 = dest[:n_lane_rows].reshape(n_tok)
    dest2 = dest[n_lane_rows:].reshape(n_tok)

    xs = _sc_scatter_rows(rows, dest1, dest2, n_blocks * bm, SC_SCATTER_WIN)
    ys = _experts(xs, blk_expert, n_used,
                  _pack_rows(p["w_exp_gate"]), _pack_rows(p["w_exp_up"]), _pack_rows(p["w_exp_down"]), bm)

    l2g = p["ln2_g"].reshape(1, D)
    l2b = p["ln2_b"].reshape(1, D)
    tp = nbp * s_len
    f1_p, f2_p = _sc_gather_rows(ys, dest1[:tp], dest2[:tp], SC_GATHER_WIN)
    f1_s, f2_s = _sc_gather_rows(ys, dest1[tp:], dest2[tp:], SC_GATHER_WIN)
    y_p = _final_dense(x1, f1_p, f2_p, gw, mod, l2g, l2b, 0, nbp, alpha, FINAL_TK)
    y_s = _final_dense(x1, f1_s, f2_s, gw, mod, l2g, l2b, nbp, nbs, alpha, FINAL_TK)
    return y_p, y_s


def _pack_kernel(w_ref, o_ref):
    o_ref[...] = pltpu.bitcast(w_ref[...].astype(BF16), U32)


def _pack_rows(w):
    *lead, k, n = w.shape
    rows = math.prod(lead) * k
    nb = min(n, 1024)
    rb = min(rows, PACK_BLOCK_ELEMS // nb)
    assert k % 2 == 0 and rows % rb == 0 and n % nb == 0
    packed = pl.pallas_call(
        _pack_kernel,
        grid=(rows // rb, n // nb),
        in_specs=[pl.BlockSpec((rb, nb), lambda i, j: (i, j))],
        out_specs=pl.BlockSpec((rb // 2, nb), lambda i, j: (i, j)),
        out_shape=jax.ShapeDtypeStruct((rows // 2, n), U32),
        name="pack",
    )(w.reshape(rows, n))
    return packed.reshape(*lead, k // 2, n)


_PARAM_NAMES = ("w_mod", "b_mod", "w_in", "b_in", "w_spatial", "b_spatial", "sgu_g", "sgu_b", "w_pool",
                "pool_scale", "w_branch_a", "w_branch_b", "w_out", "ln1_g", "ln1_b", "w_router_group",
                "b_router_group", "w_router_expert", "b_router_expert", "w_exp_gate", "w_exp_up",
                "w_exp_down", "ln2_g", "ln2_b")


def kernel(x_prompt, x_sample, c_prompt, c_sample, w_mod, b_mod, w_in, b_in, w_spatial, b_spatial, sgu_g, sgu_b, w_pool, pool_scale, w_branch_a, w_branch_b, w_out, ln1_g, ln1_b, w_router_group, b_router_group, w_router_expert, b_router_expert, w_exp_gate, w_exp_up, w_exp_down, ln2_g, ln2_b):
    params = (w_mod, b_mod, w_in, b_in, w_spatial, b_spatial, sgu_g, sgu_b, w_pool, pool_scale,
              w_branch_a, w_branch_b, w_out, ln1_g, ln1_b, w_router_group, b_router_group,
              w_router_expert, b_router_expert, w_exp_gate, w_exp_up, w_exp_down, ln2_g, ln2_b)
    depth = w_mod.shape[0]
    alpha = (2.0 * depth) ** 0.25
    c_all = jnp.concatenate([c_prompt, c_sample], axis=0)
    y_p, y_s = x_prompt, x_sample
    for l in range(depth):
        p = {name: w[l] for name, w in zip(_PARAM_NAMES, params)}
        y_p, y_s = _layer(y_p, y_s, c_all, p, alpha)
    return (y_p, y_s)
```

```python
import functools
import math

import jax
import jax.numpy as jnp
import numpy as np
from jax import lax
from jax.experimental import pallas as pl
from jax.experimental.pallas import tpu as pltpu
from jax.experimental.pallas import tpu_sc as plsc

F32 = jnp.float32
BF16 = jnp.bfloat16
U32 = jnp.uint32

D = 1024
CHUNK = 128
SGU_HEADS = 8
HEAD_DIM = D // SGU_HEADS
POOL_WINDOWS = (2, 4, 8, 16)
POOL_GROUP_DIM = D // len(POOL_WINDOWS)
N_MOD = 6
N_GROUPS = 4
EPG = 8
N_EXPERTS = N_GROUPS * EPG
D_EXPERT = D // 2
LN_EPS = 1e-5

HALO = 16
COLB = 256
ROUTER_LANES = 128
EXPERT_LANE0 = N_GROUPS
PAIRS = EPG * (EPG - 1) // 2
N_CLASSES = N_GROUPS * PAIRS
ROW_W = D // 2 + ROUTER_LANES
MIXER_TS = 512
EXPERT_BM = 256
FINAL_TK = 1024
DEST_LANES = 512
PACK_BLOCK = 1024
SC_WIN = 128
VMEM_LIMIT = 56 * 1024 * 1024


def _ln(x):
    mu = jnp.mean(x, axis=-1, keepdims=True)
    xc = x - mu
    var = jnp.mean(xc * xc, axis=-1, keepdims=True)
    return xc * lax.rsqrt(var + LN_EPS)


_GELU_A = -2.0 * math.sqrt(2.0 / math.pi) * math.log2(math.e)
_GELU_B = _GELU_A * 0.044715


def _gelu_tanh(x):
    return x / (1.0 + jnp.exp2(x * (_GELU_A + _GELU_B * (x * x))))


def _dot(a, b):
    return jnp.dot(a, b, preferred_element_type=F32)


def _mod_kernel(c_ref, w_ref, b_ref, o_ref):
    a = jax.nn.silu(c_ref[...]).astype(BF16)
    o_ref[...] = _dot(a, w_ref[...].astype(BF16)) + b_ref[...]


def _modulation(c_all, w_mod, b_mod):
    nb = c_all.shape[0]
    n_out = w_mod.shape[1]
    cb = 512
    return pl.pallas_call(
        _mod_kernel,
        grid=(n_out // cb,),
        in_specs=[pl.BlockSpec((nb, D), lambda j: (0, 0)),
                  pl.BlockSpec((D, cb), lambda j: (0, j)),
                  pl.BlockSpec((1, cb), lambda j: (0, j))],
        out_specs=pl.BlockSpec((nb, cb), lambda j: (0, j)),
        out_shape=jax.ShapeDtypeStruct((nb, n_out), F32),
        name="mod",
    )(c_all, w_mod, b_mod.reshape(1, n_out))


def _pack_kernel(w_ref, o_ref):
    o_ref[...] = pltpu.bitcast(w_ref[...].astype(BF16), U32)


def _pack_rows(w):
    *lead, k, n = w.shape
    rows = math.prod(lead) * k
    rb = min(rows, PACK_BLOCK)
    nb = min(n, PACK_BLOCK)
    assert k % 2 == 0 and rows % rb == 0 and n % nb == 0
    packed = pl.pallas_call(
        _pack_kernel,
        grid=(rows // rb, n // nb),
        in_specs=[pl.BlockSpec((rb, nb), lambda i, j: (i, j))],
        out_specs=pl.BlockSpec((rb // 2, nb), lambda i, j: (i, j)),
        out_shape=jax.ShapeDtypeStruct((rows // 2, n), U32),
        name="pack",
    )(w.reshape(rows, n))
    return packed.reshape(*lead, k // 2, n)


def _mixer_kernel(nbp, ts, s_len, alpha,
                  xp_ref, xs_ref, xpp_ref, xsp_ref, xpn_ref, xsn_ref, mod_ref, icnt_ref,
                  win_ref, bin_ref, ws_ref, bsf_ref, sg_ref, sb_ref, wpool_ref, psc_ref,
                  wa_ref, wb_ref, wo_ref, l1g_ref, l1b_ref, wr_ref, br_ref, tri_ref,
                  x1_ref, h2u_ref, cr_ref, cnt_ref,
                  h_scr, u_scr, v_scr, vb_scr, a_scr, p_scr, m_scr, h2_scr, carry_scr):
    b = pl.program_id(0)
    i = pl.program_id(1)
    is_p = b < nbp
    ncb = D // COLB
    n_ext = ts + 2 * HALO
    half = D // 2
    main = slice(HALO, HALO + ts)

    def wmat(ref, cols=slice(None)):
        return pltpu.bitcast(ref[:, cols], BF16)

    @pl.when((b == 0) & (i == 0))
    def _():
        carry_scr[...] = jnp.zeros_like(carry_scr)

    mod = mod_ref[0]
    shift1, scale1, gate1 = mod[0:1], mod[1:2], mod[2:3]
    shift2, scale2 = mod[3:4], mod[4:5]

    def adaln1(xv):
        return (_ln(xv) * (1.0 + scale1) + shift1).astype(BF16)

    h_scr[0:HALO, :] = adaln1(jnp.where(is_p, xpp_ref[0], xsp_ref[0]))
    h_scr[main, :] = adaln1(jnp.where(is_p, xp_ref[0], xs_ref[0]))
    h_scr[HALO + ts:, :] = adaln1(jnp.where(is_p, xpn_ref[0], xsn_ref[0]))

    def proj(hrows, c0):
        return _dot(h_scr[hrows, :], wmat(win_ref, slice(c0, c0 + COLB))) + bin_ref[:, c0:c0 + COLB]

    def row_rstd(mu):
        ss = jnp.zeros((ts, 1), F32)
        for j in range(ncb):
            cols = slice(j * COLB, (j + 1) * COLB)
            xc = v_scr[:, cols] - mu
            ss = ss + jnp.sum(xc * xc, axis=-1, keepdims=True)
        return lax.rsqrt(ss * (1.0 / D) + LN_EPS)

    vsum = jnp.zeros((ts, 1), F32)
    for j in range(ncb):
        cols = slice(j * COLB, (j + 1) * COLB)
        u_scr[:, cols] = _gelu_tanh(proj(main, j * COLB))
        gv = _gelu_tanh(proj(main, D + j * COLB))
        v_scr[:, cols] = gv
        vsum = vsum + jnp.sum(gv, axis=-1, keepdims=True)
    vmu = vsum * (1.0 / D)
    vrs = row_rstd(vmu)
    for j in range(ncb):
        cols = slice(j * COLB, (j + 1) * COLB)
        vb_scr[:, cols] = ((v_scr[:, cols] - vmu) * vrs * sg_ref[:, cols] + sb_ref[:, cols]).astype(BF16)

    def pool_group(gi):
        w = POOL_WINDOWS[gi]
        cols = slice(gi * POOL_GROUP_DIM, (gi + 1) * POOL_GROUP_DIM)
        ext_pos = lax.broadcasted_iota(jnp.int32, (n_ext, 1), 0) + (i * ts - HALO)
        ext_valid = (ext_pos >= 0) & (ext_pos < s_len)
        zp = jnp.where(ext_valid, proj(slice(None), 2 * D + gi * POOL_GROUP_DIM), 0.0)
        acc = zp + pltpu.roll(zp, 1, 0)
        if w >= 4:
            acc = pltpu.roll(acc, 1, 0) + pltpu.roll(acc, n_ext - 1, 0)
        if w >= 8:
            acc = pltpu.roll(acc, 2, 0) + pltpu.roll(acc, n_ext - 2, 0)
        if w >= 16:
            acc = pltpu.roll(acc, 4, 0) + pltpu.roll(acc, n_ext - 4, 0)
        inv_count = icnt_ref[:, gi:gi + 1]
        dd = (acc[main] * inv_count - zp[main]).astype(BF16)
        p_scr[:, cols] = (_dot(dd, pltpu.bitcast(wpool_ref[gi], BF16)) * psc_ref[:, cols]).astype(BF16)

    def sgu_chunk(c):
        crow = slice(c * CHUNK, (c + 1) * CHUNK)
        for hh in range(SGU_HEADS):
            cols = slice(hh * HEAD_DIM, (hh + 1) * HEAD_DIM)
            mixed = _dot(ws_ref[hh], vb_scr[crow, cols]) + bsf_ref[:, cols]
            a_scr[crow, cols] = (u_scr[crow, cols] * mixed).astype(BF16)

    n_chunks = ts // CHUNK
    for k in range(max(n_chunks, len(POOL_WINDOWS))):
        if k < len(POOL_WINDOWS):
            pool_group(k)
        if k < n_chunks:
            sgu_chunk(k)

    for j in range(ncb):
        cols = slice(j * COLB, (j + 1) * COLB)
        ga = jax.nn.sigmoid(proj(main, 3 * D + j * COLB))
        gb = jax.nn.sigmoid(proj(main, 4 * D + j * COLB))
        ta = _dot(a_scr[...], wmat(wa_ref, cols))
        tb = _dot(p_scr[...], wmat(wb_ref, cols))
        m_scr[:, cols] = (ga * ta + gb * tb).astype(BF16)

    ysum = jnp.zeros((ts, 1), F32)
    for j in range(ncb):
        cols = slice(j * COLB, (j + 1) * COLB)
        t = _dot(m_scr[...], wmat(wo_ref, cols))
        y = alpha * jnp.where(is_p, xp_ref[0, :, cols], xs_ref[0, :, cols]) + gate1[:, cols] * t
        v_scr[:, cols] = y
        ysum = ysum + jnp.sum(y, axis=-1, keepdims=True)
    ymu = ysum * (1.0 / D)
    yrs = row_rstd(ymu)
    xsum = jnp.zeros((ts, 1), F32)
    for j in range(ncb):
        cols = slice(j * COLB, (j + 1) * COLB)
        x1 = (v_scr[:, cols] - ymu) * yrs * l1g_ref[:, cols] + l1b_ref[:, cols]
        x1_ref[0, :, cols] = x1
        v_scr[:, cols] = x1
        xsum = xsum + jnp.sum(x1, axis=-1, keepdims=True)

    xmu = xsum * (1.0 / D)
    xrs = row_rstd(xmu)
    for j in range(ncb):
        cols = slice(j * COLB, (j + 1) * COLB)
        h2 = (v_scr[:, cols] - xmu) * xrs * (1.0 + scale2[:, cols]) + shift2[:, cols]
        h2_scr[:, cols] = h2.astype(BF16)
    lo_bits = lax.bitcast_convert_type(h2_scr[:, 0:half].astype(F32), U32)
    hi_bits = lax.bitcast_convert_type(h2_scr[:, half:D].astype(F32), U32)
    h2u_ref[:, 0:half] = (lo_bits >> 16) | (hi_bits & jnp.uint32(0xFFFF0000))

    logits = _dot(h2_scr[...], wmat(wr_ref)) + br_ref[...]
    lane = lax.broadcasted_iota(jnp.int32, (ts, ROUTER_LANES), 1)
    lane_f = lane.astype(F32)
    neg = -jnp.inf
    is_g = lane < N_GROUPS
    lg = jnp.where(is_g, logits, neg)
    mg = jnp.max(lg, axis=-1, keepdims=True)
    g_sel = jnp.min(jnp.where(lg == mg, lane_f, float(ROUTER_LANES)), axis=-1, keepdims=True)
    p_sel = 1.0 / jnp.sum(jnp.where(is_g, jnp.exp(logits - mg), 0.0), axis=-1, keepdims=True)
    e_lo = EXPERT_LANE0 + g_sel * EPG
    in_grp = (lane_f >= e_lo) & (lane_f < e_lo + EPG)
    le = jnp.where(in_grp, logits, neg)
    m1 = jnp.max(le, axis=-1, keepdims=True)
    i1 = jnp.min(jnp.where(le == m1, lane_f, float(ROUTER_LANES)), axis=-1, keepdims=True)
    le2 = jnp.where(lane_f == i1, neg, le)
    m2 = jnp.max(le2, axis=-1, keepdims=True)
    i2 = jnp.min(jnp.where(le2 == m2, lane_f, float(ROUTER_LANES)), axis=-1, keepdims=True)
    e2x = jnp.exp(m2 - m1)
    den = 1.0 + e2x
    g1 = p_sel / den
    g2 = p_sel * e2x / den

    j1 = i1 - e_lo
    j2 = i2 - e_lo
    first_is_a = j1 < j2
    ja = jnp.minimum(j1, j2)
    jb = jnp.maximum(j1, j2)
    cls = g_sel * PAIRS + (EPG - 1) * ja - ja * (ja - 1.0) * 0.5 + (jb - ja - 1.0)
    w_a = jnp.where(first_is_a, g1, g2)
    w_b = jnp.where(first_is_a, g2, g1)
    extra = jnp.where(lane == 0, w_a, jnp.where(lane == 1, w_b, 0.0))
    h2u_ref[:, half:half + ROUTER_LANES] = lax.bitcast_convert_type(extra, U32)

    hit = lane_f == cls
    onehot = jnp.where(hit, 1.0, 0.0)
    pre = _dot(tri_ref[...], onehot.astype(BF16)) + carry_scr[0:1, :]
    rank = jnp.sum(jnp.where(hit, pre, 0.0), axis=-1, keepdims=True)
    carry_scr[0:1, :] = carry_scr[0:1, :] + jnp.sum(onehot, axis=0, keepdims=True)
    cnt_ref[...] = carry_scr[...]

    cr = jnp.where(lane == 0, cls, jnp.where(lane == 1, rank, 0.0))
    cr_ref[...] = cr.T[0:8, :]


def _const_spec(shape):
    nd = len(shape)
    return pl.BlockSpec(shape, lambda b, i: (0,) * nd, pipeline_mode=pl.Buffered(1))


def _mixer(x_p, x_s, mod, wts, alpha, ts):
    nbp, s_len, _ = x_p.shape
    nbs = x_s.shape[0]
    assert x_s.shape[1] == s_len and s_len % ts == 0 and ts % CHUNK == 0
    nb = nbp + nbs
    n_i = s_len // ts
    hb = ts // HALO
    n_hb = s_len // HALO

    def pick(b, on_p, val, const):
        return jnp.where(b < nbp if on_p else b >= nbp, val, const)

    def main_map(on_p):
        def f(b, i):
            bb = pick(b, on_p, b if on_p else b - nbp, nbp - 1 if on_p else 0)
            ii = pick(b, on_p, i, n_i - 1 if on_p else 0)
            return (bb, ii, 0)
        return f

    def halo_map(on_p, nxt):
        def f(b, i):
            bb = pick(b, on_p, b if on_p else b - nbp, nbp - 1 if on_p else 0)
            idx = jnp.minimum((i + 1) * hb, n_hb - 1) if nxt else jnp.maximum(i * hb - 1, 0)
            ii = pick(b, on_p, idx, n_hb - 1 if on_p else 0)
            return (bb, ii, 0)
        return f

    tri = (lax.broadcasted_iota(jnp.int32, (ts, ts), 1)
           < lax.broadcasted_iota(jnp.int32, (ts, ts), 0)).astype(BF16)
    consts = list(wts) + [tri]
    in_specs = [
        pl.BlockSpec((1, ts, D), main_map(True)),
        pl.BlockSpec((1, ts, D), main_map(False)),
        pl.BlockSpec((1, HALO, D), halo_map(True, False)),
        pl.BlockSpec((1, HALO, D), halo_map(False, False)),
        pl.BlockSpec((1, HALO, D), halo_map(True, True)),
        pl.BlockSpec((1, HALO, D), halo_map(False, True)),
        pl.BlockSpec((1, N_MOD, D), lambda b, i: (b, 0, 0)),
        pl.BlockSpec((ts, ROUTER_LANES), lambda b, i: (i, 0)),
    ] + [_const_spec(w.shape) for w in consts]

    pos = jnp.arange(s_len, dtype=jnp.int32)[:, None]
    win = jnp.asarray(POOL_WINDOWS + (1,) * (ROUTER_LANES - len(POOL_WINDOWS)), jnp.int32)[None, :]
    inv_count = 1.0 / (jnp.minimum(pos + (win - 1 - win // 2), s_len - 1)
                       - jnp.maximum(pos - win // 2, 0) + 1).astype(F32)
    n_tok = nb * s_len
    out_shape = (
        jax.ShapeDtypeStruct((nb, s_len, D), F32),
        jax.ShapeDtypeStruct((n_tok, ROW_W), U32),
        jax.ShapeDtypeStruct((8, n_tok), F32),
        jax.ShapeDtypeStruct((8, ROUTER_LANES), F32),
    )
    out_specs = (
        pl.BlockSpec((1, ts, D), lambda b, i: (b, i, 0)),
        pl.BlockSpec((ts, ROW_W), lambda b, i: (b * n_i + i, 0)),
        pl.BlockSpec((8, ts), lambda b, i: (0, b * n_i + i)),
        pl.BlockSpec((8, ROUTER_LANES), lambda b, i: (0, 0)),
    )
    scratch = [
        pltpu.VMEM((ts + 2 * HALO, D), BF16),
        pltpu.VMEM((ts, D), F32),
        pltpu.VMEM((ts, D), F32),
        pltpu.VMEM((ts, D), BF16),
        pltpu.VMEM((ts, D), BF16),
        pltpu.VMEM((ts, D), BF16),
        pltpu.VMEM((ts, D), BF16),
        pltpu.VMEM((ts, D), BF16),
        pltpu.VMEM((8, ROUTER_LANES), F32),
    ]
    return pl.pallas_call(
        functools.partial(_mixer_kernel, nbp, ts, s_len, alpha),
        grid=(nb, n_i),
        in_specs=in_specs,
        out_specs=out_specs,
        out_shape=out_shape,
        scratch_shapes=scratch,
        compiler_params=pltpu.CompilerParams(
            dimension_semantics=("arbitrary", "arbitrary"), vmem_limit_bytes=VMEM_LIMIT),
        name="mixer",
    )(x_p, x_s, x_p, x_s, x_p, x_s, mod, inv_count, *consts)


def _dest_kernel(seg_ref, cls_ref, rank_ref, o_ref):
    cls = cls_ref[...]
    acc = rank_ref[...].astype(jnp.int32)
    for c in range(N_CLASSES):
        acc = acc + jnp.where(cls == float(c), seg_ref[c], 0)
    o_ref[...] = acc


def _dest(seg_start, cls, rank):
    shape = cls.shape
    return pl.pallas_call(
        _dest_kernel,
        grid_spec=pltpu.PrefetchScalarGridSpec(
            num_scalar_prefetch=1,
            grid=(1,),
            in_specs=[pl.BlockSpec(shape, lambda i, s: (0, 0)), pl.BlockSpec(shape, lambda i, s: (0, 0))],
            out_specs=pl.BlockSpec(shape, lambda i, s: (0, 0)),
        ),
        out_shape=jax.ShapeDtypeStruct(shape, jnp.int32),
        name="dest",
    )(seg_start, cls, rank)


def _sc_mesh():
    return plsc.VectorSubcoreMesh(core_axis_name="core", subcore_axis_name="subcore")


def _sc_scatter_rows(rows, dest, n_rows, win):
    n_tok, row_w = rows.shape
    mesh = _sc_mesh()
    n_workers = mesh.num_cores * mesh.num_subcores
    n_steps = n_tok // win
    assert n_tok % win == 0 and n_steps % n_workers == 0
    per_worker = n_steps // n_workers

    @pl.kernel(out_type=jax.ShapeDtypeStruct((n_rows, row_w), rows.dtype), mesh=mesh,
               scratch_types=[pltpu.VMEM((win,), jnp.int32), pltpu.VMEM((win, row_w), rows.dtype)])
    def scatter(rows_hbm, idx_hbm, out_hbm, idx_vmem, buf):
        worker = lax.axis_index("core") * mesh.num_subcores + lax.axis_index("subcore")

        @pl.loop(0, per_worker)
        def _(k):
            step = worker * per_worker + k
            pltpu.sync_copy(idx_hbm.at[step], idx_vmem)
            pltpu.sync_copy(rows_hbm.at[pl.ds(step * win, win)], buf)
            pltpu.sync_copy(buf, out_hbm.at[idx_vmem])

    return scatter(rows, dest.reshape(n_steps, win))


def _sc_gather_rows(table, idx, win):
    n_tok = idx.shape[0]
    row_w = table.shape[1]
    mesh = _sc_mesh()
    n_workers = mesh.num_cores * mesh.num_subcores
    n_steps = n_tok // win
    assert n_tok % win == 0 and n_steps % n_workers == 0
    per_worker = n_steps // n_workers

    @pl.kernel(out_type=jax.ShapeDtypeStruct((n_tok, row_w), table.dtype), mesh=mesh,
               scratch_types=[pltpu.VMEM((win,), jnp.int32), pltpu.VMEM((win, row_w), table.dtype)])
    def gather(table_hbm, idx_hbm, out_hbm, idx_vmem, buf):
        worker = lax.axis_index("core") * mesh.num_subcores + lax.axis_index("subcore")

        @pl.loop(0, per_worker)
        def _(k):
            step = worker * per_worker + k
            pltpu.sync_copy(idx_hbm.at[step], idx_vmem)
            pltpu.sync_copy(table_hbm.at[idx_vmem], buf)
            pltpu.sync_copy(buf, out_hbm.at[pl.ds(step * win, win)])

    return gather(table, idx.reshape(n_steps, win))


def _expert_kernel(bm, ea_ref, eb_ref, nused_ref, xs_ref, wga_ref, wua_ref, wda_ref, wgb_ref, wub_ref, wdb_ref,
                   ys_ref, act_scr):
    j = pl.program_id(0)
    used = j < nused_ref[0]
    half = D // 2

    def wmat(ref, k0, k1, cols):
        return ref[0, k0:k1, cols].astype(BF16)

    @pl.when(used)
    def _():
        bits = xs_ref[:, 0:half]
        lo = lax.bitcast_convert_type(bits << 16, F32).astype(BF16)
        hi = lax.bitcast_convert_type(bits & jnp.uint32(0xFFFF0000), F32).astype(BF16)
        wts = lax.bitcast_convert_type(xs_ref[:, half:half + ROUTER_LANES], F32)
        for e, (wg_ref, wu_ref) in enumerate(((wga_ref, wua_ref), (wgb_ref, wub_ref))):
            for c in range(D_EXPERT // COLB):
                cols = slice(c * COLB, (c + 1) * COLB)
                g = _dot(lo, wmat(wg_ref, 0, half, cols)) + _dot(hi, wmat(wg_ref, half, D, cols))
                up = _dot(lo, wmat(wu_ref, 0, half, cols)) + _dot(hi, wmat(wu_ref, half, D, cols))
                act_scr[e, :, cols] = (jax.nn.silu(g) * up).astype(BF16)
        w_a = wts[:, 0:1]
        w_b = wts[:, 1:2]

        def y_block(c0):
            cols = slice(c0, c0 + COLB)
            return (w_a * _dot(act_scr[0], wmat(wda_ref, 0, D_EXPERT, cols))
                    + w_b * _dot(act_scr[1], wmat(wdb_ref, 0, D_EXPERT, cols)))

        for c0 in range(0, half, COLB):
            lo_bits = lax.bitcast_convert_type(y_block(c0).astype(BF16).astype(F32), U32)
            hi_bits = lax.bitcast_convert_type(y_block(half + c0).astype(BF16).astype(F32), U32)
            ys_ref[:, c0:c0 + COLB] = (lo_bits >> 16) | (hi_bits & jnp.uint32(0xFFFF0000))

    @pl.when(jnp.logical_not(used))
    def _():
        ys_ref[...] = jnp.zeros_like(ys_ref)


def _experts(xs, blk_a, blk_b, n_used, w_gate, w_up, w_down, bm):
    n_rows = xs.shape[0]
    n_blocks = n_rows // bm

    def x_map(j, ea, eb, nu):
        return (jnp.minimum(j, nu[0] - 1), 0)

    def a_map(j, ea, eb, nu):
        return (ea[j], 0, 0)

    def b_map(j, ea, eb, nu):
        return (eb[j], 0, 0)

    return pl.pallas_call(
        functools.partial(_expert_kernel, bm),
        grid_spec=pltpu.PrefetchScalarGridSpec(
            num_scalar_prefetch=3,
            grid=(n_blocks,),
            in_specs=[pl.BlockSpec((bm, ROW_W), x_map),
                      pl.BlockSpec((1, D, D_EXPERT), a_map),
                      pl.BlockSpec((1, D, D_EXPERT), a_map),
                      pl.BlockSpec((1, D_EXPERT, D), a_map),
                      pl.BlockSpec((1, D, D_EXPERT), b_map),
                      pl.BlockSpec((1, D, D_EXPERT), b_map),
                      pl.BlockSpec((1, D_EXPERT, D), b_map)],
            out_specs=pl.BlockSpec((bm, D // 2), lambda j, ea, eb, nu: (j, 0)),
            scratch_shapes=[pltpu.VMEM((2, bm, D_EXPERT), BF16)],
        ),
        out_shape=jax.ShapeDtypeStruct((n_rows, D // 2), U32),
        compiler_params=pltpu.CompilerParams(
            dimension_semantics=("arbitrary",), vmem_limit_bytes=VMEM_LIMIT),
        name="experts",
    )(blk_a, blk_b, n_used, xs, w_gate, w_up, w_down, w_gate, w_up, w_down)


def _final_kernel(alpha, x1_ref, f_ref, mod_ref, l2g_ref, l2b_ref, out_ref):
    half = D // 2
    gate2 = mod_ref[0][5:6]
    bits = f_ref[...]
    f_lo = lax.bitcast_convert_type(bits << 16, F32)
    f_hi = lax.bitcast_convert_type(bits & jnp.uint32(0xFFFF0000), F32)
    y_lo = alpha * x1_ref[0, :, 0:half] + gate2[:, 0:half] * f_lo
    y_hi = alpha * x1_ref[0, :, half:D] + gate2[:, half:D] * f_hi
    mu = (jnp.sum(y_lo, axis=-1, keepdims=True) + jnp.sum(y_hi, axis=-1, keepdims=True)) * (1.0 / D)
    c_lo = y_lo - mu
    c_hi = y_hi - mu
    var = (jnp.sum(c_lo * c_lo, axis=-1, keepdims=True) + jnp.sum(c_hi * c_hi, axis=-1, keepdims=True)) * (1.0 / D)
    rs = lax.rsqrt(var + LN_EPS)
    out_ref[0, :, 0:half] = c_lo * rs * l2g_ref[:, 0:half] + l2b_ref[:, 0:half]
    out_ref[0, :, half:D] = c_hi * rs * l2g_ref[:, half:D] + l2b_ref[:, half:D]


def _final(x1, f, mod, ln2_g, ln2_b, b_off, nbg, alpha, tk):
    s_len = x1.shape[1]
    n_i = s_len // tk
    return pl.pallas_call(
        functools.partial(_final_kernel, alpha),
        grid=(nbg, n_i),
        in_specs=[pl.BlockSpec((1, tk, D), lambda b, i: (b + b_off, i, 0)),
                  pl.BlockSpec((tk, D // 2), lambda b, i: (b * n_i + i, 0)),
                  pl.BlockSpec((1, N_MOD, D), lambda b, i: (b + b_off, 0, 0)),
                  pl.BlockSpec((1, D), lambda b, i: (0, 0)),
                  pl.BlockSpec((1, D), lambda b, i: (0, 0))],
        out_specs=pl.BlockSpec((1, tk, D), lambda b, i: (b, i, 0)),
        out_shape=jax.ShapeDtypeStruct((nbg, s_len, D), F32),
        compiler_params=pltpu.CompilerParams(
            dimension_semantics=("arbitrary", "arbitrary"), vmem_limit_bytes=VMEM_LIMIT),
        name="final",
    )(x1, f, mod, ln2_g, ln2_b)


def _class_experts():
    ea, eb = [], []
    for g in range(N_GROUPS):
        for a in range(EPG):
            for b in range(a + 1, EPG):
                ea.append(g * EPG + a)
                eb.append(g * EPG + b)
    return np.asarray(ea, np.int32), np.asarray(eb, np.int32)


def _layer(x_p, x_s, c_all, p, alpha):
    nbp, s_len, _ = x_p.shape
    nbs = x_s.shape[0]
    nb = nbp + nbs
    n_tok = nb * s_len

    mod = _modulation(c_all, p["w_mod"], p["b_mod"]).reshape(nb, N_MOD, D)

    bsf = jnp.repeat(p["b_spatial"].T, HEAD_DIM, axis=1)
    n_in = p["w_in"].shape[1]
    w_router = jnp.concatenate(
        [p["w_router_group"], p["w_router_expert"],
         jnp.zeros((D, ROUTER_LANES - N_GROUPS - N_EXPERTS), F32)], axis=1)
    b_router = jnp.concatenate(
        [p["b_router_group"], p["b_router_expert"],
         jnp.zeros((ROUTER_LANES - N_GROUPS - N_EXPERTS,), F32)]).reshape(1, ROUTER_LANES)
    wts = [
        _pack_rows(p["w_in"]), p["b_in"].reshape(1, n_in),
        p["w_spatial"].astype(BF16), bsf,
        p["sgu_g"].reshape(1, D), p["sgu_b"].reshape(1, D),
        _pack_rows(p["w_pool"]), p["pool_scale"].reshape(1, D),
        _pack_rows(p["w_branch_a"]), _pack_rows(p["w_branch_b"]), _pack_rows(p["w_out"]),
        p["ln1_g"].reshape(1, D), p["ln1_b"].reshape(1, D),
        _pack_rows(w_router), b_router,
    ]
    x1, rows, cr, cnt = _mixer(x_p, x_s, mod, wts, alpha, MIXER_TS)

    bm = EXPERT_BM
    assert n_tok % bm == 0 and n_tok % DEST_LANES == 0
    counts = cnt[0, :N_CLASSES].astype(jnp.int32)
    padded = (counts + bm - 1) // bm * bm
    seg_end = jnp.cumsum(padded).astype(jnp.int32)
    seg_start = seg_end - padded
    n_blocks = n_tok // bm + N_CLASSES
    block_start = jnp.arange(n_blocks, dtype=jnp.int32) * bm
    blk_cls = jnp.minimum(
        jnp.sum((seg_end[None, :] <= block_start[:, None]).astype(jnp.int32), axis=1), N_CLASSES - 1)
    cls_a, cls_b = _class_experts()
    blk_a = jnp.asarray(cls_a)[blk_cls]
    blk_b = jnp.asarray(cls_b)[blk_cls]
    n_used = (seg_end[-1:] // bm).astype(jnp.int32)

    lane_shape = (n_tok // DEST_LANES, DEST_LANES)
    dest = _dest(seg_start, cr[0].reshape(lane_shape), cr[1].reshape(lane_shape)).reshape(n_tok)

    xs = _sc_scatter_rows(rows, dest, n_blocks * bm, SC_WIN)
    ys = _experts(xs, blk_a, blk_b, n_used, p["w_exp_gate"], p["w_exp_up"], p["w_exp_down"], bm)

    l2g = p["ln2_g"].reshape(1, D)
    l2b = p["ln2_b"].reshape(1, D)
    tp = nbp * s_len
    f_p = _sc_gather_rows(ys, dest[:tp], SC_WIN)
    f_s = _sc_gather_rows(ys, dest[tp:], SC_WIN)
    y_p = _final(x1, f_p, mod, l2g, l2b, 0, nbp, alpha, FINAL_TK)
    y_s = _final(x1, f_s, mod, l2g, l2b, nbp, nbs, alpha, FINAL_TK)
    return y_p, y_s


_PARAM_NAMES = ("w_mod", "b_mod", "w_in", "b_in", "w_spatial", "b_spatial", "sgu_g", "sgu_b", "w_pool",
                "pool_scale", "w_branch_a", "w_branch_b", "w_out", "ln1_g", "ln1_b", "w_router_group",
                "b_router_group", "w_router_expert", "b_router_expert", "w_exp_gate", "w_exp_up",
                "w_exp_down", "ln2_g", "ln2_b")


def kernel(x_prompt, x_sample, c_prompt, c_sample, w_mod, b_mod, w_in, b_in, w_spatial, b_spatial, sgu_g, sgu_b, w_pool, pool_scale, w_branch_a, w_branch_b, w_out, ln1_g, ln1_b, w_router_group, b_router_group, w_router_expert, b_router_expert, w_exp_gate, w_exp_up, w_exp_down, ln2_g, ln2_b):
    params = (w_mod, b_mod, w_in, b_in, w_spatial, b_spatial, sgu_g, sgu_b, w_pool, pool_scale,
              w_branch_a, w_branch_b, w_out, ln1_g, ln1_b, w_router_group, b_router_group,
              w_router_expert, b_router_expert, w_exp_gate, w_exp_up, w_exp_down, ln2_g, ln2_b)
    depth = w_mod.shape[0]
    alpha = (2.0 * depth) ** 0.25
    c_all = jnp.concatenate([c_prompt, c_sample], axis=0)
    y_p, y_s = x_prompt, x_sample
    for l in range(depth):
        p = {name: w[l] for name, w in zip(_PARAM_NAMES, params)}
        y_p, y_s = _layer(y_p, y_s, c_all, p, alpha)
    return (y_p, y_s)
```

```python
import functools
import math

import jax
import jax.numpy as jnp
import numpy as np
from jax import lax
from jax.experimental import pallas as pl
from jax.experimental.pallas import tpu as pltpu
from jax.experimental.pallas import tpu_sc as plsc

F32 = jnp.float32
BF16 = jnp.bfloat16
U32 = jnp.uint32

D = 1024
CHUNK = 128
SGU_HEADS = 8
HEAD_DIM = D // SGU_HEADS
POOL_WINDOWS = (2, 4, 8, 16)
POOL_GROUP_DIM = D // len(POOL_WINDOWS)
N_MOD = 6
N_GROUPS = 4
EPG = 8
N_EXPERTS = N_GROUPS * EPG
D_EXPERT = D // 2
LN_EPS = 1e-5

HALO = 16
COLB = 256
ROUTER_LANES = 128
EXPERT_LANE0 = N_GROUPS
PAIRS = EPG * (EPG - 1) // 2
N_CLASSES = N_GROUPS * PAIRS
ROW_W = D // 2 + ROUTER_LANES
MIXER_TS = 512
EXPERT_BM = 256
FINAL_TK = 2048
DEST_LANES = 512
PACK_BLOCK = 1024
SC_WIN = 128
VMEM_LIMIT = 56 * 1024 * 1024


def _ln(x):
    mu = jnp.mean(x, axis=-1, keepdims=True)
    xc = x - mu
    var = jnp.mean(xc * xc, axis=-1, keepdims=True)
    return xc * lax.rsqrt(var + LN_EPS)


_GELU_A = -2.0 * math.sqrt(2.0 / math.pi) * math.log2(math.e)
_GELU_B = _GELU_A * 0.044715


def _gelu_tanh(x):
    return x / (1.0 + jnp.exp2(x * (_GELU_A + _GELU_B * (x * x))))


def _dot(a, b):
    return jnp.dot(a, b, preferred_element_type=F32)


def _mod_kernel(c_ref, w_ref, b_ref, o_ref):
    a = jax.nn.silu(c_ref[...]).astype(BF16)
    o_ref[...] = _dot(a, w_ref[...].astype(BF16)) + b_ref[...]


def _modulation(c_all, w_mod, b_mod):
    nb = c_all.shape[0]
    n_out = w_mod.shape[1]
    cb = 512
    return pl.pallas_call(
        _mod_kernel,
        grid=(n_out // cb,),
        in_specs=[pl.BlockSpec((nb, D), lambda j: (0, 0)),
                  pl.BlockSpec((D, cb), lambda j: (0, j)),
                  pl.BlockSpec((1, cb), lambda j: (0, j))],
        out_specs=pl.BlockSpec((nb, cb), lambda j: (0, j)),
        out_shape=jax.ShapeDtypeStruct((nb, n_out), F32),
        name="mod",
    )(c_all, w_mod, b_mod.reshape(1, n_out))


def _pack_kernel(w_ref, o_ref):
    o_ref[...] = pltpu.bitcast(w_ref[...].astype(BF16), U32)


def _pack_rows(w):
    *lead, k, n = w.shape
    rows = math.prod(lead) * k
    rb = min(rows, PACK_BLOCK)
    nb = min(n, PACK_BLOCK)
    assert k % 2 == 0 and rows % rb == 0 and n % nb == 0
    packed = pl.pallas_call(
        _pack_kernel,
        grid=(rows // rb, n // nb),
        in_specs=[pl.BlockSpec((rb, nb), lambda i, j: (i, j))],
        out_specs=pl.BlockSpec((rb // 2, nb), lambda i, j: (i, j)),
        out_shape=jax.ShapeDtypeStruct((rows // 2, n), U32),
        name="pack",
    )(w.reshape(rows, n))
    return packed.reshape(*lead, k // 2, n)


def _mixer_kernel(nbp, ts, s_len, alpha,
                  xp_ref, xs_ref, xpp_ref, xsp_ref, xpn_ref, xsn_ref, mod_ref, icnt_ref,
                  win_ref, bin_ref, ws_ref, bsf_ref, sg_ref, sb_ref, wpool_ref, psc_ref,
                  wa_ref, wb_ref, wo_ref, l1g_ref, l1b_ref, wr_ref, br_ref, tri_ref,
                  x1_ref, h2u_ref, cr_ref, cnt_ref,
                  h_scr, u_scr, v_scr, vb_scr, a_scr, p_scr, m_scr, h2_scr, carry_scr):
    b = pl.program_id(0)
    i = pl.program_id(1)
    is_p = b < nbp
    ncb = D // COLB
    n_ext = ts + 2 * HALO
    half = D // 2
    main = slice(HALO, HALO + ts)

    def wmat(ref, cols=slice(None)):
        return pltpu.bitcast(ref[:, cols], BF16)

    @pl.when((b == 0) & (i == 0))
    def _():
        carry_scr[...] = jnp.zeros_like(carry_scr)

    mod = mod_ref[0]
    shift1, scale1, gate1 = mod[0:1], mod[1:2], mod[2:3]
    shift2, scale2 = mod[3:4], mod[4:5]

    def adaln1(xv):
        return (_ln(xv) * (1.0 + scale1) + shift1).astype(BF16)

    h_scr[0:HALO, :] = adaln1(jnp.where(is_p, xpp_ref[0], xsp_ref[0]))
    h_scr[main, :] = adaln1(jnp.where(is_p, xp_ref[0], xs_ref[0]))
    h_scr[HALO + ts:, :] = adaln1(jnp.where(is_p, xpn_ref[0], xsn_ref[0]))

    def proj(hrows, c0):
        return _dot(h_scr[hrows, :], wmat(win_ref, slice(c0, c0 + COLB))) + bin_ref[:, c0:c0 + COLB]

    def row_rstd(mu):
        ss = jnp.zeros((ts, 1), F32)
        for j in range(ncb):
            cols = slice(j * COLB, (j + 1) * COLB)
            xc = v_scr[:, cols] - mu
            ss = ss + jnp.sum(xc * xc, axis=-1, keepdims=True)
        return lax.rsqrt(ss * (1.0 / D) + LN_EPS)

    vsum = jnp.zeros((ts, 1), F32)
    for j in range(ncb):
        cols = slice(j * COLB, (j + 1) * COLB)
        u_scr[:, cols] = _gelu_tanh(proj(main, j * COLB))
        gv = _gelu_tanh(proj(main, D + j * COLB))
        v_scr[:, cols] = gv
        vsum = vsum + jnp.sum(gv, axis=-1, keepdims=True)
    vmu = vsum * (1.0 / D)
    vrs = row_rstd(vmu)
    for j in range(ncb):
        cols = slice(j * COLB, (j + 1) * COLB)
        vb_scr[:, cols] = ((v_scr[:, cols] - vmu) * vrs * sg_ref[:, cols] + sb_ref[:, cols]).astype(BF16)

    def pool_group(gi):
        w = POOL_WINDOWS[gi]
        cols = slice(gi * POOL_GROUP_DIM, (gi + 1) * POOL_GROUP_DIM)
        ext_pos = lax.broadcasted_iota(jnp.int32, (n_ext, 1), 0) + (i * ts - HALO)
        ext_valid = (ext_pos >= 0) & (ext_pos < s_len)
        zp = jnp.where(ext_valid, proj(slice(None), 2 * D + gi * POOL_GROUP_DIM), 0.0)
        acc = zp + pltpu.roll(zp, 1, 0)
        if w >= 4:
            acc = pltpu.roll(acc, 1, 0) + pltpu.roll(acc, n_ext - 1, 0)
        if w >= 8:
            acc = pltpu.roll(acc, 2, 0) + pltpu.roll(acc, n_ext - 2, 0)
        if w >= 16:
            acc = pltpu.roll(acc, 4, 0) + pltpu.roll(acc, n_ext - 4, 0)
        inv_count = icnt_ref[:, gi:gi + 1]
        dd = (acc[main] * inv_count - zp[main]).astype(BF16)
        p_scr[:, cols] = (_dot(dd, pltpu.bitcast(wpool_ref[gi], BF16)) * psc_ref[:, cols]).astype(BF16)

    def sgu_chunk(c):
        crow = slice(c * CHUNK, (c + 1) * CHUNK)
        for hh in range(SGU_HEADS):
            cols = slice(hh * HEAD_DIM, (hh + 1) * HEAD_DIM)
            mixed = _dot(ws_ref[hh], vb_scr[crow, cols]) + bsf_ref[:, cols]
            a_scr[crow, cols] = (u_scr[crow, cols] * mixed).astype(BF16)

    n_chunks = ts // CHUNK
    for k in range(max(n_chunks, len(POOL_WINDOWS))):
        if k < len(POOL_WINDOWS):
            pool_group(k)
        if k < n_chunks:
            sgu_chunk(k)

    for j in range(ncb):
        cols = slice(j * COLB, (j + 1) * COLB)
        ga = jax.nn.sigmoid(proj(main, 3 * D + j * COLB))
        gb = jax.nn.sigmoid(proj(main, 4 * D + j * COLB))
        ta = _dot(a_scr[...], wmat(wa_ref, cols))
        tb = _dot(p_scr[...], wmat(wb_ref, cols))
        m_scr[:, cols] = (ga * ta + gb * tb).astype(BF16)

    ysum = jnp.zeros((ts, 1), F32)
    for j in range(ncb):
        cols = slice(j * COLB, (j + 1) * COLB)
        t = _dot(m_scr[...], wmat(wo_ref, cols))
        y = alpha * jnp.where(is_p, xp_ref[0, :, cols], xs_ref[0, :, cols]) + gate1[:, cols] * t
        v_scr[:, cols] = y
        ysum = ysum + jnp.sum(y, axis=-1, keepdims=True)
    ymu = ysum * (1.0 / D)
    yrs = row_rstd(ymu)
    xsum = jnp.zeros((ts, 1), F32)
    for j in range(ncb):
        cols = slice(j * COLB, (j + 1) * COLB)
        x1 = (v_scr[:, cols] - ymu) * yrs * l1g_ref[:, cols] + l1b_ref[:, cols]
        x1_ref[0, :, cols] = x1
        v_scr[:, cols] = x1
        xsum = xsum + jnp.sum(x1, axis=-1, keepdims=True)

    xmu = xsum * (1.0 / D)
    xrs = row_rstd(xmu)
    for j in range(ncb):
        cols = slice(j * COLB, (j + 1) * COLB)
        h2 = (v_scr[:, cols] - xmu) * xrs * (1.0 + scale2[:, cols]) + shift2[:, cols]
        h2_scr[:, cols] = h2.astype(BF16)
    lo_bits = lax.bitcast_convert_type(h2_scr[:, 0:half].astype(F32), U32)
    hi_bits = lax.bitcast_convert_type(h2_scr[:, half:D].astype(F32), U32)
    h2u_ref[:, 0:half] = (lo_bits >> 16) | (hi_bits & jnp.uint32(0xFFFF0000))

    logits = _dot(h2_scr[...], wmat(wr_ref)) + br_ref[...]
    lane = lax.broadcasted_iota(jnp.int32, (ts, ROUTER_LANES), 1)
    lane_f = lane.astype(F32)
    neg = -jnp.inf
    is_g = lane < N_GROUPS
    lg = jnp.where(is_g, logits, neg)
    mg = jnp.max(lg, axis=-1, keepdims=True)
    g_sel = jnp.min(jnp.where(lg == mg, lane_f, float(ROUTER_LANES)), axis=-1, keepdims=True)
    p_sel = 1.0 / jnp.sum(jnp.where(is_g, jnp.exp(logits - mg), 0.0), axis=-1, keepdims=True)
    e_lo = EXPERT_LANE0 + g_sel * EPG
    in_grp = (lane_f >= e_lo) & (lane_f < e_lo + EPG)
    le = jnp.where(in_grp, logits, neg)
    m1 = jnp.max(le, axis=-1, keepdims=True)
    i1 = jnp.min(jnp.where(le == m1, lane_f, float(ROUTER_LANES)), axis=-1, keepdims=True)
    le2 = jnp.where(lane_f == i1, neg, le)
    m2 = jnp.max(le2, axis=-1, keepdims=True)
    i2 = jnp.min(jnp.where(le2 == m2, lane_f, float(ROUTER_LANES)), axis=-1, keepdims=True)
    e2x = jnp.exp(m2 - m1)
    den = 1.0 + e2x
    g1 = p_sel / den
    g2 = p_sel * e2x / den

    j1 = i1 - e_lo
    j2 = i2 - e_lo
    first_is_a = j1 < j2
    ja = jnp.minimum(j1, j2)
    jb = jnp.maximum(j1, j2)
    cls = g_sel * PAIRS + (EPG - 1) * ja - ja * (ja - 1.0) * 0.5 + (jb - ja - 1.0)
    w_a = jnp.where(first_is_a, g1, g2)
    w_b = jnp.where(first_is_a, g2, g1)
    extra = jnp.where(lane == 0, w_a, jnp.where(lane == 1, w_b, 0.0))
    h2u_ref[:, half:half + ROUTER_LANES] = lax.bitcast_convert_type(extra, U32)

    hit = lane_f == cls
    onehot = jnp.where(hit, 1.0, 0.0)
    pre = _dot(tri_ref[...], onehot.astype(BF16)) + carry_scr[0:1, :]
    rank = jnp.sum(jnp.where(hit, pre, 0.0), axis=-1, keepdims=True)
    carry_scr[0:1, :] = carry_scr[0:1, :] + jnp.sum(onehot, axis=0, keepdims=True)
    cnt_ref[...] = carry_scr[...]

    cr = jnp.where(lane == 0, cls, jnp.where(lane == 1, rank, 0.0))
    cr_ref[...] = cr.T[0:8, :]


def _const_spec(shape):
    nd = len(shape)
    return pl.BlockSpec(shape, lambda b, i: (0,) * nd, pipeline_mode=pl.Buffered(1))


def _mixer(x_p, x_s, mod, wts, alpha, ts):
    nbp, s_len, _ = x_p.shape
    nbs = x_s.shape[0]
    assert x_s.shape[1] == s_len and s_len % ts == 0 and ts % CHUNK == 0
    nb = nbp + nbs
    n_i = s_len // ts
    hb = ts // HALO
    n_hb = s_len // HALO

    def pick(b, on_p, val, const):
        return jnp.where(b < nbp if on_p else b >= nbp, val, const)

    def main_map(on_p):
        def f(b, i):
            bb = pick(b, on_p, b if on_p else b - nbp, nbp - 1 if on_p else 0)
            ii = pick(b, on_p, i, n_i - 1 if on_p else 0)
            return (bb, ii, 0)
        return f

    def halo_map(on_p, nxt):
        def f(b, i):
            bb = pick(b, on_p, b if on_p else b - nbp, nbp - 1 if on_p else 0)
            idx = jnp.minimum((i + 1) * hb, n_hb - 1) if nxt else jnp.maximum(i * hb - 1, 0)
            ii = pick(b, on_p, idx, n_hb - 1 if on_p else 0)
            return (bb, ii, 0)
        return f

    tri = (lax.broadcasted_iota(jnp.int32, (ts, ts), 1)
           < lax.broadcasted_iota(jnp.int32, (ts, ts), 0)).astype(BF16)
    consts = list(wts) + [tri]
    in_specs = [
        pl.BlockSpec((1, ts, D), main_map(True)),
        pl.BlockSpec((1, ts, D), main_map(False)),
        pl.BlockSpec((1, HALO, D), halo_map(True, False)),
        pl.BlockSpec((1, HALO, D), halo_map(False, False)),
        pl.BlockSpec((1, HALO, D), halo_map(True, True)),
        pl.BlockSpec((1, HALO, D), halo_map(False, True)),
        pl.BlockSpec((1, N_MOD, D), lambda b, i: (b, 0, 0)),
        pl.BlockSpec((ts, ROUTER_LANES), lambda b, i: (i, 0)),
    ] + [_const_spec(w.shape) for w in consts]

    pos = jnp.arange(s_len, dtype=jnp.int32)[:, None]
    win = jnp.asarray(POOL_WINDOWS + (1,) * (ROUTER_LANES - len(POOL_WINDOWS)), jnp.int32)[None, :]
    inv_count = 1.0 / (jnp.minimum(pos + (win - 1 - win // 2), s_len - 1)
                       - jnp.maximum(pos - win // 2, 0) + 1).astype(F32)
    n_tok = nb * s_len
    out_shape = (
        jax.ShapeDtypeStruct((nb, s_len, D), F32),
        jax.ShapeDtypeStruct((n_tok, ROW_W), U32),
        jax.ShapeDtypeStruct((8, n_tok), F32),
        jax.ShapeDtypeStruct((8, ROUTER_LANES), F32),
    )
    out_specs = (
        pl.BlockSpec((1, ts, D), lambda b, i: (b, i, 0)),
        pl.BlockSpec((ts, ROW_W), lambda b, i: (b * n_i + i, 0)),
        pl.BlockSpec((8, ts), lambda b, i: (0, b * n_i + i)),
        pl.BlockSpec((8, ROUTER_LANES), lambda b, i: (0, 0)),
    )
    scratch = [
        pltpu.VMEM((ts + 2 * HALO, D), BF16),
        pltpu.VMEM((ts, D), F32),
        pltpu.VMEM((ts, D), F32),
        pltpu.VMEM((ts, D), BF16),
        pltpu.VMEM((ts, D), BF16),
        pltpu.VMEM((ts, D), BF16),
        pltpu.VMEM((ts, D), BF16),
        pltpu.VMEM((ts, D), BF16),
        pltpu.VMEM((8, ROUTER_LANES), F32),
    ]
    return pl.pallas_call(
        functools.partial(_mixer_kernel, nbp, ts, s_len, alpha),
        grid=(nb, n_i),
        in_specs=in_specs,
        out_specs=out_specs,
        out_shape=out_shape,
        scratch_shapes=scratch,
        compiler_params=pltpu.CompilerParams(
            dimension_semantics=("arbitrary", "arbitrary"), vmem_limit_bytes=VMEM_LIMIT),
        name="mixer",
    )(x_p, x_s, x_p, x_s, x_p, x_s, mod, inv_count, *consts)


def _dest_kernel(seg_ref, cls_ref, rank_ref, o_ref):
    cls = cls_ref[...]
    acc = rank_ref[...].astype(jnp.int32)
    for c in range(N_CLASSES):
        acc = acc + jnp.where(cls == float(c), seg_ref[c], 0)
    o_ref[...] = acc


def _dest(seg_start, cls, rank):
    shape = cls.shape
    return pl.pallas_call(
        _dest_kernel,
        grid_spec=pltpu.PrefetchScalarGridSpec(
            num_scalar_prefetch=1,
            grid=(1,),
            in_specs=[pl.BlockSpec(shape, lambda i, s: (0, 0)), pl.BlockSpec(shape, lambda i, s: (0, 0))],
            out_specs=pl.BlockSpec(shape, lambda i, s: (0, 0)),
        ),
        out_shape=jax.ShapeDtypeStruct(shape, jnp.int32),
        name="dest",
    )(seg_start, cls, rank)


def _sc_mesh():
    return plsc.VectorSubcoreMesh(core_axis_name="core", subcore_axis_name="subcore")


def _sc_scatter_rows(rows, dest, n_rows, win):
    n_tok, row_w = rows.shape
    mesh = _sc_mesh()
    n_workers = mesh.num_cores * mesh.num_subcores
    n_steps = n_tok // win
    assert n_tok % win == 0 and n_steps % n_workers == 0
    per_worker = n_steps // n_workers

    @pl.kernel(out_type=jax.ShapeDtypeStruct((n_rows, row_w), rows.dtype), mesh=mesh,
               scratch_types=[pltpu.VMEM((win,), jnp.int32), pltpu.VMEM((win, row_w), rows.dtype)])
    def scatter(rows_hbm, idx_hbm, out_hbm, idx_vmem, buf):
        worker = lax.axis_index("core") * mesh.num_subcores + lax.axis_index("subcore")

        @pl.loop(0, per_worker)
        def _(k):
            step = worker * per_worker + k
            pltpu.sync_copy(idx_hbm.at[step], idx_vmem)
            pltpu.sync_copy(rows_hbm.at[pl.ds(step * win, win)], buf)
            pltpu.sync_copy(buf, out_hbm.at[idx_vmem])

    return scatter(rows, dest.reshape(n_steps, win))


def _sc_gather_rows(table, idx, win):
    n_tok = idx.shape[0]
    row_w = table.shape[1]
    mesh = _sc_mesh()
    n_workers = mesh.num_cores * mesh.num_subcores
    n_steps = n_tok // win
    assert n_tok % win == 0 and n_steps % n_workers == 0
    per_worker = n_steps // n_workers

    @pl.kernel(out_type=jax.ShapeDtypeStruct((n_tok, row_w), table.dtype), mesh=mesh,
               scratch_types=[pltpu.VMEM((win,), jnp.int32), pltpu.VMEM((win, row_w), table.dtype)])
    def gather(table_hbm, idx_hbm, out_hbm, idx_vmem, buf):
        worker = lax.axis_index("core") * mesh.num_subcores + lax.axis_index("subcore")

        @pl.loop(0, per_worker)
        def _(k):
            step = worker * per_worker + k
            pltpu.sync_copy(idx_hbm.at[step], idx_vmem)
            pltpu.sync_copy(table_hbm.at[idx_vmem], buf)
            pltpu.sync_copy(buf, out_hbm.at[pl.ds(step * win, win)])

    return gather(table, idx.reshape(n_steps, win))


def _expert_kernel(bm, ea_ref, eb_ref, nused_ref, xs_ref, wga_ref, wua_ref, wda_ref, wgb_ref, wub_ref, wdb_ref,
                   ys_ref, act_scr):
    j = pl.program_id(0)
    used = j < nused_ref[0]
    half = D // 2

    def wmat(ref, k0, k1, cols):
        return pltpu.bitcast(ref[0, k0 // 2:k1 // 2, cols], BF16)

    @pl.when(used)
    def _():
        bits = xs_ref[:, 0:half]
        lo = lax.bitcast_convert_type(bits << 16, F32).astype(BF16)
        hi = lax.bitcast_convert_type(bits & jnp.uint32(0xFFFF0000), F32).astype(BF16)
        wts = lax.bitcast_convert_type(xs_ref[:, half:half + ROUTER_LANES], F32)
        for e, (wg_ref, wu_ref) in enumerate(((wga_ref, wua_ref), (wgb_ref, wub_ref))):
            for c in range(D_EXPERT // COLB):
                cols = slice(c * COLB, (c + 1) * COLB)
                g = _dot(lo, wmat(wg_ref, 0, half, cols)) + _dot(hi, wmat(wg_ref, half, D, cols))
                up = _dot(lo, wmat(wu_ref, 0, half, cols)) + _dot(hi, wmat(wu_ref, half, D, cols))
                act_scr[e, :, cols] = (jax.nn.silu(g) * up).astype(BF16)
        w_a = wts[:, 0:1]
        w_b = wts[:, 1:2]

        def y_block(c0):
            cols = slice(c0, c0 + COLB)
            return (w_a * _dot(act_scr[0], wmat(wda_ref, 0, D_EXPERT, cols))
                    + w_b * _dot(act_scr[1], wmat(wdb_ref, 0, D_EXPERT, cols)))

        for c0 in range(0, half, COLB):
            lo_bits = lax.bitcast_convert_type(y_block(c0).astype(BF16).astype(F32), U32)
            hi_bits = lax.bitcast_convert_type(y_block(half + c0).astype(BF16).astype(F32), U32)
            ys_ref[:, c0:c0 + COLB] = (lo_bits >> 16) | (hi_bits & jnp.uint32(0xFFFF0000))


def _experts(xs, blk_a, blk_b, n_used, w_gate, w_up, w_down, bm):
    n_rows = xs.shape[0]
    n_blocks = n_rows // bm

    def x_map(j, ea, eb, nu):
        return (jnp.minimum(j, nu[0] - 1), 0)

    def a_map(j, ea, eb, nu):
        return (ea[j], 0, 0)

    def b_map(j, ea, eb, nu):
        return (eb[j], 0, 0)

    return pl.pallas_call(
        functools.partial(_expert_kernel, bm),
        grid_spec=pltpu.PrefetchScalarGridSpec(
            num_scalar_prefetch=3,
            grid=(n_blocks,),
            in_specs=[pl.BlockSpec((bm, ROW_W), x_map),
                      pl.BlockSpec((1, D // 2, D_EXPERT), a_map),
                      pl.BlockSpec((1, D // 2, D_EXPERT), a_map),
                      pl.BlockSpec((1, D_EXPERT // 2, D), a_map),
                      pl.BlockSpec((1, D // 2, D_EXPERT), b_map),
                      pl.BlockSpec((1, D // 2, D_EXPERT), b_map),
                      pl.BlockSpec((1, D_EXPERT // 2, D), b_map)],
            out_specs=pl.BlockSpec((bm, D // 2), x_map),
            scratch_shapes=[pltpu.VMEM((2, bm, D_EXPERT), BF16)],
        ),
        out_shape=jax.ShapeDtypeStruct((n_rows, D // 2), U32),
        compiler_params=pltpu.CompilerParams(
            dimension_semantics=("arbitrary",), vmem_limit_bytes=VMEM_LIMIT),
        name="experts",
    )(blk_a, blk_b, n_used, xs, w_gate, w_up, w_down, w_gate, w_up, w_down)


def _final_kernel(alpha, x1_ref, f_ref, mod_ref, l2g_ref, l2b_ref, out_ref):
    half = D // 2
    gate2 = mod_ref[0][5:6]
    bits = f_ref[...]
    f_lo = lax.bitcast_convert_type(bits << 16, F32)
    f_hi = lax.bitcast_convert_type(bits & jnp.uint32(0xFFFF0000), F32)
    y_lo = alpha * x1_ref[0, :, 0:half] + gate2[:, 0:half] * f_lo
    y_hi = alpha * x1_ref[0, :, half:D] + gate2[:, half:D] * f_hi
    mu = (jnp.sum(y_lo, axis=-1, keepdims=True) + jnp.sum(y_hi, axis=-1, keepdims=True)) * (1.0 / D)
    c_lo = y_lo - mu
    c_hi = y_hi - mu
    var = (jnp.sum(c_lo * c_lo, axis=-1, keepdims=True) + jnp.sum(c_hi * c_hi, axis=-1, keepdims=True)) * (1.0 / D)
    rs = lax.rsqrt(var + LN_EPS)
    out_ref[0, :, 0:half] = c_lo * rs * l2g_ref[:, 0:half] + l2b_ref[:, 0:half]
    out_ref[0, :, half:D] = c_hi * rs * l2g_ref[:, half:D] + l2b_ref[:, half:D]


def _final(x1, f, mod, ln2_g, ln2_b, b_off, nbg, alpha, tk):
    s_len = x1.shape[1]
    n_i = s_len // tk
    return pl.pallas_call(
        functools.partial(_final_kernel, alpha),
        grid=(nbg, n_i),
        in_specs=[pl.BlockSpec((1, tk, D), lambda b, i: (b + b_off, i, 0)),
                  pl.BlockSpec((tk, D // 2), lambda b, i: (b * n_i + i, 0)),
                  pl.BlockSpec((1, N_MOD, D), lambda b, i: (b + b_off, 0, 0)),
                  pl.BlockSpec((1, D), lambda b, i: (0, 0)),
                  pl.BlockSpec((1, D), lambda b, i: (0, 0))],
        out_specs=pl.BlockSpec((1, tk, D), lambda b, i: (b, i, 0)),
        out_shape=jax.ShapeDtypeStruct((nbg, s_len, D), F32),
        compiler_params=pltpu.CompilerParams(
            dimension_semantics=("arbitrary", "arbitrary"), vmem_limit_bytes=VMEM_LIMIT),
        name="final",
    )(x1, f, mod, ln2_g, ln2_b)


def _class_experts():
    ea, eb = [], []
    for g in range(N_GROUPS):
        for a in range(EPG):
            for b in range(a + 1, EPG):
                ea.append(g * EPG + a)
                eb.append(g * EPG + b)
    return np.asarray(ea, np.int32), np.asarray(eb, np.int32)


def _layer(x_p, x_s, c_all, p, alpha):
    nbp, s_len, _ = x_p.shape
    nbs = x_s.shape[0]
    nb = nbp + nbs
    n_tok = nb * s_len

    mod = _modulation(c_all, p["w_mod"], p["b_mod"]).reshape(nb, N_MOD, D)

    bsf = jnp.repeat(p["b_spatial"].T, HEAD_DIM, axis=1)
    n_in = p["w_in"].shape[1]
    w_router = jnp.concatenate(
        [p["w_router_group"], p["w_router_expert"],
         jnp.zeros((D, ROUTER_LANES - N_GROUPS - N_EXPERTS), F32)], axis=1)
    b_router = jnp.concatenate(
        [p["b_router_group"], p["b_router_expert"],
         jnp.zeros((ROUTER_LANES - N_GROUPS - N_EXPERTS,), F32)]).reshape(1, ROUTER_LANES)
    wts = [
        _pack_rows(p["w_in"]), p["b_in"].reshape(1, n_in),
        p["w_spatial"].astype(BF16), bsf,
        p["sgu_g"].reshape(1, D), p["sgu_b"].reshape(1, D),
        _pack_rows(p["w_pool"]), p["pool_scale"].reshape(1, D),
        _pack_rows(p["w_branch_a"]), _pack_rows(p["w_branch_b"]), _pack_rows(p["w_out"]),
        p["ln1_g"].reshape(1, D), p["ln1_b"].reshape(1, D),
        _pack_rows(w_router), b_router,
    ]
    x1, rows, cr, cnt = _mixer(x_p, x_s, mod, wts, alpha, MIXER_TS)

    bm = EXPERT_BM
    assert n_tok % bm == 0 and n_tok % DEST_LANES == 0
    counts = cnt[0, :N_CLASSES].astype(jnp.int32)
    padded = (counts + bm - 1) // bm * bm
    seg_end = jnp.cumsum(padded).astype(jnp.int32)
    seg_start = seg_end - padded
    n_blocks = n_tok // bm + N_CLASSES
    block_start = jnp.arange(n_blocks, dtype=jnp.int32) * bm
    blk_cls = jnp.minimum(
        jnp.sum((seg_end[None, :] <= block_start[:, None]).astype(jnp.int32), axis=1), N_CLASSES - 1)
    cls_a, cls_b = _class_experts()
    blk_a = jnp.asarray(cls_a)[blk_cls]
    blk_b = jnp.asarray(cls_b)[blk_cls]
    n_used = (seg_end[-1:] // bm).astype(jnp.int32)

    lane_shape = (n_tok // DEST_LANES, DEST_LANES)
    dest = _dest(seg_start, cr[0].reshape(lane_shape), cr[1].reshape(lane_shape)).reshape(n_tok)

    xs = _sc_scatter_rows(rows, dest, n_blocks * bm, SC_WIN)
    ys = _experts(xs, blk_a, blk_b, n_used,
                  _pack_rows(p["w_exp_gate"]), _pack_rows(p["w_exp_up"]), _pack_rows(p["w_exp_down"]), bm)

    l2g = p["ln2_g"].reshape(1, D)
    l2b = p["ln2_b"].reshape(1, D)
    tp = nbp * s_len
    f_p = _sc_gather_rows(ys, dest[:tp], SC_WIN)
    f_s = _sc_gather_rows(ys, dest[tp:], SC_WIN)
    y_p = _final(x1, f_p, mod, l2g, l2b, 0, nbp, alpha, FINAL_TK)
    y_s = _final(x1, f_s, mod, l2g, l2b, nbp, nbs, alpha, FINAL_TK)
    return y_p, y_s


_PARAM_NAMES = ("w_mod", "b_mod", "w_in", "b_in", "w_spatial", "b_spatial", "sgu_g", "sgu_b", "w_pool",
                "pool_scale", "w_branch_a", "w_branch_b", "w_out", "ln1_g", "ln1_b", "w_router_group",
                "b_router_group", "w_router_expert", "b_router_expert", "w_exp_gate", "w_exp_up",
                "w_exp_down", "ln2_g", "ln2_b")


def kernel(x_prompt, x_sample, c_prompt, c_sample, w_mod, b_mod, w_in, b_in, w_spatial, b_spatial, sgu_g, sgu_b, w_pool, pool_scale, w_branch_a, w_branch_b, w_out, ln1_g, ln1_b, w_router_group, b_router_group, w_router_expert, b_router_expert, w_exp_gate, w_exp_up, w_exp_down, ln2_g, ln2_b):
    params = (w_mod, b_mod, w_in, b_in, w_spatial, b_spatial, sgu_g, sgu_b, w_pool, pool_scale,
              w_branch_a, w_branch_b, w_out, ln1_g, ln1_b, w_router_group, b_router_group,
              w_router_expert, b_router_expert, w_exp_gate, w_exp_up, w_exp_down, ln2_g, ln2_b)
    depth = w_mod.shape[0]
    alpha = (2.0 * depth) ** 0.25
    c_all = jnp.concatenate([c_prompt, c_sample], axis=0)
    y_p, y_s = x_prompt, x_sample
    for l in range(depth):
        p = {name: w[l] for name, w in zip(_PARAM_NAMES, params)}
        y_p, y_s = _layer(y_p, y_s, c_all, p, alpha)
    return (y_p, y_s)
```

```python
import functools
import math

import jax
import jax.numpy as jnp
import numpy as np
from jax import lax
from jax.experimental import pallas as pl
from jax.experimental.pallas import tpu as pltpu
from jax.experimental.pallas import tpu_sc as plsc

F32 = jnp.float32
BF16 = jnp.bfloat16
U32 = jnp.uint32

D = 1024
CHUNK = 128
SGU_HEADS = 8
HEAD_DIM = D // SGU_HEADS
POOL_WINDOWS = (2, 4, 8, 16)
POOL_GROUP_DIM = D // len(POOL_WINDOWS)
N_MOD = 6
N_GROUPS = 4
EPG = 8
N_EXPERTS = N_GROUPS * EPG
D_EXPERT = D // 2
LN_EPS = 1e-5

HALO = 16
COLB = 256
ROUTER_LANES = 128
EXPERT_LANE0 = N_GROUPS
PAIRS = EPG * (EPG - 1) // 2
N_CLASSES = N_GROUPS * PAIRS
ROW_W = D // 2 + ROUTER_LANES
MIXER_TS = 512
EXPERT_BM = 256
FINAL_TK = 2048
DEST_LANES = 512
PACK_BLOCK = 1024
SC_WIN = 128
SC_LANES = 16
SC_PACK_CHUNK = 32
SC_PACK_UNROLL = 8
VMEM_LIMIT = 56 * 1024 * 1024


def _ln(x):
    mu = jnp.mean(x, axis=-1, keepdims=True)
    xc = x - mu
    var = jnp.mean(xc * xc, axis=-1, keepdims=True)
    return xc * lax.rsqrt(var + LN_EPS)


_GELU_A = -2.0 * math.sqrt(2.0 / math.pi) * math.log2(math.e)
_GELU_B = _GELU_A * 0.044715


def _gelu_tanh(x):
    return x / (1.0 + jnp.exp2(x * (_GELU_A + _GELU_B * (x * x))))


def _dot(a, b):
    return jnp.dot(a, b, preferred_element_type=F32)


def _mod_kernel(c_ref, w_ref, b_ref, o_ref):
    a = jax.nn.silu(c_ref[...]).astype(BF16)
    o_ref[...] = _dot(a, w_ref[...].astype(BF16)) + b_ref[...]


def _modulation(c_all, w_mod, b_mod):
    nb = c_all.shape[0]
    n_out = w_mod.shape[1]
    cb = 512
    return pl.pallas_call(
        _mod_kernel,
        grid=(n_out // cb,),
        in_specs=[pl.BlockSpec((nb, D), lambda j: (0, 0)),
                  pl.BlockSpec((D, cb), lambda j: (0, j)),
                  pl.BlockSpec((1, cb), lambda j: (0, j))],
        out_specs=pl.BlockSpec((nb, cb), lambda j: (0, j)),
        out_shape=jax.ShapeDtypeStruct((nb, n_out), F32),
        name="mod",
    )(c_all, w_mod, b_mod.reshape(1, n_out))


def _pack_kernel(w_ref, o_ref):
    o_ref[...] = pltpu.bitcast(w_ref[...].astype(BF16), U32)


def _pack_rows(w):
    *lead, k, n = w.shape
    rows = math.prod(lead) * k
    rb = min(rows, PACK_BLOCK)
    nb = min(n, PACK_BLOCK)
    assert k % 2 == 0 and rows % rb == 0 and n % nb == 0
    packed = pl.pallas_call(
        _pack_kernel,
        grid=(rows // rb, n // nb),
        in_specs=[pl.BlockSpec((rb, nb), lambda i, j: (i, j))],
        out_specs=pl.BlockSpec((rb // 2, nb), lambda i, j: (i, j)),
        out_shape=jax.ShapeDtypeStruct((rows // 2, n), U32),
        name="pack",
    )(w.reshape(rows, n))
    return packed.reshape(*lead, k // 2, n)


def _mixer_kernel(nbp, ts, s_len, alpha,
                  xp_ref, xs_ref, xpp_ref, xsp_ref, xpn_ref, xsn_ref, mod_ref, icnt_ref,
                  win_ref, bin_ref, ws_ref, bsf_ref, sg_ref, sb_ref, wpool_ref, psc_ref,
                  wa_ref, wb_ref, wo_ref, l1g_ref, l1b_ref, wr_ref, br_ref, tri_ref,
                  x1_ref, h2u_ref, cr_ref, cnt_ref,
                  h_scr, u_scr, v_scr, vb_scr, a_scr, p_scr, m_scr, h2_scr, carry_scr):
    b = pl.program_id(0)
    i = pl.program_id(1)
    is_p = b < nbp
    ncb = D // COLB
    n_ext = ts + 2 * HALO
    half = D // 2
    main = slice(HALO, HALO + ts)

    def wmat(ref, cols=slice(None)):
        return pltpu.bitcast(ref[:, cols], BF16)

    @pl.when((b == 0) & (i == 0))
    def _():
        carry_scr[...] = jnp.zeros_like(carry_scr)

    mod = mod_ref[0]
    shift1, scale1, gate1 = mod[0:1], mod[1:2], mod[2:3]
    shift2, scale2 = mod[3:4], mod[4:5]

    def adaln1(xv):
        return (_ln(xv) * (1.0 + scale1) + shift1).astype(BF16)

    h_scr[0:HALO, :] = adaln1(jnp.where(is_p, xpp_ref[0], xsp_ref[0]))
    h_scr[main, :] = adaln1(jnp.where(is_p, xp_ref[0], xs_ref[0]))
    h_scr[HALO + ts:, :] = adaln1(jnp.where(is_p, xpn_ref[0], xsn_ref[0]))

    def proj(hrows, c0):
        return _dot(h_scr[hrows, :], wmat(win_ref, slice(c0, c0 + COLB))) + bin_ref[:, c0:c0 + COLB]

    def row_rstd(mu):
        ss = jnp.zeros((ts, 1), F32)
        for j in range(ncb):
            cols = slice(j * COLB, (j + 1) * COLB)
            xc = v_scr[:, cols] - mu
            ss = ss + jnp.sum(xc * xc, axis=-1, keepdims=True)
        return lax.rsqrt(ss * (1.0 / D) + LN_EPS)

    vsum = jnp.zeros((ts, 1), F32)
    for j in range(ncb):
        cols = slice(j * COLB, (j + 1) * COLB)
        u_scr[:, cols] = _gelu_tanh(proj(main, j * COLB))
        gv = _gelu_tanh(proj(main, D + j * COLB))
        v_scr[:, cols] = gv
        vsum = vsum + jnp.sum(gv, axis=-1, keepdims=True)
    vmu = vsum * (1.0 / D)
    vrs = row_rstd(vmu)
    for j in range(ncb):
        cols = slice(j * COLB, (j + 1) * COLB)
        vb_scr[:, cols] = ((v_scr[:, cols] - vmu) * vrs * sg_ref[:, cols] + sb_ref[:, cols]).astype(BF16)

    def pool_group(gi):
        w = POOL_WINDOWS[gi]
        cols = slice(gi * POOL_GROUP_DIM, (gi + 1) * POOL_GROUP_DIM)
        ext_pos = lax.broadcasted_iota(jnp.int32, (n_ext, 1), 0) + (i * ts - HALO)
        ext_valid = (ext_pos >= 0) & (ext_pos < s_len)
        zp = jnp.where(ext_valid, proj(slice(None), 2 * D + gi * POOL_GROUP_DIM), 0.0)
        acc = zp + pltpu.roll(zp, 1, 0)
        if w >= 4:
            acc = pltpu.roll(acc, 1, 0) + pltpu.roll(acc, n_ext - 1, 0)
        if w >= 8:
            acc = pltpu.roll(acc, 2, 0) + pltpu.roll(acc, n_ext - 2, 0)
        if w >= 16:
            acc = pltpu.roll(acc, 4, 0) + pltpu.roll(acc, n_ext - 4, 0)
        inv_count = icnt_ref[:, gi:gi + 1]
        dd = (acc[main] * inv_count - zp[main]).astype(BF16)
        p_scr[:, cols] = (_dot(dd, pltpu.bitcast(wpool_ref[gi], BF16)) * psc_ref[:, cols]).astype(BF16)

    def sgu_chunk(c):
        crow = slice(c * CHUNK, (c + 1) * CHUNK)
        for hh in range(SGU_HEADS):
            cols = slice(hh * HEAD_DIM, (hh + 1) * HEAD_DIM)
            mixed = _dot(ws_ref[hh], vb_scr[crow, cols]) + bsf_ref[:, cols]
            a_scr[crow, cols] = (u_scr[crow, cols] * mixed).astype(BF16)

    n_chunks = ts // CHUNK
    for k in range(max(n_chunks, len(POOL_WINDOWS))):
        if k < len(POOL_WINDOWS):
            pool_group(k)
        if k < n_chunks:
            sgu_chunk(k)

    for j in range(ncb):
        cols = slice(j * COLB, (j + 1) * COLB)
        ga = jax.nn.sigmoid(proj(main, 3 * D + j * COLB))
        gb = jax.nn.sigmoid(proj(main, 4 * D + j * COLB))
        ta = _dot(a_scr[...], wmat(wa_ref, cols))
        tb = _dot(p_scr[...], wmat(wb_ref, cols))
        m_scr[:, cols] = (ga * ta + gb * tb).astype(BF16)

    ysum = jnp.zeros((ts, 1), F32)
    for j in range(ncb):
        cols = slice(j * COLB, (j + 1) * COLB)
        t = _dot(m_scr[...], wmat(wo_ref, cols))
        y = alpha * jnp.where(is_p, xp_ref[0, :, cols], xs_ref[0, :, cols]) + gate1[:, cols] * t
        v_scr[:, cols] = y
        ysum = ysum + jnp.sum(y, axis=-1, keepdims=True)
    ymu = ysum * (1.0 / D)
    yrs = row_rstd(ymu)
    xsum = jnp.zeros((ts, 1), F32)
    for j in range(ncb):
        cols = slice(j * COLB, (j + 1) * COLB)
        x1 = (v_scr[:, cols] - ymu) * yrs * l1g_ref[:, cols] + l1b_ref[:, cols]
        x1_ref[0, :, cols] = x1
        v_scr[:, cols] = x1
        xsum = xsum + jnp.sum(x1, axis=-1, keepdims=True)

    xmu = xsum * (1.0 / D)
    xrs = row_rstd(xmu)
    for j in range(ncb):
        cols = slice(j * COLB, (j + 1) * COLB)
        h2 = (v_scr[:, cols] - xmu) * xrs * (1.0 + scale2[:, cols]) + shift2[:, cols]
        h2_scr[:, cols] = h2.astype(BF16)
    lo_bits = lax.bitcast_convert_type(h2_scr[:, 0:half].astype(F32), U32)
    hi_bits = lax.bitcast_convert_type(h2_scr[:, half:D].astype(F32), U32)
    h2u_ref[:, 0:half] = (lo_bits >> 16) | (hi_bits & jnp.uint32(0xFFFF0000))

    logits = _dot(h2_scr[...], wmat(wr_ref)) + br_ref[...]
    lane = lax.broadcasted_iota(jnp.int32, (ts, ROUTER_LANES), 1)
    lane_f = lane.astype(F32)
    neg = -jnp.inf
    is_g = lane < N_GROUPS
    lg = jnp.where(is_g, logits, neg)
    mg = jnp.max(lg, axis=-1, keepdims=True)
    g_sel = jnp.min(jnp.where(lg == mg, lane_f, float(ROUTER_LANES)), axis=-1, keepdims=True)
    p_sel = 1.0 / jnp.sum(jnp.where(is_g, jnp.exp(logits - mg), 0.0), axis=-1, keepdims=True)
    e_lo = EXPERT_LANE0 + g_sel * EPG
    in_grp = (lane_f >= e_lo) & (lane_f < e_lo + EPG)
    le = jnp.where(in_grp, logits, neg)
    m1 = jnp.max(le, axis=-1, keepdims=True)
    i1 = jnp.min(jnp.where(le == m1, lane_f, float(ROUTER_LANES)), axis=-1, keepdims=True)
    le2 = jnp.where(lane_f == i1, neg, le)
    m2 = jnp.max(le2, axis=-1, keepdims=True)
    i2 = jnp.min(jnp.where(le2 == m2, lane_f, float(ROUTER_LANES)), axis=-1, keepdims=True)
    e2x = jnp.exp(m2 - m1)
    den = 1.0 + e2x
    g1 = p_sel / den
    g2 = p_sel * e2x / den

    j1 = i1 - e_lo
    j2 = i2 - e_lo
    first_is_a = j1 < j2
    ja = jnp.minimum(j1, j2)
    jb = jnp.maximum(j1, j2)
    cls = g_sel * PAIRS + (EPG - 1) * ja - ja * (ja - 1.0) * 0.5 + (jb - ja - 1.0)
    w_a = jnp.where(first_is_a, g1, g2)
    w_b = jnp.where(first_is_a, g2, g1)
    extra = jnp.where(lane == 0, w_a, jnp.where(lane == 1, w_b, 0.0))
    h2u_ref[:, half:half + ROUTER_LANES] = lax.bitcast_convert_type(extra, U32)

    hit = lane_f == cls
    onehot = jnp.where(hit, 1.0, 0.0)
    pre = _dot(tri_ref[...], onehot.astype(BF16)) + carry_scr[0:1, :]
    rank = jnp.sum(jnp.where(hit, pre, 0.0), axis=-1, keepdims=True)
    carry_scr[0:1, :] = carry_scr[0:1, :] + jnp.sum(onehot, axis=0, keepdims=True)
    cnt_ref[...] = carry_scr[...]

    cr = jnp.where(lane == 0, cls, jnp.where(lane == 1, rank, 0.0))
    cr_ref[...] = cr.T[0:8, :]


def _const_spec(shape):
    nd = len(shape)
    return pl.BlockSpec(shape, lambda b, i: (0,) * nd, pipeline_mode=pl.Buffered(1))


def _mixer(x_p, x_s, mod, wts, alpha, ts):
    nbp, s_len, _ = x_p.shape
    nbs = x_s.shape[0]
    assert x_s.shape[1] == s_len and s_len % ts == 0 and ts % CHUNK == 0
    nb = nbp + nbs
    n_i = s_len // ts
    hb = ts // HALO
    n_hb = s_len // HALO

    def pick(b, on_p, val, const):
        return jnp.where(b < nbp if on_p else b >= nbp, val, const)

    def main_map(on_p):
        def f(b, i):
            bb = pick(b, on_p, b if on_p else b - nbp, nbp - 1 if on_p else 0)
            ii = pick(b, on_p, i, n_i - 1 if on_p else 0)
            return (bb, ii, 0)
        return f

    def halo_map(on_p, nxt):
        def f(b, i):
            bb = pick(b, on_p, b if on_p else b - nbp, nbp - 1 if on_p else 0)
            idx = jnp.minimum((i + 1) * hb, n_hb - 1) if nxt else jnp.maximum(i * hb - 1, 0)
            ii = pick(b, on_p, idx, n_hb - 1 if on_p else 0)
            return (bb, ii, 0)
        return f

    tri = (lax.broadcasted_iota(jnp.int32, (ts, ts), 1)
           < lax.broadcasted_iota(jnp.int32, (ts, ts), 0)).astype(BF16)
    consts = list(wts) + [tri]
    in_specs = [
        pl.BlockSpec((1, ts, D), main_map(True)),
        pl.BlockSpec((1, ts, D), main_map(False)),
        pl.BlockSpec((1, HALO, D), halo_map(True, False)),
        pl.BlockSpec((1, HALO, D), halo_map(False, False)),
        pl.BlockSpec((1, HALO, D), halo_map(True, True)),
        pl.BlockSpec((1, HALO, D), halo_map(False, True)),
        pl.BlockSpec((1, N_MOD, D), lambda b, i: (b, 0, 0)),
        pl.BlockSpec((ts, ROUTER_LANES), lambda b, i: (i, 0)),
    ] + [_const_spec(w.shape) for w in consts]

    pos = jnp.arange(s_len, dtype=jnp.int32)[:, None]
    win = jnp.asarray(POOL_WINDOWS + (1,) * (ROUTER_LANES - len(POOL_WINDOWS)), jnp.int32)[None, :]
    inv_count = 1.0 / (jnp.minimum(pos + (win - 1 - win // 2), s_len - 1)
                       - jnp.maximum(pos - win // 2, 0) + 1).astype(F32)
    n_tok = nb * s_len
    out_shape = (
        jax.ShapeDtypeStruct((nb, s_len, D), F32),
        jax.ShapeDtypeStruct((n_tok, ROW_W), U32),
        jax.ShapeDtypeStruct((8, n_tok), F32),
        jax.ShapeDtypeStruct((8, ROUTER_LANES), F32),
    )
    out_specs = (
        pl.BlockSpec((1, ts, D), lambda b, i: (b, i, 0)),
        pl.BlockSpec((ts, ROW_W), lambda b, i: (b * n_i + i, 0)),
        pl.BlockSpec((8, ts), lambda b, i: (0, b * n_i + i)),
        pl.BlockSpec((8, ROUTER_LANES), lambda b, i: (0, 0)),
    )
    scratch = [
        pltpu.VMEM((ts + 2 * HALO, D), BF16),
        pltpu.VMEM((ts, D), F32),
        pltpu.VMEM((ts, D), F32),
        pltpu.VMEM((ts, D), BF16),
        pltpu.VMEM((ts, D), BF16),
        pltpu.VMEM((ts, D), BF16),
        pltpu.VMEM((ts, D), BF16),
        pltpu.VMEM((ts, D), BF16),
        pltpu.VMEM((8, ROUTER_LANES), F32),
    ]
    return pl.pallas_call(
        functools.partial(_mixer_kernel, nbp, ts, s_len, alpha),
        grid=(nb, n_i),
        in_specs=in_specs,
        out_specs=out_specs,
        out_shape=out_shape,
        scratch_shapes=scratch,
        compiler_params=pltpu.CompilerParams(
            dimension_semantics=("arbitrary", "arbitrary"), vmem_limit_bytes=VMEM_LIMIT),
        name="mixer",
    )(x_p, x_s, x_p, x_s, x_p, x_s, mod, inv_count, *consts)


def _dest_kernel(seg_ref, cls_ref, rank_ref, o_ref):
    cls = cls_ref[...]
    acc = rank_ref[...].astype(jnp.int32)
    for c in range(N_CLASSES):
        acc = acc + jnp.where(cls == float(c), seg_ref[c], 0)
    o_ref[...] = acc


def _dest(seg_start, cls, rank):
    shape = cls.shape
    return pl.pallas_call(
        _dest_kernel,
        grid_spec=pltpu.PrefetchScalarGridSpec(
            num_scalar_prefetch=1,
            grid=(1,),
            in_specs=[pl.BlockSpec(shape, lambda i, s: (0, 0)), pl.BlockSpec(shape, lambda i, s: (0, 0))],
            out_specs=pl.BlockSpec(shape, lambda i, s: (0, 0)),
        ),
        out_shape=jax.ShapeDtypeStruct(shape, jnp.int32),
        name="dest",
    )(seg_start, cls, rank)


def _sc_mesh():
    return plsc.VectorSubcoreMesh(core_axis_name="core", subcore_axis_name="subcore")


def _sc_scatter_rows(rows, dest, n_rows, win):
    n_tok, row_w = rows.shape
    mesh = _sc_mesh()
    n_workers = mesh.num_cores * mesh.num_subcores
    n_steps = n_tok // win
    assert n_tok % win == 0 and n_steps % n_workers == 0
    per_worker = n_steps // n_workers

    @pl.kernel(out_type=jax.ShapeDtypeStruct((n_rows, row_w), rows.dtype), mesh=mesh,
               scratch_types=[pltpu.VMEM((win,), jnp.int32), pltpu.VMEM((win, row_w), rows.dtype)])
    def scatter(rows_hbm, idx_hbm, out_hbm, idx_vmem, buf):
        worker = lax.axis_index("core") * mesh.num_subcores + lax.axis_index("subcore")

        @pl.loop(0, per_worker)
        def _(k):
            step = worker * per_worker + k
            pltpu.sync_copy(idx_hbm.at[step], idx_vmem)
            pltpu.sync_copy(rows_hbm.at[pl.ds(step * win, win)], buf)
            pltpu.sync_copy(buf, out_hbm.at[idx_vmem])

    return scatter(rows, dest.reshape(n_steps, win))


def _sc_gather_rows(table, idx, win):
    n_tok = idx.shape[0]
    row_w = table.shape[1]
    mesh = _sc_mesh()
    n_workers = mesh.num_cores * mesh.num_subcores
    n_steps = n_tok // win
    assert n_tok % win == 0 and n_steps % n_workers == 0
    per_worker = n_steps // n_workers

    @pl.kernel(out_type=jax.ShapeDtypeStruct((n_tok, row_w), table.dtype), mesh=mesh,
               scratch_types=[pltpu.VMEM((win,), jnp.int32), pltpu.VMEM((win, row_w), table.dtype)])
    def gather(table_hbm, idx_hbm, out_hbm, idx_vmem, buf):
        worker = lax.axis_index("core") * mesh.num_subcores + lax.axis_index("subcore")

        @pl.loop(0, per_worker)
        def _(k):
            step = worker * per_worker + k
            pltpu.sync_copy(idx_hbm.at[step], idx_vmem)
            pltpu.sync_copy(table_hbm.at[idx_vmem], buf)
            pltpu.sync_copy(buf, out_hbm.at[pl.ds(step * win, win)])

    return gather(table, idx.reshape(n_steps, win))


def _sc_pack_rows(w, chunk):
    *lead, k, n = w.shape
    rows = math.prod(lead) * k
    mesh = _sc_mesh()
    lanes = SC_LANES
    n_workers = mesh.num_cores * mesh.num_subcores
    assert k % 2 == 0 and rows % (n_workers * chunk) == 0 and chunk % 2 == 0 and n % (lanes * SC_PACK_UNROLL) == 0
    per_worker = rows // (n_workers * chunk)

    def to_bf16_bits(x):
        bits = lax.bitcast_convert_type(x, U32)
        return (bits + jnp.uint32(0x7FFF) + ((bits >> 16) & jnp.uint32(1))) >> 16

    @pl.kernel(out_type=jax.ShapeDtypeStruct((rows // 2, n), U32), mesh=mesh,
               scratch_types=[pltpu.VMEM((chunk, n), F32), pltpu.VMEM((chunk // 2, n), U32)],
               compiler_params=pltpu.CompilerParams(needs_layout_passes=False))
    def pack(w_hbm, out_hbm, src, dst):
        worker = lax.axis_index("core") * mesh.num_subcores + lax.axis_index("subcore")

        @pl.loop(0, per_worker)
        def _(step):
            blk = worker * per_worker + step
            row0 = pl.multiple_of(blk * chunk, chunk)
            out_row0 = pl.multiple_of(blk * (chunk // 2), chunk // 2)
            pltpu.sync_copy(w_hbm.at[pl.ds(row0, chunk)], src)

            @pl.loop(0, chunk // 2)
            def _(q):
                @pl.loop(0, n, step=lanes * SC_PACK_UNROLL)
                def _(c0):
                    for u in range(SC_PACK_UNROLL):
                        c = c0 + u * lanes
                        lo = to_bf16_bits(src[2 * q, pl.ds(c, lanes)])
                        hi = to_bf16_bits(src[2 * q + 1, pl.ds(c, lanes)])
                        dst[q, pl.ds(c, lanes)] = lo | (hi << 16)

            pltpu.sync_copy(dst, out_hbm.at[pl.ds(out_row0, chunk // 2)])

    return pack(w.reshape(rows, n)).reshape(*lead, k // 2, n)


def _expert_kernel(bm, ea_ref, eb_ref, nused_ref, xs_ref, wga_ref, wua_ref, wda_ref, wgb_ref, wub_ref, wdb_ref,
                   ys_ref, act_scr):
    j = pl.program_id(0)
    used = j < nused_ref[0]
    half = D // 2

    def wmat(ref, k0, k1, cols):
        return pltpu.bitcast(ref[0, k0 // 2:k1 // 2, cols], BF16)

    @pl.when(used)
    def _():
        bits = xs_ref[:, 0:half]
        lo = lax.bitcast_convert_type(bits << 16, F32).astype(BF16)
        hi = lax.bitcast_convert_type(bits & jnp.uint32(0xFFFF0000), F32).astype(BF16)
        wts = lax.bitcast_convert_type(xs_ref[:, half:half + ROUTER_LANES], F32)
        for e, (wg_ref, wu_ref) in enumerate(((wga_ref, wua_ref), (wgb_ref, wub_ref))):
            for c in range(D_EXPERT // COLB):
                cols = slice(c * COLB, (c + 1) * COLB)
                g = _dot(lo, wmat(wg_ref, 0, half, cols)) + _dot(hi, wmat(wg_ref, half, D, cols))
                up = _dot(lo, wmat(wu_ref, 0, half, cols)) + _dot(hi, wmat(wu_ref, half, D, cols))
                act_scr[e, :, cols] = (jax.nn.silu(g) * up).astype(BF16)
        w_a = wts[:, 0:1]
        w_b = wts[:, 1:2]

        def y_block(c0):
            cols = slice(c0, c0 + COLB)
            return (w_a * _dot(act_scr[0], wmat(wda_ref, 0, D_EXPERT, cols))
                    + w_b * _dot(act_scr[1], wmat(wdb_ref, 0, D_EXPERT, cols)))

        for c0 in range(0, half, COLB):
            lo_bits = lax.bitcast_convert_type(y_block(c0).astype(BF16).astype(F32), U32)
            hi_bits = lax.bitcast_convert_type(y_block(half + c0).astype(BF16).astype(F32), U32)
            ys_ref[:, c0:c0 + COLB] = (lo_bits >> 16) | (hi_bits & jnp.uint32(0xFFFF0000))


def _experts(xs, blk_a, blk_b, n_used, w_gate, w_up, w_down, bm):
    n_rows = xs.shape[0]
    n_blocks = n_rows // bm

    def x_map(j, ea, eb, nu):
        return (jnp.minimum(j, nu[0] - 1), 0)

    def a_map(j, ea, eb, nu):
        return (ea[j], 0, 0)

    def b_map(j, ea, eb, nu):
        return (eb[j], 0, 0)

    return pl.pallas_call(
        functools.partial(_expert_kernel, bm),
        grid_spec=pltpu.PrefetchScalarGridSpec(
            num_scalar_prefetch=3,
            grid=(n_blocks,),
            in_specs=[pl.BlockSpec((bm, ROW_W), x_map),
                      pl.BlockSpec((1, D // 2, D_EXPERT), a_map),
                      pl.BlockSpec((1, D // 2, D_EXPERT), a_map),
                      pl.BlockSpec((1, D_EXPERT // 2, D), a_map),
                      pl.BlockSpec((1, D // 2, D_EXPERT), b_map),
                      pl.BlockSpec((1, D // 2, D_EXPERT), b_map),
                      pl.BlockSpec((1, D_EXPERT // 2, D), b_map)],
            out_specs=pl.BlockSpec((bm, D // 2), x_map),
            scratch_shapes=[pltpu.VMEM((2, bm, D_EXPERT), BF16)],
        ),
        out_shape=jax.ShapeDtypeStruct((n_rows, D // 2), U32),
        compiler_params=pltpu.CompilerParams(
            dimension_semantics=("arbitrary",), vmem_limit_bytes=VMEM_LIMIT),
        name="experts",
    )(blk_a, blk_b, n_used, xs, w_gate, w_up, w_down, w_gate, w_up, w_down)


def _final_kernel(alpha, x1_ref, f_ref, mod_ref, l2g_ref, l2b_ref, out_ref):
    half = D // 2
    gate2 = mod_ref[0][5:6]
    bits = f_ref[...]
    f_lo = lax.bitcast_convert_type(bits << 16, F32)
    f_hi = lax.bitcast_convert_type(bits & jnp.uint32(0xFFFF0000), F32)
    y_lo = alpha * x1_ref[0, :, 0:half] + gate2[:, 0:half] * f_lo
    y_hi = alpha * x1_ref[0, :, half:D] + gate2[:, half:D] * f_hi
    mu = (jnp.sum(y_lo, axis=-1, keepdims=True) + jnp.sum(y_hi, axis=-1, keepdims=True)) * (1.0 / D)
    c_lo = y_lo - mu
    c_hi = y_hi - mu
    var = (jnp.sum(c_lo * c_lo, axis=-1, keepdims=True) + jnp.sum(c_hi * c_hi, axis=-1, keepdims=True)) * (1.0 / D)
    rs = lax.rsqrt(var + LN_EPS)
    out_ref[0, :, 0:half] = c_lo * rs * l2g_ref[:, 0:half] + l2b_ref[:, 0:half]
    out_ref[0, :, half:D] = c_hi * rs * l2g_ref[:, half:D] + l2b_ref[:, half:D]


def _final(x1, f, mod, ln2_g, ln2_b, b_off, nbg, alpha, tk):
    s_len = x1.shape[1]
    n_i = s_len // tk
    return pl.pallas_call(
        functools.partial(_final_kernel, alpha),
        grid=(nbg, n_i),
        in_specs=[pl.BlockSpec((1, tk, D), lambda b, i: (b + b_off, i, 0)),
                  pl.BlockSpec((tk, D // 2), lambda b, i: (b * n_i + i, 0)),
                  pl.BlockSpec((1, N_MOD, D), lambda b, i: (b + b_off, 0, 0)),
                  pl.BlockSpec((1, D), lambda b, i: (0, 0)),
                  pl.BlockSpec((1, D), lambda b, i: (0, 0))],
        out_specs=pl.BlockSpec((1, tk, D), lambda b, i: (b, i, 0)),
        out_shape=jax.ShapeDtypeStruct((nbg, s_len, D), F32),
        compiler_params=pltpu.CompilerParams(
            dimension_semantics=("arbitrary", "arbitrary"), vmem_limit_bytes=VMEM_LIMIT),
        name="final",
    )(x1, f, mod, ln2_g, ln2_b)


def _class_experts():
    ea, eb = [], []
    for g in range(N_GROUPS):
        for a in range(EPG):
            for b in range(a + 1, EPG):
                ea.append(g * EPG + a)
                eb.append(g * EPG + b)
    return np.asarray(ea, np.int32), np.asarray(eb, np.int32)


def _layer(x_p, x_s, c_all, p, alpha):
    nbp, s_len, _ = x_p.shape
    nbs = x_s.shape[0]
    nb = nbp + nbs
    n_tok = nb * s_len

    mod = _modulation(c_all, p["w_mod"], p["b_mod"]).reshape(nb, N_MOD, D)

    bsf = jnp.repeat(p["b_spatial"].T, HEAD_DIM, axis=1)
    n_in = p["w_in"].shape[1]
    w_router = jnp.concatenate(
        [p["w_router_group"], p["w_router_expert"],
         jnp.zeros((D, ROUTER_LANES - N_GROUPS - N_EXPERTS), F32)], axis=1)
    b_router = jnp.concatenate(
        [p["b_router_group"], p["b_router_expert"],
         jnp.zeros((ROUTER_LANES - N_GROUPS - N_EXPERTS,), F32)]).reshape(1, ROUTER_LANES)
    wts = [
        _pack_rows(p["w_in"]), p["b_in"].reshape(1, n_in),
        p["w_spatial"].astype(BF16), bsf,
        p["sgu_g"].reshape(1, D), p["sgu_b"].reshape(1, D),
        _pack_rows(p["w_pool"]), p["pool_scale"].reshape(1, D),
        _pack_rows(p["w_branch_a"]), _pack_rows(p["w_branch_b"]), _pack_rows(p["w_out"]),
        p["ln1_g"].reshape(1, D), p["ln1_b"].reshape(1, D),
        _pack_rows(w_router), b_router,
    ]
    expert_w = [_sc_pack_rows(p[name], SC_PACK_CHUNK) for name in ("w_exp_gate", "w_exp_up", "w_exp_down")]
    x1, rows, cr, cnt = _mixer(x_p, x_s, mod, wts, alpha, MIXER_TS)

    bm = EXPERT_BM
    assert n_tok % bm == 0 and n_tok % DEST_LANES == 0
    counts = cnt[0, :N_CLASSES].astype(jnp.int32)
    padded = (counts + bm - 1) // bm * bm
    seg_end = jnp.cumsum(padded).astype(jnp.int32)
    seg_start = seg_end - padded
    n_blocks = n_tok // bm + N_CLASSES
    block_start = jnp.arange(n_blocks, dtype=jnp.int32) * bm
    blk_cls = jnp.minimum(
        jnp.sum((seg_end[None, :] <= block_start[:, None]).astype(jnp.int32), axis=1), N_CLASSES - 1)
    cls_a, cls_b = _class_experts()
    blk_a = jnp.asarray(cls_a)[blk_cls]
    blk_b = jnp.asarray(cls_b)[blk_cls]
    n_used = (seg_end[-1:] // bm).astype(jnp.int32)

    lane_shape = (n_tok // DEST_LANES, DEST_LANES)
    dest = _dest(seg_start, cr[0].reshape(lane_shape), cr[1].reshape(lane_shape)).reshape(n_tok)

    xs = _sc_scatter_rows(rows, dest, n_blocks * bm, SC_WIN)
    ys = _experts(xs, blk_a, blk_b, n_used, *expert_w, bm)

    l2g = p["ln2_g"].reshape(1, D)
    l2b = p["ln2_b"].reshape(1, D)
    tp = nbp * s_len
    f_p = _sc_gather_rows(ys, dest[:tp], SC_WIN)
    f_s = _sc_gather_rows(ys, dest[tp:], SC_WIN)
    y_p = _final(x1, f_p, mod, l2g, l2b, 0, nbp, alpha, FINAL_TK)
    y_s = _final(x1, f_s, mod, l2g, l2b, nbp, nbs, alpha, FINAL_TK)
    return y_p, y_s


_PARAM_NAMES = ("w_mod", "b_mod", "w_in", "b_in", "w_spatial", "b_spatial", "sgu_g", "sgu_b", "w_pool",
                "pool_scale", "w_branch_a", "w_branch_b", "w_out", "ln1_g", "ln1_b", "w_router_group",
                "b_router_group", "w_router_expert", "b_router_expert", "w_exp_gate", "w_exp_up",
                "w_exp_down", "ln2_g", "ln2_b")


def kernel(x_prompt, x_sample, c_prompt, c_sample, w_mod, b_mod, w_in, b_in, w_spatial, b_spatial, sgu_g, sgu_b, w_pool, pool_scale, w_branch_a, w_branch_b, w_out, ln1_g, ln1_b, w_router_group, b_router_group, w_router_expert, b_router_expert, w_exp_gate, w_exp_up, w_exp_down, ln2_g, ln2_b):
    params = (w_mod, b_mod, w_in, b_in, w_spatial, b_spatial, sgu_g, sgu_b, w_pool, pool_scale,
              w_branch_a, w_branch_b, w_out, ln1_g, ln1_b, w_router_group, b_router_group,
              w_router_expert, b_router_expert, w_exp_gate, w_exp_up, w_exp_down, ln2_g, ln2_b)
    depth = w_mod.shape[0]
    alpha = (2.0 * depth) ** 0.25
    c_all = jnp.concatenate([c_prompt, c_sample], axis=0)
    y_p, y_s = x_prompt, x_sample
    for l in range(depth):
        p = {name: w[l] for name, w in zip(_PARAM_NAMES, params)}
        y_p, y_s = _layer(y_p, y_s, c_all, p, alpha)
    return (y_p, y_s)
```

```python
import functools
import math

import jax
import jax.numpy as jnp
import numpy as np
from jax import lax
from jax.experimental import pallas as pl
from jax.experimental.pallas import tpu as pltpu
from jax.experimental.pallas import tpu_sc as plsc

F32 = jnp.float32
BF16 = jnp.bfloat16
U32 = jnp.uint32

D = 1024
CHUNK = 128
SGU_HEADS = 8
HEAD_DIM = D // SGU_HEADS
POOL_WINDOWS = (2, 4, 8, 16)
POOL_GROUP_DIM = D // len(POOL_WINDOWS)
N_MOD = 6
N_GROUPS = 4
EPG = 8
N_EXPERTS = N_GROUPS * EPG
D_EXPERT = D // 2
LN_EPS = 1e-5

HALO = 16
COLB = 256
ROUTER_LANES = 128
EXPERT_LANE0 = N_GROUPS
PAIRS = EPG * (EPG - 1) // 2
N_CLASSES = N_GROUPS * PAIRS
ROW_W = D // 2 + ROUTER_LANES
MIXER_TS = 512
EXPERT_BM = 256
FINAL_TK = 2048
DEST_LANES = 512
PACK_BLOCK = 1024
SC_WIN = 128
SC_LANES = 16
SC_PACK_CHUNK = 32
SC_PACK_UNROLL = 8
VMEM_LIMIT = 56 * 1024 * 1024


def _ln(x):
    mu = jnp.mean(x, axis=-1, keepdims=True)
    xc = x - mu
    var = jnp.mean(xc * xc, axis=-1, keepdims=True)
    return xc * lax.rsqrt(var + LN_EPS)


_GELU_A = -2.0 * math.sqrt(2.0 / math.pi) * math.log2(math.e)
_GELU_B = _GELU_A * 0.044715


def _gelu_tanh(x):
    return x / (1.0 + jnp.exp2(x * (_GELU_A + _GELU_B * (x * x))))


def _dot(a, b):
    return jnp.dot(a, b, preferred_element_type=F32)


def _mod_kernel(c_ref, w_ref, b_ref, o_ref):
    a = jax.nn.silu(c_ref[...]).astype(BF16)
    o_ref[...] = _dot(a, w_ref[...].astype(BF16)) + b_ref[...]


def _modulation(c_all, w_mod, b_mod):
    nb = c_all.shape[0]
    n_out = w_mod.shape[1]
    cb = 512
    return pl.pallas_call(
        _mod_kernel,
        grid=(n_out // cb,),
        in_specs=[pl.BlockSpec((nb, D), lambda j: (0, 0)),
                  pl.BlockSpec((D, cb), lambda j: (0, j)),
                  pl.BlockSpec((1, cb), lambda j: (0, j))],
        out_specs=pl.BlockSpec((nb, cb), lambda j: (0, j)),
        out_shape=jax.ShapeDtypeStruct((nb, n_out), F32),
        name="mod",
    )(c_all, w_mod, b_mod.reshape(1, n_out))


def _pack_kernel(w_ref, o_ref):
    o_ref[...] = pltpu.bitcast(w_ref[...].astype(BF16), U32)


def _pack_rows(w):
    *lead, k, n = w.shape
    rows = math.prod(lead) * k
    rb = min(rows, PACK_BLOCK)
    nb = min(n, PACK_BLOCK)
    assert k % 2 == 0 and rows % rb == 0 and n % nb == 0
    packed = pl.pallas_call(
        _pack_kernel,
        grid=(rows // rb, n // nb),
        in_specs=[pl.BlockSpec((rb, nb), lambda i, j: (i, j))],
        out_specs=pl.BlockSpec((rb // 2, nb), lambda i, j: (i, j)),
        out_shape=jax.ShapeDtypeStruct((rows // 2, n), U32),
        name="pack",
    )(w.reshape(rows, n))
    return packed.reshape(*lead, k // 2, n)


def _mixer_kernel(nbp, ts, s_len, n_i, alpha,
                  xp_ref, xs_ref, xpp_ref, xsp_ref, xpn_ref, xsn_ref, mod_ref, modb_ref, icnt_ref,
                  win_ref, bin_ref, ws_ref, bsf_ref, sg_ref, sb_ref, wpool_ref, psc_ref,
                  wa_ref, wb_ref, wo_ref, l1g_ref, l1b_ref, wr_ref, br_ref, tri_ref,
                  x1_ref, h2u_ref, cr_ref, cnt_ref,
                  h_scr, u_scr, v_scr, vb_scr, a_scr, p_scr, m_scr, h2_scr, y_scr, ysum_scr, ga_scr, gb_scr,
                  carry_scr):
    t = pl.program_id(0)
    n_tiles = pl.num_programs(0) - 1
    tf = jnp.minimum(t, n_tiles - 1)
    b = tf // n_i
    i = tf % n_i
    is_p = b < nbp
    ncb = D // COLB
    n_ext = ts + 2 * HALO
    half = D // 2
    main = slice(HALO, HALO + ts)

    def wmat(ref, cols=slice(None)):
        return pltpu.bitcast(ref[:, cols], BF16)

    @pl.when(t == 0)
    def _():
        carry_scr[...] = jnp.zeros_like(carry_scr)
        y_scr[...] = jnp.zeros_like(y_scr)
        ysum_scr[...] = jnp.zeros_like(ysum_scr)

    def row_rstd(read, mu):
        ss = jnp.zeros((ts, 1), F32)
        for j in range(ncb):
            xc = read(slice(j * COLB, (j + 1) * COLB)) - mu
            ss = ss + jnp.sum(xc * xc, axis=-1, keepdims=True)
        return lax.rsqrt(ss * (1.0 / D) + LN_EPS)

    mod = mod_ref[0]
    shift1, scale1, gate1 = mod[0:1], mod[1:2], mod[2:3]

    def adaln1(xv):
        return (_ln(xv) * (1.0 + scale1) + shift1).astype(BF16)

    h_scr[0:HALO, :] = adaln1(jnp.where(is_p, xpp_ref[0], xsp_ref[0]))
    h_scr[main, :] = adaln1(jnp.where(is_p, xp_ref[0], xs_ref[0]))
    h_scr[HALO + ts:, :] = adaln1(jnp.where(is_p, xpn_ref[0], xsn_ref[0]))

    def proj(hrows, c0):
        return _dot(h_scr[hrows, :], wmat(win_ref, slice(c0, c0 + COLB))) + bin_ref[:, c0:c0 + COLB]

    modb = modb_ref[0]
    shift2, scale2 = modb[3:4], modb[4:5]
    ymu = ysum_scr[:, 0:1] * (1.0 / D)
    yrs = row_rstd(lambda cols: y_scr[:, cols], ymu)
    xsum = jnp.zeros((ts, 1), F32)
    for j in range(ncb):
        cols = slice(j * COLB, (j + 1) * COLB)
        x1 = (y_scr[:, cols] - ymu) * yrs * l1g_ref[:, cols] + l1b_ref[:, cols]
        x1_ref[0, :, cols] = x1
        xsum = xsum + jnp.sum(x1, axis=-1, keepdims=True)
    xmu = xsum * (1.0 / D)
    xrs = row_rstd(lambda cols: x1_ref[0, :, cols], xmu)
    for j in range(ncb):
        cols = slice(j * COLB, (j + 1) * COLB)
        h2 = (x1_ref[0, :, cols] - xmu) * xrs * (1.0 + scale2[:, cols]) + shift2[:, cols]
        h2_scr[:, cols] = h2.astype(BF16)
    lo_bits = lax.bitcast_convert_type(h2_scr[:, 0:half].astype(F32), U32)
    hi_bits = lax.bitcast_convert_type(h2_scr[:, half:D].astype(F32), U32)
    h2u_ref[:, 0:half] = (lo_bits >> 16) | (hi_bits & jnp.uint32(0xFFFF0000))

    vsum = jnp.zeros((ts, 1), F32)
    for j in range(ncb):
        cols = slice(j * COLB, (j + 1) * COLB)
        u_scr[:, cols] = _gelu_tanh(proj(main, j * COLB))
        gv = _gelu_tanh(proj(main, D + j * COLB))
        v_scr[:, cols] = gv
        vsum = vsum + jnp.sum(gv, axis=-1, keepdims=True)

    logits = _dot(h2_scr[...], wmat(wr_ref)) + br_ref[...]
    lane = lax.broadcasted_iota(jnp.int32, (ts, ROUTER_LANES), 1)
    lane_f = lane.astype(F32)
    neg = -jnp.inf
    is_g = lane < N_GROUPS
    lg = jnp.where(is_g, logits, neg)
    mg = jnp.max(lg, axis=-1, keepdims=True)
    g_sel = jnp.min(jnp.where(lg == mg, lane_f, float(ROUTER_LANES)), axis=-1, keepdims=True)
    p_sel = 1.0 / jnp.sum(jnp.where(is_g, jnp.exp(logits - mg), 0.0), axis=-1, keepdims=True)
    e_lo = EXPERT_LANE0 + g_sel * EPG
    in_grp = (lane_f >= e_lo) & (lane_f < e_lo + EPG)
    le = jnp.where(in_grp, logits, neg)
    m1 = jnp.max(le, axis=-1, keepdims=True)
    i1 = jnp.min(jnp.where(le == m1, lane_f, float(ROUTER_LANES)), axis=-1, keepdims=True)
    le2 = jnp.where(lane_f == i1, neg, le)
    m2 = jnp.max(le2, axis=-1, keepdims=True)
    i2 = jnp.min(jnp.where(le2 == m2, lane_f, float(ROUTER_LANES)), axis=-1, keepdims=True)
    e2x = jnp.exp(m2 - m1)
    den = 1.0 + e2x
    g1 = p_sel / den
    g2 = p_sel * e2x / den

    j1 = i1 - e_lo
    j2 = i2 - e_lo
    first_is_a = j1 < j2
    ja = jnp.minimum(j1, j2)
    jb = jnp.maximum(j1, j2)
    cls = g_sel * PAIRS + (EPG - 1) * ja - ja * (ja - 1.0) * 0.5 + (jb - ja - 1.0)
    w_a = jnp.where(first_is_a, g1, g2)
    w_b = jnp.where(first_is_a, g2, g1)
    extra = jnp.where(lane == 0, w_a, jnp.where(lane == 1, w_b, 0.0))
    h2u_ref[:, half:half + ROUTER_LANES] = lax.bitcast_convert_type(extra, U32)

    hit = lane_f == cls
    onehot = jnp.where(hit, (t > 0).astype(F32), 0.0)

    def finish_ranks():
        pre = _dot(tri_ref[...], onehot.astype(BF16)) + carry_scr[0:1, :]
        rank = jnp.sum(jnp.where(hit, pre, 0.0), axis=-1, keepdims=True)
        carry_scr[0:1, :] = carry_scr[0:1, :] + jnp.sum(onehot, axis=0, keepdims=True)
        cnt_ref[...] = carry_scr[...]
        cr = jnp.where(lane == 0, cls, jnp.where(lane == 1, rank, 0.0))
        cr_ref[...] = cr.T[0:8, :]

    vmu = vsum * (1.0 / D)
    vrs = row_rstd(lambda cols: v_scr[:, cols], vmu)
    for j in range(ncb):
        cols = slice(j * COLB, (j + 1) * COLB)
        vb_scr[:, cols] = ((v_scr[:, cols] - vmu) * vrs * sg_ref[:, cols] + sb_ref[:, cols]).astype(BF16)

    def pool_project(gi):
        ext_pos = lax.broadcasted_iota(jnp.int32, (n_ext, 1), 0) + (i * ts - HALO)
        ext_valid = (ext_pos >= 0) & (ext_pos < s_len)
        return jnp.where(ext_valid, proj(slice(None), 2 * D + gi * POOL_GROUP_DIM), 0.0)

    def pool_group(gi, zp):
        w = POOL_WINDOWS[gi]
        cols = slice(gi * POOL_GROUP_DIM, (gi + 1) * POOL_GROUP_DIM)
        acc = zp + pltpu.roll(zp, 1, 0)
        if w >= 4:
            acc = pltpu.roll(acc, 1, 0) + pltpu.roll(acc, n_ext - 1, 0)
        if w >= 8:
            acc = pltpu.roll(acc, 2, 0) + pltpu.roll(acc, n_ext - 2, 0)
        if w >= 16:
            acc = pltpu.roll(acc, 4, 0) + pltpu.roll(acc, n_ext - 4, 0)
        inv_count = icnt_ref[:, gi:gi + 1]
        dd = (acc[main] * inv_count - zp[main]).astype(BF16)
        p_scr[:, cols] = (_dot(dd, pltpu.bitcast(wpool_ref[gi], BF16)) * psc_ref[:, cols]).astype(BF16)

    def sgu_chunk(c):
        crow = slice(c * CHUNK, (c + 1) * CHUNK)
        for hh in range(SGU_HEADS):
            cols = slice(hh * HEAD_DIM, (hh + 1) * HEAD_DIM)
            mixed = _dot(ws_ref[hh], vb_scr[crow, cols]) + bsf_ref[:, cols]
            a_scr[crow, cols] = (u_scr[crow, cols] * mixed).astype(BF16)

    assert len(POOL_WINDOWS) == ncb
    for j in range(ncb):
        cols = slice(j * COLB, (j + 1) * COLB)
        zp = pool_project(j)
        ga_scr[:, cols] = jax.nn.sigmoid(proj(main, 3 * D + j * COLB))
        gb_scr[:, cols] = jax.nn.sigmoid(proj(main, 4 * D + j * COLB))
        pool_group(j, zp)
    for c in range(ts // CHUNK):
        sgu_chunk(c)

    for j in range(ncb):
        cols = slice(j * COLB, (j + 1) * COLB)
        ta = _dot(a_scr[...], wmat(wa_ref, cols))
        tb = _dot(p_scr[...], wmat(wb_ref, cols))
        m_scr[:, cols] = (ga_scr[:, cols] * ta + gb_scr[:, cols] * tb).astype(BF16)

    ysum = jnp.zeros((ts, 1), F32)
    for j in range(ncb):
        cols = slice(j * COLB, (j + 1) * COLB)
        tmix = _dot(m_scr[...], wmat(wo_ref, cols))
        y = alpha * jnp.where(is_p, xp_ref[0, :, cols], xs_ref[0, :, cols]) + gate1[:, cols] * tmix
        y_scr[:, cols] = y
        ysum = ysum + jnp.sum(y, axis=-1, keepdims=True)
    ysum_scr[...] = jnp.broadcast_to(ysum, ysum_scr.shape)

    finish_ranks()


def _const_spec(shape):
    nd = len(shape)
    return pl.BlockSpec(shape, lambda t: (0,) * nd, pipeline_mode=pl.Buffered(1))


def _mixer(x_p, x_s, mod, wts, alpha, ts):
    nbp, s_len, _ = x_p.shape
    nbs = x_s.shape[0]
    assert x_s.shape[1] == s_len and s_len % ts == 0 and ts % CHUNK == 0
    nb = nbp + nbs
    n_i = s_len // ts
    hb = ts // HALO
    n_hb = s_len // HALO

    n_tiles = nb * n_i

    def front(t):
        tf = jnp.minimum(t, n_tiles - 1)
        return tf // n_i, tf % n_i

    def back(t):
        return jnp.maximum(t - 1, 0)

    def pick(b, on_p, val, const):
        return jnp.where(b < nbp if on_p else b >= nbp, val, const)

    def main_map(on_p):
        def f(t):
            b, i = front(t)
            bb = pick(b, on_p, b if on_p else b - nbp, nbp - 1 if on_p else 0)
            ii = pick(b, on_p, i, n_i - 1 if on_p else 0)
            return (bb, ii, 0)
        return f

    def halo_map(on_p, nxt):
        def f(t):
            b, i = front(t)
            bb = pick(b, on_p, b if on_p else b - nbp, nbp - 1 if on_p else 0)
            idx = jnp.minimum((i + 1) * hb, n_hb - 1) if nxt else jnp.maximum(i * hb - 1, 0)
            ii = pick(b, on_p, idx, n_hb - 1 if on_p else 0)
            return (bb, ii, 0)
        return f

    tri = (lax.broadcasted_iota(jnp.int32, (ts, ts), 1)
           < lax.broadcasted_iota(jnp.int32, (ts, ts), 0)).astype(BF16)
    consts = list(wts) + [tri]
    in_specs = [
        pl.BlockSpec((1, ts, D), main_map(True)),
        pl.BlockSpec((1, ts, D), main_map(False)),
        pl.BlockSpec((1, HALO, D), halo_map(True, False)),
        pl.BlockSpec((1, HALO, D), halo_map(False, False)),
        pl.BlockSpec((1, HALO, D), halo_map(True, True)),
        pl.BlockSpec((1, HALO, D), halo_map(False, True)),
        pl.BlockSpec((1, N_MOD, D), lambda t: (front(t)[0], 0, 0)),
        pl.BlockSpec((1, N_MOD, D), lambda t: (back(t) // n_i, 0, 0)),
        pl.BlockSpec((ts, ROUTER_LANES), lambda t: (front(t)[1], 0)),
    ] + [_const_spec(w.shape) for w in consts]

    pos = jnp.arange(s_len, dtype=jnp.int32)[:, None]
    win = jnp.asarray(POOL_WINDOWS + (1,) * (ROUTER_LANES - len(POOL_WINDOWS)), jnp.int32)[None, :]
    inv_count = 1.0 / (jnp.minimum(pos + (win - 1 - win // 2), s_len - 1)
                       - jnp.maximum(pos - win // 2, 0) + 1).astype(F32)
    n_tok = nb * s_len
    out_shape = (
        jax.ShapeDtypeStruct((nb, s_len, D), F32),
        jax.ShapeDtypeStruct((n_tok, ROW_W), U32),
        jax.ShapeDtypeStruct((8, n_tok), F32),
        jax.ShapeDtypeStruct((8, ROUTER_LANES), F32),
    )
    out_specs = (
        pl.BlockSpec((1, ts, D), lambda t: (back(t) // n_i, back(t) % n_i, 0)),
        pl.BlockSpec((ts, ROW_W), lambda t: (back(t), 0)),
        pl.BlockSpec((8, ts), lambda t: (0, back(t))),
        pl.BlockSpec((8, ROUTER_LANES), lambda t: (0, 0)),
    )
    scratch = [
        pltpu.VMEM((ts + 2 * HALO, D), BF16),
        pltpu.VMEM((ts, D), F32),
        pltpu.VMEM((ts, D), F32),
        pltpu.VMEM((ts, D), BF16),
        pltpu.VMEM((ts, D), BF16),
        pltpu.VMEM((ts, D), BF16),
        pltpu.VMEM((ts, D), BF16),
        pltpu.VMEM((ts, D), BF16),
        pltpu.VMEM((ts, D), F32),
        pltpu.VMEM((ts, ROUTER_LANES), F32),
        pltpu.VMEM((ts, D), F32),
        pltpu.VMEM((ts, D), F32),
        pltpu.VMEM((8, ROUTER_LANES), F32),
    ]
    return pl.pallas_call(
        functools.partial(_mixer_kernel, nbp, ts, s_len, n_i, alpha),
        grid=(n_tiles + 1,),
        in_specs=in_specs,
        out_specs=out_specs,
        out_shape=out_shape,
        scratch_shapes=scratch,
        compiler_params=pltpu.CompilerParams(
            dimension_semantics=("arbitrary",), vmem_limit_bytes=VMEM_LIMIT),
        name="mixer",
    )(x_p, x_s, x_p, x_s, x_p, x_s, mod, mod, inv_count, *consts)


def _dest_kernel(seg_ref, cls_ref, rank_ref, o_ref):
    cls = cls_ref[...]
    acc = rank_ref[...].astype(jnp.int32)
    for c in range(N_CLASSES):
        acc = acc + jnp.where(cls == float(c), seg_ref[c], 0)
    o_ref[...] = acc


def _dest(seg_start, cls, rank):
    shape = cls.shape
    return pl.pallas_call(
        _dest_kernel,
        grid_spec=pltpu.PrefetchScalarGridSpec(
            num_scalar_prefetch=1,
            grid=(1,),
            in_specs=[pl.BlockSpec(shape, lambda i, s: (0, 0)), pl.BlockSpec(shape, lambda i, s: (0, 0))],
            out_specs=pl.BlockSpec(shape, lambda i, s: (0, 0)),
        ),
        out_shape=jax.ShapeDtypeStruct(shape, jnp.int32),
        name="dest",
    )(seg_start, cls, rank)


def _sc_mesh():
    return plsc.VectorSubcoreMesh(core_axis_name="core", subcore_axis_name="subcore")


def _sc_scatter_rows(rows, dest, n_rows, win):
    n_tok, row_w = rows.shape
    mesh = _sc_mesh()
    n_workers = mesh.num_cores * mesh.num_subcores
    n_steps = n_tok // win
    assert n_tok % win == 0 and n_steps % n_workers == 0
    per_worker = n_steps // n_workers

    @pl.kernel(out_type=jax.ShapeDtypeStruct((n_rows, row_w), rows.dtype), mesh=mesh,
               scratch_types=[pltpu.VMEM((win,), jnp.int32), pltpu.VMEM((win, row_w), rows.dtype)])
    def scatter(rows_hbm, idx_hbm, out_hbm, idx_vmem, buf):
        worker = lax.axis_index("core") * mesh.num_subcores + lax.axis_index("subcore")

        @pl.loop(0, per_worker)
        def _(k):
            step = worker * per_worker + k
            pltpu.sync_copy(idx_hbm.at[step], idx_vmem)
            pltpu.sync_copy(rows_hbm.at[pl.ds(step * win, win)], buf)
            pltpu.sync_copy(buf, out_hbm.at[idx_vmem])

    return scatter(rows, dest.reshape(n_steps, win))


def _sc_gather_rows(table, idx, win):
    n_tok = idx.shape[0]
    row_w = table.shape[1]
    mesh = _sc_mesh()
    n_workers = mesh.num_cores * mesh.num_subcores
    n_steps = n_tok // win
    assert n_tok % win == 0 and n_steps % n_workers == 0
    per_worker = n_steps // n_workers

    @pl.kernel(out_type=jax.ShapeDtypeStruct((n_tok, row_w), table.dtype), mesh=mesh,
               scratch_types=[pltpu.VMEM((win,), jnp.int32), pltpu.VMEM((win, row_w), table.dtype)])
    def gather(table_hbm, idx_hbm, out_hbm, idx_vmem, buf):
        worker = lax.axis_index("core") * mesh.num_subcores + lax.axis_index("subcore")

        @pl.loop(0, per_worker)
        def _(k):
            step = worker * per_worker + k
            pltpu.sync_copy(idx_hbm.at[step], idx_vmem)
            pltpu.sync_copy(table_hbm.at[idx_vmem], buf)
            pltpu.sync_copy(buf, out_hbm.at[pl.ds(step * win, win)])

    return gather(table, idx.reshape(n_steps, win))


def _sc_pack_rows(w, chunk):
    *lead, k, n = w.shape
    rows = math.prod(lead) * k
    mesh = _sc_mesh()
    lanes = SC_LANES
    n_workers = mesh.num_cores * mesh.num_subcores
    assert k % 2 == 0 and rows % (n_workers * chunk) == 0 and chunk % 2 == 0 and n % (lanes * SC_PACK_UNROLL) == 0
    per_worker = rows // (n_workers * chunk)

    def to_bf16_bits(x):
        bits = lax.bitcast_convert_type(x, U32)
        return (bits + jnp.uint32(0x7FFF) + ((bits >> 16) & jnp.uint32(1))) >> 16

    @pl.kernel(out_type=jax.ShapeDtypeStruct((rows // 2, n), U32), mesh=mesh,
               scratch_types=[pltpu.VMEM((chunk, n), F32), pltpu.VMEM((chunk // 2, n), U32)],
               compiler_params=pltpu.CompilerParams(needs_layout_passes=False))
    def pack(w_hbm, out_hbm, src, dst):
        worker = lax.axis_index("core") * mesh.num_subcores + lax.axis_index("subcore")

        @pl.loop(0, per_worker)
        def _(step):
            blk = worker * per_worker + step
            row0 = pl.multiple_of(blk * chunk, chunk)
            out_row0 = pl.multiple_of(blk * (chunk // 2), chunk // 2)
            pltpu.sync_copy(w_hbm.at[pl.ds(row0, chunk)], src)

            @pl.loop(0, chunk // 2)
            def _(q):
                @pl.loop(0, n, step=lanes * SC_PACK_UNROLL)
                def _(c0):
                    for u in range(SC_PACK_UNROLL):
                        c = c0 + u * lanes
                        lo = to_bf16_bits(src[2 * q, pl.ds(c, lanes)])
                        hi = to_bf16_bits(src[2 * q + 1, pl.ds(c, lanes)])
                        dst[q, pl.ds(c, lanes)] = lo | (hi << 16)

            pltpu.sync_copy(dst, out_hbm.at[pl.ds(out_row0, chunk // 2)])

    return pack(w.reshape(rows, n)).reshape(*lead, k // 2, n)


def _expert_kernel(bm, ea_ref, eb_ref, nused_ref, xs_ref, wga_ref, wua_ref, wda_ref, wgb_ref, wub_ref, wdb_ref,
                   ys_ref, act_scr):
    j = pl.program_id(0)
    used = j < nused_ref[0]
    half = D // 2

    def wmat(ref, k0, k1, cols):
        return pltpu.bitcast(ref[0, k0 // 2:k1 // 2, cols], BF16)

    @pl.when(used)
    def _():
        bits = xs_ref[:, 0:half]
        lo = lax.bitcast_convert_type(bits << 16, F32).astype(BF16)
        hi = lax.bitcast_convert_type(bits & jnp.uint32(0xFFFF0000), F32).astype(BF16)
        wts = lax.bitcast_convert_type(xs_ref[:, half:half + ROUTER_LANES], F32)
        for e, (wg_ref, wu_ref) in enumerate(((wga_ref, wua_ref), (wgb_ref, wub_ref))):
            for c in range(D_EXPERT // COLB):
                cols = slice(c * COLB, (c + 1) * COLB)
                g = _dot(lo, wmat(wg_ref, 0, half, cols)) + _dot(hi, wmat(wg_ref, half, D, cols))
                up = _dot(lo, wmat(wu_ref, 0, half, cols)) + _dot(hi, wmat(wu_ref, half, D, cols))
                act_scr[e, :, cols] = (jax.nn.silu(g) * up).astype(BF16)
        w_a = wts[:, 0:1]
        w_b = wts[:, 1:2]

        def y_block(c0):
            cols = slice(c0, c0 + COLB)
            return (w_a * _dot(act_scr[0], wmat(wda_ref, 0, D_EXPERT, cols))
                    + w_b * _dot(act_scr[1], wmat(wdb_ref, 0, D_EXPERT, cols)))

        for c0 in range(0, half, COLB):
            lo_bits = lax.bitcast_convert_type(y_block(c0).astype(BF16).astype(F32), U32)
            hi_bits = lax.bitcast_convert_type(y_block(half + c0).astype(BF16).astype(F32), U32)
            ys_ref[:, c0:c0 + COLB] = (lo_bits >> 16) | (hi_bits & jnp.uint32(0xFFFF0000))


def _experts(xs, blk_a, blk_b, n_used, w_gate, w_up, w_down, bm):
    n_rows = xs.shape[0]
    n_blocks = n_rows // bm

    def x_map(j, ea, eb, nu):
        return (jnp.minimum(j, nu[0] - 1), 0)

    def a_map(j, ea, eb, nu):
        return (ea[j], 0, 0)

    def b_map(j, ea, eb, nu):
        return (eb[j], 0, 0)

    return pl.pallas_call(
        functools.partial(_expert_kernel, bm),
        grid_spec=pltpu.PrefetchScalarGridSpec(
            num_scalar_prefetch=3,
            grid=(n_blocks,),
            in_specs=[pl.BlockSpec((bm, ROW_W), x_map),
                      pl.BlockSpec((1, D // 2, D_EXPERT), a_map),
                      pl.BlockSpec((1, D // 2, D_EXPERT), a_map),
                      pl.BlockSpec((1, D_EXPERT // 2, D), a_map),
                      pl.BlockSpec((1, D // 2, D_EXPERT), b_map),
                      pl.BlockSpec((1, D // 2, D_EXPERT), b_map),
                      pl.BlockSpec((1, D_EXPERT // 2, D), b_map)],
            out_specs=pl.BlockSpec((bm, D // 2), x_map),
            scratch_shapes=[pltpu.VMEM((2, bm, D_EXPERT), BF16)],
        ),
        out_shape=jax.ShapeDtypeStruct((n_rows, D // 2), U32),
        compiler_params=pltpu.CompilerParams(
            dimension_semantics=("arbitrary",), vmem_limit_bytes=VMEM_LIMIT),
        name="experts",
    )(blk_a, blk_b, n_used, xs, w_gate, w_up, w_down, w_gate, w_up, w_down)


def _final_kernel(alpha, x1_ref, f_ref, mod_ref, l2g_ref, l2b_ref, out_ref):
    half = D // 2
    gate2 = mod_ref[0][5:6]
    bits = f_ref[...]
    f_lo = lax.bitcast_convert_type(bits << 16, F32)
    f_hi = lax.bitcast_convert_type(bits & jnp.uint32(0xFFFF0000), F32)
    y_lo = alpha * x1_ref[0, :, 0:half] + gate2[:, 0:half] * f_lo
    y_hi = alpha * x1_ref[0, :, half:D] + gate2[:, half:D] * f_hi
    mu = (jnp.sum(y_lo, axis=-1, keepdims=True) + jnp.sum(y_hi, axis=-1, keepdims=True)) * (1.0 / D)
    c_lo = y_lo - mu
    c_hi = y_hi - mu
    var = (jnp.sum(c_lo * c_lo, axis=-1, keepdims=True) + jnp.sum(c_hi * c_hi, axis=-1, keepdims=True)) * (1.0 / D)
    rs = lax.rsqrt(var + LN_EPS)
    out_ref[0, :, 0:half] = c_lo * rs * l2g_ref[:, 0:half] + l2b_ref[:, 0:half]
    out_ref[0, :, half:D] = c_hi * rs * l2g_ref[:, half:D] + l2b_ref[:, half:D]


def _final(x1, f, mod, ln2_g, ln2_b, b_off, nbg, alpha, tk):
    s_len = x1.shape[1]
    n_i = s_len // tk
    return pl.pallas_call(
        functools.partial(_final_kernel, alpha),
        grid=(nbg, n_i),
        in_specs=[pl.BlockSpec((1, tk, D), lambda b, i: (b + b_off, i, 0)),
                  pl.BlockSpec((tk, D // 2), lambda b, i: (b * n_i + i, 0)),
                  pl.BlockSpec((1, N_MOD, D), lambda b, i: (b + b_off, 0, 0)),
                  pl.BlockSpec((1, D), lambda b, i: (0, 0)),
                  pl.BlockSpec((1, D), lambda b, i: (0, 0))],
        out_specs=pl.BlockSpec((1, tk, D), lambda b, i: (b, i, 0)),
        out_shape=jax.ShapeDtypeStruct((nbg, s_len, D), F32),
        compiler_params=pltpu.CompilerParams(
            dimension_semantics=("arbitrary", "arbitrary"), vmem_limit_bytes=VMEM_LIMIT),
        name="final",
    )(x1, f, mod, ln2_g, ln2_b)


def _class_experts():
    ea, eb = [], []
    for g in range(N_GROUPS):
        for a in range(EPG):
            for b in range(a + 1, EPG):
                ea.append(g * EPG + a)
                eb.append(g * EPG + b)
    return np.asarray(ea, np.int32), np.asarray(eb, np.int32)


def _layer(x_p, x_s, c_all, p, alpha):
    nbp, s_len, _ = x_p.shape
    nbs = x_s.shape[0]
    nb = nbp + nbs
    n_tok = nb * s_len

    mod = _modulation(c_all, p["w_mod"], p["b_mod"]).reshape(nb, N_MOD, D)

    bsf = jnp.repeat(p["b_spatial"].T, HEAD_DIM, axis=1)
    n_in = p["w_in"].shape[1]
    w_router = jnp.concatenate(
        [p["w_router_group"], p["w_router_expert"],
         jnp.zeros((D, ROUTER_LANES - N_GROUPS - N_EXPERTS), F32)], axis=1)
    b_router = jnp.concatenate(
        [p["b_router_group"], p["b_router_expert"],
         jnp.zeros((ROUTER_LANES - N_GROUPS - N_EXPERTS,), F32)]).reshape(1, ROUTER_LANES)
    wts = [
        _pack_rows(p["w_in"]), p["b_in"].reshape(1, n_in),
        p["w_spatial"].astype(BF16), bsf,
        p["sgu_g"].reshape(1, D), p["sgu_b"].reshape(1, D),
        _pack_rows(p["w_pool"]), p["pool_scale"].reshape(1, D),
        _pack_rows(p["w_branch_a"]), _pack_rows(p["w_branch_b"]), _pack_rows(p["w_out"]),
        p["ln1_g"].reshape(1, D), p["ln1_b"].reshape(1, D),
        _pack_rows(w_router), b_router,
    ]
    expert_w = [_sc_pack_rows(p[name], SC_PACK_CHUNK) for name in ("w_exp_gate", "w_exp_up", "w_exp_down")]
    x1, rows, cr, cnt = _mixer(x_p, x_s, mod, wts, alpha, MIXER_TS)

    bm = EXPERT_BM
    assert n_tok % bm == 0 and n_tok % DEST_LANES == 0
    counts = cnt[0, :N_CLASSES].astype(jnp.int32)
    padded = (counts + bm - 1) // bm * bm
    seg_end = jnp.cumsum(padded).astype(jnp.int32)
    seg_start = seg_end - padded
    n_blocks = n_tok // bm + N_CLASSES
    block_start = jnp.arange(n_blocks, dtype=jnp.int32) * bm
    blk_cls = jnp.minimum(
        jnp.sum((seg_end[None, :] <= block_start[:, None]).astype(jnp.int32), axis=1), N_CLASSES - 1)
    cls_a, cls_b = _class_experts()
    blk_a = jnp.asarray(cls_a)[blk_cls]
    blk_b = jnp.asarray(cls_b)[blk_cls]
    n_used = (seg_end[-1:] // bm).astype(jnp.int32)

    lane_shape = (n_tok // DEST_LANES, DEST_LANES)
    dest = _dest(seg_start, cr[0].reshape(lane_shape), cr[1].reshape(lane_shape)).reshape(n_tok)

    xs = _sc_scatter_rows(rows, dest, n_blocks * bm, SC_WIN)
    ys = _experts(xs, blk_a, blk_b, n_used, *expert_w, bm)

    l2g = p["ln2_g"].reshape(1, D)
    l2b = p["ln2_b"].reshape(1, D)
    tp = nbp * s_len
    f_p = _sc_gather_rows(ys, dest[:tp], SC_WIN)
    f_s = _sc_gather_rows(ys, dest[tp:], SC_WIN)
    y_p = _final(x1, f_p, mod, l2g, l2b, 0, nbp, alpha, FINAL_TK)
    y_s = _final(x1, f_s, mod, l2g, l2b, nbp, nbs, alpha, FINAL_TK)
    return y_p, y_s


_PARAM_NAMES = ("w_mod", "b_mod", "w_in", "b_in", "w_spatial", "b_spatial", "sgu_g", "sgu_b", "w_pool",
                "pool_scale", "w_branch_a", "w_branch_b", "w_out", "ln1_g", "ln1_b", "w_router_group",
                "b_router_group", "w_router_expert", "b_router_expert", "w_exp_gate", "w_exp_up",
                "w_exp_down", "ln2_g", "ln2_b")


def kernel(x_prompt, x_sample, c_prompt, c_sample, w_mod, b_mod, w_in, b_in, w_spatial, b_spatial, sgu_g, sgu_b, w_pool, pool_scale, w_branch_a, w_branch_b, w_out, ln1_g, ln1_b, w_router_group, b_router_group, w_router_expert, b_router_expert, w_exp_gate, w_exp_up, w_exp_down, ln2_g, ln2_b):
    params = (w_mod, b_mod, w_in, b_in, w_spatial, b_spatial, sgu_g, sgu_b, w_pool, pool_scale,
              w_branch_a, w_branch_b, w_out, ln1_g, ln1_b, w_router_group, b_router_group,
              w_router_expert, b_router_expert, w_exp_gate, w_exp_up, w_exp_down, ln2_g, ln2_b)
    depth = w_mod.shape[0]
    alpha = (2.0 * depth) ** 0.25
    c_all = jnp.concatenate([c_prompt, c_sample], axis=0)
    y_p, y_s = x_prompt, x_sample
    for l in range(depth):
        p = {name: w[l] for name, w in zip(_PARAM_NAMES, params)}
        y_p, y_s = _layer(y_p, y_s, c_all, p, alpha)
    return (y_p, y_s)
```

```python
import functools
import math

import jax
import jax.numpy as jnp
import numpy as np
from jax import lax
from jax.experimental import pallas as pl
from jax.experimental.pallas import tpu as pltpu
from jax.experimental.pallas import tpu_sc as plsc

F32 = jnp.float32
BF16 = jnp.bfloat16
U32 = jnp.uint32

D = 1024
CHUNK = 128
SGU_HEADS = 8
HEAD_DIM = D // SGU_HEADS
POOL_WINDOWS = (2, 4, 8, 16)
POOL_GROUP_DIM = D // len(POOL_WINDOWS)
N_MOD = 6
N_GROUPS = 4
EPG = 8
N_EXPERTS = N_GROUPS * EPG
D_EXPERT = D // 2
LN_EPS = 1e-5

HALO = 16
COLB = 256
ROUTER_LANES = 128
EXPERT_LANE0 = N_GROUPS
PAIRS = EPG * (EPG - 1) // 2
N_CLASSES = N_GROUPS * PAIRS
ROW_W = D // 2 + ROUTER_LANES
MIXER_TS = 512
EXPERT_BM = 256
FINAL_TK = 2048
DEST_LANES = 512
PACK_BLOCK = 1024
SC_WIN = 128
SC_LANES = 16
SC_PACK_CHUNK = 32
SC_PACK_UNROLL = 8
VMEM_LIMIT = 56 * 1024 * 1024


def _ln(x):
    mu = jnp.mean(x, axis=-1, keepdims=True)
    xc = x - mu
    var = jnp.mean(xc * xc, axis=-1, keepdims=True)
    return xc * lax.rsqrt(var + LN_EPS)


_GELU_A = -2.0 * math.sqrt(2.0 / math.pi) * math.log2(math.e)
_GELU_B = _GELU_A * 0.044715


def _gelu_tanh(x):
    return x / (1.0 + jnp.exp2(x * (_GELU_A + _GELU_B * (x * x))))


def _dot(a, b):
    return jnp.dot(a, b, preferred_element_type=F32)


def _mod_kernel(c_ref, w_ref, b_ref, o_ref):
    a = jax.nn.silu(c_ref[...]).astype(BF16)
    o_ref[...] = _dot(a, w_ref[...].astype(BF16)) + b_ref[...]


def _modulation(c_all, w_mod, b_mod):
    nb = c_all.shape[0]
    n_out = w_mod.shape[1]
    cb = 512
    return pl.pallas_call(
        _mod_kernel,
        grid=(n_out // cb,),
        in_specs=[pl.BlockSpec((nb, D), lambda j: (0, 0)),
                  pl.BlockSpec((D, cb), lambda j: (0, j)),
                  pl.BlockSpec((1, cb), lambda j: (0, j))],
        out_specs=pl.BlockSpec((nb, cb), lambda j: (0, j)),
        out_shape=jax.ShapeDtypeStruct((nb, n_out), F32),
        name="mod",
    )(c_all, w_mod, b_mod.reshape(1, n_out))


def _pack_kernel(w_ref, o_ref):
    o_ref[...] = pltpu.bitcast(w_ref[...].astype(BF16), U32)


def _pack_rows(w):
    *lead, k, n = w.shape
    rows = math.prod(lead) * k
    rb = min(rows, PACK_BLOCK)
    nb = min(n, PACK_BLOCK)
    assert k % 2 == 0 and rows % rb == 0 and n % nb == 0
    packed = pl.pallas_call(
        _pack_kernel,
        grid=(rows // rb, n // nb),
        in_specs=[pl.BlockSpec((rb, nb), lambda i, j: (i, j))],
        out_specs=pl.BlockSpec((rb // 2, nb), lambda i, j: (i, j)),
        out_shape=jax.ShapeDtypeStruct((rows // 2, n), U32),
        name="pack",
    )(w.reshape(rows, n))
    return packed.reshape(*lead, k // 2, n)


def _mixer_kernel(nbp, ts, s_len, n_i, alpha,
                  xp_ref, xs_ref, xpp_ref, xsp_ref, xpn_ref, xsn_ref, mod_ref, modb_ref, icnt_ref,
                  win_ref, bin_ref, ws_ref, bsf_ref, sg_ref, sb_ref, wpool_ref, psc_ref,
                  wa_ref, wb_ref, wo_ref, l1g_ref, l1b_ref, wr_ref, br_ref, tri_ref,
                  x1_ref, h2u_ref, cr_ref, cnt_ref,
                  h_scr, u_scr, v_scr, vb_scr, a_scr, p_scr, m_scr, h2_scr, y_scr, xk_scr, ga_scr, gb_scr,
                  carry_scr):
    t = pl.program_id(0)
    n_tiles = pl.num_programs(0) - 1
    tf = jnp.minimum(t, n_tiles - 1)
    b = tf // n_i
    i = tf % n_i
    is_p = b < nbp
    ncb = D // COLB
    n_ext = ts + 2 * HALO
    half = D // 2
    main = slice(HALO, HALO + ts)

    def wmat(ref, cols=slice(None)):
        return pltpu.bitcast(ref[:, cols], BF16)

    @pl.when(t == 0)
    def _():
        carry_scr[...] = jnp.zeros_like(carry_scr)
        m_scr[...] = jnp.zeros_like(m_scr)
        xk_scr[...] = jnp.zeros_like(xk_scr)

    def row_rstd(read, mu):
        ss = jnp.zeros((ts, 1), F32)
        for j in range(ncb):
            xc = read(slice(j * COLB, (j + 1) * COLB)) - mu
            ss = ss + jnp.sum(xc * xc, axis=-1, keepdims=True)
        return lax.rsqrt(ss * (1.0 / D) + LN_EPS)

    modb = modb_ref[0]
    gate1_b, shift2, scale2 = modb[2:3], modb[3:4], modb[4:5]
    ysum = jnp.zeros((ts, 1), F32)
    for j in range(ncb):
        cols = slice(j * COLB, (j + 1) * COLB)
        tmix = _dot(m_scr[...], wmat(wo_ref, cols))
        y = alpha * xk_scr[:, cols] + gate1_b[:, cols] * tmix
        y_scr[:, cols] = y
        ysum = ysum + jnp.sum(y, axis=-1, keepdims=True)

    mod = mod_ref[0]
    shift1, scale1 = mod[0:1], mod[1:2]

    def adaln1(xv):
        return (_ln(xv) * (1.0 + scale1) + shift1).astype(BF16)

    h_scr[0:HALO, :] = adaln1(jnp.where(is_p, xpp_ref[0], xsp_ref[0]))
    h_scr[main, :] = adaln1(jnp.where(is_p, xp_ref[0], xs_ref[0]))
    h_scr[HALO + ts:, :] = adaln1(jnp.where(is_p, xpn_ref[0], xsn_ref[0]))

    def proj(hrows, c0):
        return _dot(h_scr[hrows, :], wmat(win_ref, slice(c0, c0 + COLB))) + bin_ref[:, c0:c0 + COLB]

    ymu = ysum * (1.0 / D)
    yrs = row_rstd(lambda cols: y_scr[:, cols], ymu)
    xsum = jnp.zeros((ts, 1), F32)
    for j in range(ncb):
        cols = slice(j * COLB, (j + 1) * COLB)
        x1 = (y_scr[:, cols] - ymu) * yrs * l1g_ref[:, cols] + l1b_ref[:, cols]
        x1_ref[0, :, cols] = x1
        xsum = xsum + jnp.sum(x1, axis=-1, keepdims=True)
    xmu = xsum * (1.0 / D)
    xrs = row_rstd(lambda cols: x1_ref[0, :, cols], xmu)
    for j in range(ncb):
        cols = slice(j * COLB, (j + 1) * COLB)
        h2 = (x1_ref[0, :, cols] - xmu) * xrs * (1.0 + scale2[:, cols]) + shift2[:, cols]
        h2_scr[:, cols] = h2.astype(BF16)
    lo_bits = lax.bitcast_convert_type(h2_scr[:, 0:half].astype(F32), U32)
    hi_bits = lax.bitcast_convert_type(h2_scr[:, half:D].astype(F32), U32)
    h2u_ref[:, 0:half] = (lo_bits >> 16) | (hi_bits & jnp.uint32(0xFFFF0000))

    vsum = jnp.zeros((ts, 1), F32)
    for j in range(ncb):
        cols = slice(j * COLB, (j + 1) * COLB)
        u_scr[:, cols] = _gelu_tanh(proj(main, j * COLB))
        gv = _gelu_tanh(proj(main, D + j * COLB))
        v_scr[:, cols] = gv
        vsum = vsum + jnp.sum(gv, axis=-1, keepdims=True)

    logits = _dot(h2_scr[...], wmat(wr_ref)) + br_ref[...]
    lane = lax.broadcasted_iota(jnp.int32, (ts, ROUTER_LANES), 1)
    lane_f = lane.astype(F32)
    neg = -jnp.inf
    is_g = lane < N_GROUPS
    lg = jnp.where(is_g, logits, neg)
    mg = jnp.max(lg, axis=-1, keepdims=True)
    g_sel = jnp.min(jnp.where(lg == mg, lane_f, float(ROUTER_LANES)), axis=-1, keepdims=True)
    p_sel = 1.0 / jnp.sum(jnp.where(is_g, jnp.exp(logits - mg), 0.0), axis=-1, keepdims=True)
    e_lo = EXPERT_LANE0 + g_sel * EPG
    in_grp = (lane_f >= e_lo) & (lane_f < e_lo + EPG)
    le = jnp.where(in_grp, logits, neg)
    m1 = jnp.max(le, axis=-1, keepdims=True)
    i1 = jnp.min(jnp.where(le == m1, lane_f, float(ROUTER_LANES)), axis=-1, keepdims=True)
    le2 = jnp.where(lane_f == i1, neg, le)
    m2 = jnp.max(le2, axis=-1, keepdims=True)
    i2 = jnp.min(jnp.where(le2 == m2, lane_f, float(ROUTER_LANES)), axis=-1, keepdims=True)
    e2x = jnp.exp(m2 - m1)
    den = 1.0 + e2x
    g1 = p_sel / den
    g2 = p_sel * e2x / den

    j1 = i1 - e_lo
    j2 = i2 - e_lo
    first_is_a = j1 < j2
    ja = jnp.minimum(j1, j2)
    jb = jnp.maximum(j1, j2)
    cls = g_sel * PAIRS + (EPG - 1) * ja - ja * (ja - 1.0) * 0.5 + (jb - ja - 1.0)
    w_a = jnp.where(first_is_a, g1, g2)
    w_b = jnp.where(first_is_a, g2, g1)
    extra = jnp.where(lane == 0, w_a, jnp.where(lane == 1, w_b, 0.0))
    h2u_ref[:, half:half + ROUTER_LANES] = lax.bitcast_convert_type(extra, U32)

    hit = lane_f == cls
    onehot = jnp.where(hit, (t > 0).astype(F32), 0.0)

    def finish_ranks():
        pre = _dot(tri_ref[...], onehot.astype(BF16)) + carry_scr[0:1, :]
        rank = jnp.sum(jnp.where(hit, pre, 0.0), axis=-1, keepdims=True)
        carry_scr[0:1, :] = carry_scr[0:1, :] + jnp.sum(onehot, axis=0, keepdims=True)
        cnt_ref[...] = carry_scr[...]
        cr = jnp.where(lane == 0, cls, jnp.where(lane == 1, rank, 0.0))
        cr_ref[...] = cr.T[0:8, :]

    vmu = vsum * (1.0 / D)
    vrs = row_rstd(lambda cols: v_scr[:, cols], vmu)
    for j in range(ncb):
        cols = slice(j * COLB, (j + 1) * COLB)
        vb_scr[:, cols] = ((v_scr[:, cols] - vmu) * vrs * sg_ref[:, cols] + sb_ref[:, cols]).astype(BF16)

    def pool_project(gi):
        ext_pos = lax.broadcasted_iota(jnp.int32, (n_ext, 1), 0) + (i * ts - HALO)
        ext_valid = (ext_pos >= 0) & (ext_pos < s_len)
        return jnp.where(ext_valid, proj(slice(None), 2 * D + gi * POOL_GROUP_DIM), 0.0)

    def pool_group(gi, zp):
        w = POOL_WINDOWS[gi]
        cols = slice(gi * POOL_GROUP_DIM, (gi + 1) * POOL_GROUP_DIM)
        acc = zp + pltpu.roll(zp, 1, 0)
        if w >= 4:
            acc = pltpu.roll(acc, 1, 0) + pltpu.roll(acc, n_ext - 1, 0)
        if w >= 8:
            acc = pltpu.roll(acc, 2, 0) + pltpu.roll(acc, n_ext - 2, 0)
        if w >= 16:
            acc = pltpu.roll(acc, 4, 0) + pltpu.roll(acc, n_ext - 4, 0)
        inv_count = icnt_ref[:, gi:gi + 1]
        dd = (acc[main] * inv_count - zp[main]).astype(BF16)
        p_scr[:, cols] = (_dot(dd, pltpu.bitcast(wpool_ref[gi], BF16)) * psc_ref[:, cols]).astype(BF16)

    def sgu_chunk(c):
        crow = slice(c * CHUNK, (c + 1) * CHUNK)
        for hh in range(SGU_HEADS):
            cols = slice(hh * HEAD_DIM, (hh + 1) * HEAD_DIM)
            mixed = _dot(ws_ref[hh], vb_scr[crow, cols]) + bsf_ref[:, cols]
            a_scr[crow, cols] = (u_scr[crow, cols] * mixed).astype(BF16)

    assert len(POOL_WINDOWS) == ncb
    for j in range(ncb):
        cols = slice(j * COLB, (j + 1) * COLB)
        zp = pool_project(j)
        ga_scr[:, cols] = jax.nn.sigmoid(proj(main, 3 * D + j * COLB))
        gb_scr[:, cols] = jax.nn.sigmoid(proj(main, 4 * D + j * COLB))
        pool_group(j, zp)
    for j in range(ncb):
        cols = slice(j * COLB, (j + 1) * COLB)
        gb_scr[:, cols] = gb_scr[:, cols] * _dot(p_scr[...], wmat(wb_ref, cols))
    for c in range(ts // CHUNK):
        sgu_chunk(c)

    finish_ranks()

    for j in range(ncb):
        cols = slice(j * COLB, (j + 1) * COLB)
        ta = _dot(a_scr[...], wmat(wa_ref, cols))
        m_scr[:, cols] = (ga_scr[:, cols] * ta + gb_scr[:, cols]).astype(BF16)
    xk_scr[...] = jnp.where(is_p, xp_ref[0], xs_ref[0])


def _const_spec(shape):
    nd = len(shape)
    return pl.BlockSpec(shape, lambda t: (0,) * nd, pipeline_mode=pl.Buffered(1))


def _mixer(x_p, x_s, mod, wts, alpha, ts):
    nbp, s_len, _ = x_p.shape
    nbs = x_s.shape[0]
    assert x_s.shape[1] == s_len and s_len % ts == 0 and ts % CHUNK == 0
    nb = nbp + nbs
    n_i = s_len // ts
    hb = ts // HALO
    n_hb = s_len // HALO

    n_tiles = nb * n_i

    def front(t):
        tf = jnp.minimum(t, n_tiles - 1)
        return tf // n_i, tf % n_i

    def back(t):
        return jnp.maximum(t - 1, 0)

    def pick(b, on_p, val, const):
        return jnp.where(b < nbp if on_p else b >= nbp, val, const)

    def main_map(on_p):
        def f(t):
            b, i = front(t)
            bb = pick(b, on_p, b if on_p else b - nbp, nbp - 1 if on_p else 0)
            ii = pick(b, on_p, i, n_i - 1 if on_p else 0)
            return (bb, ii, 0)
        return f

    def halo_map(on_p, nxt):
        def f(t):
            b, i = front(t)
            bb = pick(b, on_p, b if on_p else b - nbp, nbp - 1 if on_p else 0)
            idx = jnp.minimum((i + 1) * hb, n_hb - 1) if nxt else jnp.maximum(i * hb - 1, 0)
            ii = pick(b, on_p, idx, n_hb - 1 if on_p else 0)
            return (bb, ii, 0)
        return f

    tri = (lax.broadcasted_iota(jnp.int32, (ts, ts), 1)
           < lax.broadcasted_iota(jnp.int32, (ts, ts), 0)).astype(BF16)
    consts = list(wts) + [tri]
    in_specs = [
        pl.BlockSpec((1, ts, D), main_map(True)),
        pl.BlockSpec((1, ts, D), main_map(False)),
        pl.BlockSpec((1, HALO, D), halo_map(True, False)),
        pl.BlockSpec((1, HALO, D), halo_map(False, False)),
        pl.BlockSpec((1, HALO, D), halo_map(True, True)),
        pl.BlockSpec((1, HALO, D), halo_map(False, True)),
        pl.BlockSpec((1, N_MOD, D), lambda t: (front(t)[0], 0, 0)),
        pl.BlockSpec((1, N_MOD, D), lambda t: (back(t) // n_i, 0, 0)),
        pl.BlockSpec((ts, ROUTER_LANES), lambda t: (front(t)[1], 0)),
    ] + [_const_spec(w.shape) for w in consts]

    pos = jnp.arange(s_len, dtype=jnp.int32)[:, None]
    win = jnp.asarray(POOL_WINDOWS + (1,) * (ROUTER_LANES - len(POOL_WINDOWS)), jnp.int32)[None, :]
    inv_count = 1.0 / (jnp.minimum(pos + (win - 1 - win // 2), s_len - 1)
                       - jnp.maximum(pos - win // 2, 0) + 1).astype(F32)
    n_tok = nb * s_len
    out_shape = (
        jax.ShapeDtypeStruct((nb, s_len, D), F32),
        jax.ShapeDtypeStruct((n_tok, ROW_W), U32),
        jax.ShapeDtypeStruct((8, n_tok), F32),
        jax.ShapeDtypeStruct((8, ROUTER_LANES), F32),
    )
    out_specs = (
        pl.BlockSpec((1, ts, D), lambda t: (back(t) // n_i, back(t) % n_i, 0)),
        pl.BlockSpec((ts, ROW_W), lambda t: (back(t), 0)),
        pl.BlockSpec((8, ts), lambda t: (0, back(t))),
        pl.BlockSpec((8, ROUTER_LANES), lambda t: (0, 0)),
    )
    scratch = [
        pltpu.VMEM((ts + 2 * HALO, D), BF16),
        pltpu.VMEM((ts, D), F32),
        pltpu.VMEM((ts, D), F32),
        pltpu.VMEM((ts, D), BF16),
        pltpu.VMEM((ts, D), BF16),
        pltpu.VMEM((ts, D), BF16),
        pltpu.VMEM((ts, D), BF16),
        pltpu.VMEM((ts, D), BF16),
        pltpu.VMEM((ts, D), F32),
        pltpu.VMEM((ts, D), F32),
        pltpu.VMEM((ts, D), F32),
        pltpu.VMEM((ts, D), F32),
        pltpu.VMEM((8, ROUTER_LANES), F32),
    ]
    return pl.pallas_call(
        functools.partial(_mixer_kernel, nbp, ts, s_len, n_i, alpha),
        grid=(n_tiles + 1,),
        in_specs=in_specs,
        out_specs=out_specs,
        out_shape=out_shape,
        scratch_shapes=scratch,
        compiler_params=pltpu.CompilerParams(
            dimension_semantics=("arbitrary",), vmem_limit_bytes=VMEM_LIMIT),
        name="mixer",
    )(x_p, x_s, x_p, x_s, x_p, x_s, mod, mod, inv_count, *consts)


def _dest_kernel(seg_ref, cls_ref, rank_ref, o_ref):
    cls = cls_ref[...]
    acc = rank_ref[...].astype(jnp.int32)
    for c in range(N_CLASSES):
        acc = acc + jnp.where(cls == float(c), seg_ref[c], 0)
    o_ref[...] = acc


def _dest(seg_start, cls, rank):
    shape = cls.shape
    return pl.pallas_call(
        _dest_kernel,
        grid_spec=pltpu.PrefetchScalarGridSpec(
            num_scalar_prefetch=1,
            grid=(1,),
            in_specs=[pl.BlockSpec(shape, lambda i, s: (0, 0)), pl.BlockSpec(shape, lambda i, s: (0, 0))],
            out_specs=pl.BlockSpec(shape, lambda i, s: (0, 0)),
        ),
        out_shape=jax.ShapeDtypeStruct(shape, jnp.int32),
        name="dest",
    )(seg_start, cls, rank)


def _sc_mesh():
    return plsc.VectorSubcoreMesh(core_axis_name="core", subcore_axis_name="subcore")


def _sc_scatter_rows(rows, dest, n_rows, win):
    n_tok, row_w = rows.shape
    mesh = _sc_mesh()
    n_workers = mesh.num_cores * mesh.num_subcores
    n_steps = n_tok // win
    assert n_tok % win == 0 and n_steps % n_workers == 0
    per_worker = n_steps // n_workers

    @pl.kernel(out_type=jax.ShapeDtypeStruct((n_rows, row_w), rows.dtype), mesh=mesh,
               scratch_types=[pltpu.VMEM((win,), jnp.int32), pltpu.VMEM((win, row_w), rows.dtype)])
    def scatter(rows_hbm, idx_hbm, out_hbm, idx_vmem, buf):
        worker = lax.axis_index("core") * mesh.num_subcores + lax.axis_index("subcore")

        @pl.loop(0, per_worker)
        def _(k):
            step = worker * per_worker + k
            pltpu.sync_copy(idx_hbm.at[step], idx_vmem)
            pltpu.sync_copy(rows_hbm.at[pl.ds(step * win, win)], buf)
            pltpu.sync_copy(buf, out_hbm.at[idx_vmem])

    return scatter(rows, dest.reshape(n_steps, win))


def _sc_gather_rows(table, idx, win):
    n_tok = idx.shape[0]
    row_w = table.shape[1]
    mesh = _sc_mesh()
    n_workers = mesh.num_cores * mesh.num_subcores
    n_steps = n_tok // win
    assert n_tok % win == 0 and n_steps % n_workers == 0
    per_worker = n_steps // n_workers

    @pl.kernel(out_type=jax.ShapeDtypeStruct((n_tok, row_w), table.dtype), mesh=mesh,
               scratch_types=[pltpu.VMEM((win,), jnp.int32), pltpu.VMEM((win, row_w), table.dtype)])
    def gather(table_hbm, idx_hbm, out_hbm, idx_vmem, buf):
        worker = lax.axis_index("core") * mesh.num_subcores + lax.axis_index("subcore")

        @pl.loop(0, per_worker)
        def _(k):
            step = worker * per_worker + k
            pltpu.sync_copy(idx_hbm.at[step], idx_vmem)
            pltpu.sync_copy(table_hbm.at[idx_vmem], buf)
            pltpu.sync_copy(buf, out_hbm.at[pl.ds(step * win, win)])

    return gather(table, idx.reshape(n_steps, win))


def _sc_pack_rows(w, chunk):
    *lead, k, n = w.shape
    rows = math.prod(lead) * k
    mesh = _sc_mesh()
    lanes = SC_LANES
    n_workers = mesh.num_cores * mesh.num_subcores
    assert k % 2 == 0 and rows % (n_workers * chunk) == 0 and chunk % 2 == 0 and n % (lanes * SC_PACK_UNROLL) == 0
    per_worker = rows // (n_workers * chunk)

    def to_bf16_bits(x):
        bits = lax.bitcast_convert_type(x, U32)
        return (bits + jnp.uint32(0x7FFF) + ((bits >> 16) & jnp.uint32(1))) >> 16

    @pl.kernel(out_type=jax.ShapeDtypeStruct((rows // 2, n), U32), mesh=mesh,
               scratch_types=[pltpu.VMEM((chunk, n), F32), pltpu.VMEM((chunk // 2, n), U32)],
               compiler_params=pltpu.CompilerParams(needs_layout_passes=False))
    def pack(w_hbm, out_hbm, src, dst):
        worker = lax.axis_index("core") * mesh.num_subcores + lax.axis_index("subcore")

        @pl.loop(0, per_worker)
        def _(step):
            blk = worker * per_worker + step
            row0 = pl.multiple_of(blk * chunk, chunk)
            out_row0 = pl.multiple_of(blk * (chunk // 2), chunk // 2)
            pltpu.sync_copy(w_hbm.at[pl.ds(row0, chunk)], src)

            @pl.loop(0, chunk // 2)
            def _(q):
                @pl.loop(0, n, step=lanes * SC_PACK_UNROLL)
                def _(c0):
                    for u in range(SC_PACK_UNROLL):
                        c = c0 + u * lanes
                        lo = to_bf16_bits(src[2 * q, pl.ds(c, lanes)])
                        hi = to_bf16_bits(src[2 * q + 1, pl.ds(c, lanes)])
                        dst[q, pl.ds(c, lanes)] = lo | (hi << 16)

            pltpu.sync_copy(dst, out_hbm.at[pl.ds(out_row0, chunk // 2)])

    return pack(w.reshape(rows, n)).reshape(*lead, k // 2, n)


def _expert_kernel(bm, ea_ref, eb_ref, nused_ref, xs_ref, wga_ref, wua_ref, wda_ref, wgb_ref, wub_ref, wdb_ref,
                   ys_ref, act_scr, y_scr):
    j = pl.program_id(0)
    n_used = nused_ref[0]
    half = D // 2

    def wmat(ref, k0, k1, cols):
        return pltpu.bitcast(ref[0, k0 // 2:k1 // 2, cols], BF16)

    def store_previous():
        for c0 in range(0, half, COLB):
            lo_bits = lax.bitcast_convert_type(y_scr[:, c0:c0 + COLB].astype(BF16).astype(F32), U32)
            hi_bits = lax.bitcast_convert_type(y_scr[:, half + c0:half + c0 + COLB].astype(BF16).astype(F32), U32)
            ys_ref[:, c0:c0 + COLB] = (lo_bits >> 16) | (hi_bits & jnp.uint32(0xFFFF0000))

    def compute():
        bits = xs_ref[:, 0:half]
        lo = lax.bitcast_convert_type(bits << 16, F32).astype(BF16)
        hi = lax.bitcast_convert_type(bits & jnp.uint32(0xFFFF0000), F32).astype(BF16)
        wts = lax.bitcast_convert_type(xs_ref[:, half:half + ROUTER_LANES], F32)
        for e, (wg_ref, wu_ref) in enumerate(((wga_ref, wua_ref), (wgb_ref, wub_ref))):
            for c in range(D_EXPERT // COLB):
                cols = slice(c * COLB, (c + 1) * COLB)
                g = _dot(lo, wmat(wg_ref, 0, half, cols)) + _dot(hi, wmat(wg_ref, half, D, cols))
                up = _dot(lo, wmat(wu_ref, 0, half, cols)) + _dot(hi, wmat(wu_ref, half, D, cols))
                act_scr[e, :, cols] = (jax.nn.silu(g) * up).astype(BF16)
        w_a = wts[:, 0:1]
        w_b = wts[:, 1:2]

        for c0 in range(0, D, COLB):
            cols = slice(c0, c0 + COLB)
            y_scr[:, cols] = (w_a * _dot(act_scr[0], wmat(wda_ref, 0, D_EXPERT, cols))
                              + w_b * _dot(act_scr[1], wmat(wdb_ref, 0, D_EXPERT, cols)))

    @pl.when(j == 0)
    def _():
        compute()

    @pl.when((j > 0) & (j < n_used))
    def _():
        store_previous()
        compute()

    @pl.when(j == n_used)
    def _():
        store_previous()


def _experts(xs, blk_a, blk_b, n_used, w_gate, w_up, w_down, bm):
    n_rows = xs.shape[0]
    n_blocks = n_rows // bm

    def x_map(j, ea, eb, nu):
        return (jnp.minimum(j, nu[0] - 1), 0)

    def y_map(j, ea, eb, nu):
        return (jnp.clip(j - 1, 0, nu[0] - 1), 0)

    def a_map(j, ea, eb, nu):
        return (ea[j], 0, 0)

    def b_map(j, ea, eb, nu):
        return (eb[j], 0, 0)

    return pl.pallas_call(
        functools.partial(_expert_kernel, bm),
        grid_spec=pltpu.PrefetchScalarGridSpec(
            num_scalar_prefetch=3,
            grid=(n_blocks,),
            in_specs=[pl.BlockSpec((bm, ROW_W), x_map),
                      pl.BlockSpec((1, D // 2, D_EXPERT), a_map),
                      pl.BlockSpec((1, D // 2, D_EXPERT), a_map),
                      pl.BlockSpec((1, D_EXPERT // 2, D), a_map),
                      pl.BlockSpec((1, D // 2, D_EXPERT), b_map),
                      pl.BlockSpec((1, D // 2, D_EXPERT), b_map),
                      pl.BlockSpec((1, D_EXPERT // 2, D), b_map)],
            out_specs=pl.BlockSpec((bm, D // 2), y_map),
            scratch_shapes=[pltpu.VMEM((2, bm, D_EXPERT), BF16), pltpu.VMEM((bm, D), F32)],
        ),
        out_shape=jax.ShapeDtypeStruct((n_rows, D // 2), U32),
        compiler_params=pltpu.CompilerParams(
            dimension_semantics=("arbitrary",), vmem_limit_bytes=VMEM_LIMIT),
        name="experts",
    )(blk_a, blk_b, n_used, xs, w_gate, w_up, w_down, w_gate, w_up, w_down)


def _final_kernel(alpha, x1_ref, f_ref, mod_ref, l2g_ref, l2b_ref, out_ref):
    half = D // 2
    gate2 = mod_ref[0][5:6]
    bits = f_ref[...]
    f_lo = lax.bitcast_convert_type(bits << 16, F32)
    f_hi = lax.bitcast_convert_type(bits & jnp.uint32(0xFFFF0000), F32)
    y_lo = alpha * x1_ref[0, :, 0:half] + gate2[:, 0:half] * f_lo
    y_hi = alpha * x1_ref[0, :, half:D] + gate2[:, half:D] * f_hi
    mu = (jnp.sum(y_lo, axis=-1, keepdims=True) + jnp.sum(y_hi, axis=-1, keepdims=True)) * (1.0 / D)
    c_lo = y_lo - mu
    c_hi = y_hi - mu
    var = (jnp.sum(c_lo * c_lo, axis=-1, keepdims=True) + jnp.sum(c_hi * c_hi, axis=-1, keepdims=True)) * (1.0 / D)
    rs = lax.rsqrt(var + LN_EPS)
    out_ref[0, :, 0:half] = c_lo * rs * l2g_ref[:, 0:half] + l2b_ref[:, 0:half]
    out_ref[0, :, half:D] = c_hi * rs * l2g_ref[:, half:D] + l2b_ref[:, half:D]


def _final(x1, f, mod, ln2_g, ln2_b, b_off, nbg, alpha, tk):
    s_len = x1.shape[1]
    n_i = s_len // tk
    return pl.pallas_call(
        functools.partial(_final_kernel, alpha),
        grid=(nbg, n_i),
        in_specs=[pl.BlockSpec((1, tk, D), lambda b, i: (b + b_off, i, 0)),
                  pl.BlockSpec((tk, D // 2), lambda b, i: (b * n_i + i, 0)),
                  pl.BlockSpec((1, N_MOD, D), lambda b, i: (b + b_off, 0, 0)),
                  pl.BlockSpec((1, D), lambda b, i: (0, 0)),
                  pl.BlockSpec((1, D), lambda b, i: (0, 0))],
        out_specs=pl.BlockSpec((1, tk, D), lambda b, i: (b, i, 0)),
        out_shape=jax.ShapeDtypeStruct((nbg, s_len, D), F32),
        compiler_params=pltpu.CompilerParams(
            dimension_semantics=("arbitrary", "arbitrary"), vmem_limit_bytes=VMEM_LIMIT),
        name="final",
    )(x1, f, mod, ln2_g, ln2_b)


def _class_experts():
    ea, eb = [], []
    for g in range(N_GROUPS):
        for a in range(EPG):
            for b in range(a + 1, EPG):
                ea.append(g * EPG + a)
                eb.append(g * EPG + b)
    return np.asarray(ea, np.int32), np.asarray(eb, np.int32)


def _layer(x_p, x_s, c_all, p, alpha):
    nbp, s_len, _ = x_p.shape
    nbs = x_s.shape[0]
    nb = nbp + nbs
    n_tok = nb * s_len

    mod = _modulation(c_all, p["w_mod"], p["b_mod"]).reshape(nb, N_MOD, D)

    bsf = jnp.repeat(p["b_spatial"].T, HEAD_DIM, axis=1)
    n_in = p["w_in"].shape[1]
    w_router = jnp.concatenate(
        [p["w_router_group"], p["w_router_expert"],
         jnp.zeros((D, ROUTER_LANES - N_GROUPS - N_EXPERTS), F32)], axis=1)
    b_router = jnp.concatenate(
        [p["b_router_group"], p["b_router_expert"],
         jnp.zeros((ROUTER_LANES - N_GROUPS - N_EXPERTS,), F32)]).reshape(1, ROUTER_LANES)
    wts = [
        _pack_rows(p["w_in"]), p["b_in"].reshape(1, n_in),
        p["w_spatial"].astype(BF16), bsf,
        p["sgu_g"].reshape(1, D), p["sgu_b"].reshape(1, D),
        _pack_rows(p["w_pool"]), p["pool_scale"].reshape(1, D),
        _pack_rows(p["w_branch_a"]), _pack_rows(p["w_branch_b"]), _pack_rows(p["w_out"]),
        p["ln1_g"].reshape(1, D), p["ln1_b"].reshape(1, D),
        _pack_rows(w_router), b_router,
    ]
    expert_w = [_sc_pack_rows(p[name], SC_PACK_CHUNK) for name in ("w_exp_gate", "w_exp_up", "w_exp_down")]
    x1, rows, cr, cnt = _mixer(x_p, x_s, mod, wts, alpha, MIXER_TS)

    bm = EXPERT_BM
    assert n_tok % bm == 0 and n_tok % DEST_LANES == 0
    counts = cnt[0, :N_CLASSES].astype(jnp.int32)
    padded = (counts + bm - 1) // bm * bm
    seg_end = jnp.cumsum(padded).astype(jnp.int32)
    seg_start = seg_end - padded
    n_blocks = n_tok // bm + N_CLASSES
    block_start = jnp.arange(n_blocks, dtype=jnp.int32) * bm
    blk_cls = jnp.minimum(
        jnp.sum((seg_end[None, :] <= block_start[:, None]).astype(jnp.int32), axis=1), N_CLASSES - 1)
    cls_a, cls_b = _class_experts()
    blk_a = jnp.asarray(cls_a)[blk_cls]
    blk_b = jnp.asarray(cls_b)[blk_cls]
    n_used = (seg_end[-1:] // bm).astype(jnp.int32)

    lane_shape = (n_tok // DEST_LANES, DEST_LANES)
    dest = _dest(seg_start, cr[0].reshape(lane_shape), cr[1].reshape(lane_shape)).reshape(n_tok)

    xs = _sc_scatter_rows(rows, dest, n_blocks * bm, SC_WIN)
    ys = _experts(xs, blk_a, blk_b, n_used, *expert_w, bm)

    l2g = p["ln2_g"].reshape(1, D)
    l2b = p["ln2_b"].reshape(1, D)
    tp = nbp * s_len
    f_p = _sc_gather_rows(ys, dest[:tp], SC_WIN)
    f_s = _sc_gather_rows(ys, dest[tp:], SC_WIN)
    y_p = _final(x1, f_p, mod, l2g, l2b, 0, nbp, alpha, FINAL_TK)
    y_s = _final(x1, f_s, mod, l2g, l2b, nbp, nbs, alpha, FINAL_TK)
    return y_p, y_s


_PARAM_NAMES = ("w_mod", "b_mod", "w_in", "b_in", "w_spatial", "b_spatial", "sgu_g", "sgu_b", "w_pool",
                "pool_scale", "w_branch_a", "w_branch_b", "w_out", "ln1_g", "ln1_b", "w_router_group",
                "b_router_group", "w_router_expert", "b_router_expert", "w_exp_gate", "w_exp_up",
                "w_exp_down", "ln2_g", "ln2_b")


def kernel(x_prompt, x_sample, c_prompt, c_sample, w_mod, b_mod, w_in, b_in, w_spatial, b_spatial, sgu_g, sgu_b, w_pool, pool_scale, w_branch_a, w_branch_b, w_out, ln1_g, ln1_b, w_router_group, b_router_group, w_router_expert, b_router_expert, w_exp_gate, w_exp_up, w_exp_down, ln2_g, ln2_b):
    params = (w_mod, b_mod, w_in, b_in, w_spatial, b_spatial, sgu_g, sgu_b, w_pool, pool_scale,
              w_branch_a, w_branch_b, w_out, ln1_g, ln1_b, w_router_group, b_router_group,
              w_router_expert, b_router_expert, w_exp_gate, w_exp_up, w_exp_down, ln2_g, ln2_b)
    depth = w_mod.shape[0]
    alpha = (2.0 * depth) ** 0.25
    c_all = jnp.concatenate([c_prompt, c_sample], axis=0)
    y_p, y_s = x_prompt, x_sample
    for l in range(depth):
        p = {name: w[l] for name, w in zip(_PARAM_NAMES, params)}
        y_p, y_s = _layer(y_p, y_s, c_all, p, alpha)
    return (y_p, y_s)
```

```python
import functools
import math

import jax
import jax.numpy as jnp
import numpy as np
from jax import lax
from jax.experimental import pallas as pl
from jax.experimental.pallas import tpu as pltpu
from jax.experimental.pallas import tpu_sc as plsc

F32 = jnp.float32
BF16 = jnp.bfloat16
U32 = jnp.uint32

D = 1024
CHUNK = 128
SGU_HEADS = 8
HEAD_DIM = D // SGU_HEADS
POOL_WINDOWS = (2, 4, 8, 16)
POOL_GROUP_DIM = D // len(POOL_WINDOWS)
N_MOD = 6
N_GROUPS = 4
EPG = 8
N_EXPERTS = N_GROUPS * EPG
D_EXPERT = D // 2
LN_EPS = 1e-5

HALO = 16
COLB = 256
ROUTER_LANES = 128
EXPERT_LANE0 = N_GROUPS
PAIRS = EPG * (EPG - 1) // 2
N_CLASSES = N_GROUPS * PAIRS
ROW_W = D // 2 + ROUTER_LANES
MIXER_TS = 512
EXPERT_BM = 512
FINAL_TK = 2048
DEST_LANES = 512
PACK_BLOCK = 1024
SC_WIN = 128
SC_LANES = 16
SC_PACK_CHUNK = 32
SC_PACK_UNROLL = 8
VMEM_LIMIT = 56 * 1024 * 1024


def _ln(x):
    mu = jnp.mean(x, axis=-1, keepdims=True)
    xc = x - mu
    var = jnp.mean(xc * xc, axis=-1, keepdims=True)
    return xc * lax.rsqrt(var + LN_EPS)


_GELU_A = -2.0 * math.sqrt(2.0 / math.pi) * math.log2(math.e)
_GELU_B = _GELU_A * 0.044715


def _gelu_tanh(x):
    return x / (1.0 + jnp.exp2(x * (_GELU_A + _GELU_B * (x * x))))


def _dot(a, b):
    return jnp.dot(a, b, preferred_element_type=F32)


def _mod_kernel(c_ref, w_ref, b_ref, o_ref):
    a = jax.nn.silu(c_ref[...]).astype(BF16)
    o_ref[...] = _dot(a, w_ref[...].astype(BF16)) + b_ref[...]


def _modulation(c_all, w_mod, b_mod):
    nb = c_all.shape[0]
    n_out = w_mod.shape[1]
    cb = 512
    return pl.pallas_call(
        _mod_kernel,
        grid=(n_out // cb,),
        in_specs=[pl.BlockSpec((nb, D), lambda j: (0, 0)),
                  pl.BlockSpec((D, cb), lambda j: (0, j)),
                  pl.BlockSpec((1, cb), lambda j: (0, j))],
        out_specs=pl.BlockSpec((nb, cb), lambda j: (0, j)),
        out_shape=jax.ShapeDtypeStruct((nb, n_out), F32),
        name="mod",
    )(c_all, w_mod, b_mod.reshape(1, n_out))


def _pack_kernel(w_ref, o_ref):
    o_ref[...] = pltpu.bitcast(w_ref[...].astype(BF16), U32)


def _pack_rows(w):
    *lead, k, n = w.shape
    rows = math.prod(lead) * k
    rb = min(rows, PACK_BLOCK)
    nb = min(n, PACK_BLOCK)
    assert k % 2 == 0 and rows % rb == 0 and n % nb == 0
    packed = pl.pallas_call(
        _pack_kernel,
        grid=(rows // rb, n // nb),
        in_specs=[pl.BlockSpec((rb, nb), lambda i, j: (i, j))],
        out_specs=pl.BlockSpec((rb // 2, nb), lambda i, j: (i, j)),
        out_shape=jax.ShapeDtypeStruct((rows // 2, n), U32),
        name="pack",
    )(w.reshape(rows, n))
    return packed.reshape(*lead, k // 2, n)


def _mixer_kernel(nbp, ts, s_len, n_i, alpha,
                  xp_ref, xs_ref, xpp_ref, xsp_ref, xpn_ref, xsn_ref, mod_ref, modb_ref, icnt_ref,
                  win_ref, bin_ref, ws_ref, bsf_ref, sg_ref, sb_ref, wpool_ref, psc_ref,
                  wa_ref, wb_ref, wo_ref, l1g_ref, l1b_ref, wr_ref, br_ref, tri_ref,
                  x1_ref, h2u_ref, cr_ref, cnt_ref,
                  h_scr, u_scr, v_scr, vb_scr, a_scr, p_scr, m_scr, h2_scr, y_scr, xk_scr, ga_scr, gb_scr,
                  carry_scr):
    t = pl.program_id(0)
    n_tiles = pl.num_programs(0) - 1
    tf = jnp.minimum(t, n_tiles - 1)
    b = tf // n_i
    i = tf % n_i
    is_p = b < nbp
    ncb = D // COLB
    n_ext = ts + 2 * HALO
    half = D // 2
    main = slice(HALO, HALO + ts)

    def wmat(ref, cols=slice(None)):
        return pltpu.bitcast(ref[:, cols], BF16)

    @pl.when(t == 0)
    def _():
        carry_scr[...] = jnp.zeros_like(carry_scr)
        m_scr[...] = jnp.zeros_like(m_scr)
        xk_scr[...] = jnp.zeros_like(xk_scr)

    def row_rstd(read, mu):
        ss = jnp.zeros((ts, 1), F32)
        for j in range(ncb):
            xc = read(slice(j * COLB, (j + 1) * COLB)) - mu
            ss = ss + jnp.sum(xc * xc, axis=-1, keepdims=True)
        return lax.rsqrt(ss * (1.0 / D) + LN_EPS)

    modb = modb_ref[0]
    gate1_b, shift2, scale2 = modb[2:3], modb[3:4], modb[4:5]
    ysum = jnp.zeros((ts, 1), F32)
    for j in range(ncb):
        cols = slice(j * COLB, (j + 1) * COLB)
        tmix = _dot(m_scr[...], wmat(wo_ref, cols))
        y = alpha * xk_scr[:, cols] + gate1_b[:, cols] * tmix
        y_scr[:, cols] = y
        ysum = ysum + jnp.sum(y, axis=-1, keepdims=True)

    mod = mod_ref[0]
    shift1, scale1 = mod[0:1], mod[1:2]

    def adaln1(xv):
        return (_ln(xv) * (1.0 + scale1) + shift1).astype(BF16)

    h_scr[0:HALO, :] = adaln1(jnp.where(is_p, xpp_ref[0], xsp_ref[0]))
    h_scr[main, :] = adaln1(jnp.where(is_p, xp_ref[0], xs_ref[0]))
    h_scr[HALO + ts:, :] = adaln1(jnp.where(is_p, xpn_ref[0], xsn_ref[0]))

    def proj(hrows, c0):
        return _dot(h_scr[hrows, :], wmat(win_ref, slice(c0, c0 + COLB))) + bin_ref[:, c0:c0 + COLB]

    ymu = ysum * (1.0 / D)
    yrs = row_rstd(lambda cols: y_scr[:, cols], ymu)
    xsum = jnp.zeros((ts, 1), F32)
    for j in range(ncb):
        cols = slice(j * COLB, (j + 1) * COLB)
        x1 = (y_scr[:, cols] - ymu) * yrs * l1g_ref[:, cols] + l1b_ref[:, cols]
        x1_ref[0, :, cols] = x1
        xsum = xsum + jnp.sum(x1, axis=-1, keepdims=True)
    xmu = xsum * (1.0 / D)
    xrs = row_rstd(lambda cols: x1_ref[0, :, cols], xmu)
    for j in range(ncb):
        cols = slice(j * COLB, (j + 1) * COLB)
        h2 = (x1_ref[0, :, cols] - xmu) * xrs * (1.0 + scale2[:, cols]) + shift2[:, cols]
        h2_scr[:, cols] = h2.astype(BF16)
    lo_bits = lax.bitcast_convert_type(h2_scr[:, 0:half].astype(F32), U32)
    hi_bits = lax.bitcast_convert_type(h2_scr[:, half:D].astype(F32), U32)
    h2u_ref[:, 0:half] = (lo_bits >> 16) | (hi_bits & jnp.uint32(0xFFFF0000))

    vsum = jnp.zeros((ts, 1), F32)
    for j in range(ncb):
        cols = slice(j * COLB, (j + 1) * COLB)
        u_scr[:, cols] = _gelu_tanh(proj(main, j * COLB))
        gv = _gelu_tanh(proj(main, D + j * COLB))
        v_scr[:, cols] = gv
        vsum = vsum + jnp.sum(gv, axis=-1, keepdims=True)

    logits = _dot(h2_scr[...], wmat(wr_ref)) + br_ref[...]
    lane = lax.broadcasted_iota(jnp.int32, (ts, ROUTER_LANES), 1)
    lane_f = lane.astype(F32)
    neg = -jnp.inf
    is_g = lane < N_GROUPS
    lg = jnp.where(is_g, logits, neg)
    mg = jnp.max(lg, axis=-1, keepdims=True)
    g_sel = jnp.min(jnp.where(lg == mg, lane_f, float(ROUTER_LANES)), axis=-1, keepdims=True)
    p_sel = 1.0 / jnp.sum(jnp.where(is_g, jnp.exp(logits - mg), 0.0), axis=-1, keepdims=True)
    e_lo = EXPERT_LANE0 + g_sel * EPG
    in_grp = (lane_f >= e_lo) & (lane_f < e_lo + EPG)
    le = jnp.where(in_grp, logits, neg)
    m1 = jnp.max(le, axis=-1, keepdims=True)
    i1 = jnp.min(jnp.where(le == m1, lane_f, float(ROUTER_LANES)), axis=-1, keepdims=True)
    le2 = jnp.where(lane_f == i1, neg, le)
    m2 = jnp.max(le2, axis=-1, keepdims=True)
    i2 = jnp.min(jnp.where(le2 == m2, lane_f, float(ROUTER_LANES)), axis=-1, keepdims=True)
    e2x = jnp.exp(m2 - m1)
    den = 1.0 + e2x
    g1 = p_sel / den
    g2 = p_sel * e2x / den

    j1 = i1 - e_lo
    j2 = i2 - e_lo
    first_is_a = j1 < j2
    ja = jnp.minimum(j1, j2)
    jb = jnp.maximum(j1, j2)
    cls = g_sel * PAIRS + (EPG - 1) * ja - ja * (ja - 1.0) * 0.5 + (jb - ja - 1.0)
    w_a = jnp.where(first_is_a, g1, g2)
    w_b = jnp.where(first_is_a, g2, g1)
    extra = jnp.where(lane == 0, w_a, jnp.where(lane == 1, w_b, 0.0))
    h2u_ref[:, half:half + ROUTER_LANES] = lax.bitcast_convert_type(extra, U32)

    hit = lane_f == cls
    onehot = jnp.where(hit, (t > 0).astype(F32), 0.0)

    def finish_ranks():
        pre = _dot(tri_ref[...], onehot.astype(BF16)) + carry_scr[0:1, :]
        rank = jnp.sum(jnp.where(hit, pre, 0.0), axis=-1, keepdims=True)
        carry_scr[0:1, :] = carry_scr[0:1, :] + jnp.sum(onehot, axis=0, keepdims=True)
        cnt_ref[...] = carry_scr[...]
        cr = jnp.where(lane == 0, cls, jnp.where(lane == 1, rank, 0.0))
        cr_ref[...] = cr.T[0:8, :]

    vmu = vsum * (1.0 / D)
    vrs = row_rstd(lambda cols: v_scr[:, cols], vmu)
    for j in range(ncb):
        cols = slice(j * COLB, (j + 1) * COLB)
        vb_scr[:, cols] = ((v_scr[:, cols] - vmu) * vrs * sg_ref[:, cols] + sb_ref[:, cols]).astype(BF16)

    def pool_project(gi):
        ext_pos = lax.broadcasted_iota(jnp.int32, (n_ext, 1), 0) + (i * ts - HALO)
        ext_valid = (ext_pos >= 0) & (ext_pos < s_len)
        return jnp.where(ext_valid, proj(slice(None), 2 * D + gi * POOL_GROUP_DIM), 0.0)

    def pool_group(gi, zp):
        w = POOL_WINDOWS[gi]
        cols = slice(gi * POOL_GROUP_DIM, (gi + 1) * POOL_GROUP_DIM)
        acc = zp + pltpu.roll(zp, 1, 0)
        if w >= 4:
            acc = pltpu.roll(acc, 1, 0) + pltpu.roll(acc, n_ext - 1, 0)
        if w >= 8:
            acc = pltpu.roll(acc, 2, 0) + pltpu.roll(acc, n_ext - 2, 0)
        if w >= 16:
            acc = pltpu.roll(acc, 4, 0) + pltpu.roll(acc, n_ext - 4, 0)
        inv_count = icnt_ref[:, gi:gi + 1]
        dd = (acc[main] * inv_count - zp[main]).astype(BF16)
        p_scr[:, cols] = (_dot(dd, pltpu.bitcast(wpool_ref[gi], BF16)) * psc_ref[:, cols]).astype(BF16)

    def sgu_chunk(c):
        crow = slice(c * CHUNK, (c + 1) * CHUNK)
        for hh in range(SGU_HEADS):
            cols = slice(hh * HEAD_DIM, (hh + 1) * HEAD_DIM)
            mixed = _dot(ws_ref[hh], vb_scr[crow, cols]) + bsf_ref[:, cols]
            a_scr[crow, cols] = (u_scr[crow, cols] * mixed).astype(BF16)

    assert len(POOL_WINDOWS) == ncb
    for j in range(ncb):
        cols = slice(j * COLB, (j + 1) * COLB)
        zp = pool_project(j)
        ga_scr[:, cols] = jax.nn.sigmoid(proj(main, 3 * D + j * COLB))
        gb_scr[:, cols] = jax.nn.sigmoid(proj(main, 4 * D + j * COLB))
        pool_group(j, zp)
    for j in range(ncb):
        cols = slice(j * COLB, (j + 1) * COLB)
        gb_scr[:, cols] = gb_scr[:, cols] * _dot(p_scr[...], wmat(wb_ref, cols))
    for c in range(ts // CHUNK):
        sgu_chunk(c)

    finish_ranks()

    for j in range(ncb):
        cols = slice(j * COLB, (j + 1) * COLB)
        ta = _dot(a_scr[...], wmat(wa_ref, cols))
        m_scr[:, cols] = (ga_scr[:, cols] * ta + gb_scr[:, cols]).astype(BF16)
    xk_scr[...] = jnp.where(is_p, xp_ref[0], xs_ref[0])


def _const_spec(shape):
    nd = len(shape)
    return pl.BlockSpec(shape, lambda t: (0,) * nd, pipeline_mode=pl.Buffered(1))


def _mixer(x_p, x_s, mod, wts, alpha, ts):
    nbp, s_len, _ = x_p.shape
    nbs = x_s.shape[0]
    assert x_s.shape[1] == s_len and s_len % ts == 0 and ts % CHUNK == 0
    nb = nbp + nbs
    n_i = s_len // ts
    hb = ts // HALO
    n_hb = s_len // HALO

    n_tiles = nb * n_i

    def front(t):
        tf = jnp.minimum(t, n_tiles - 1)
        return tf // n_i, tf % n_i

    def back(t):
        return jnp.maximum(t - 1, 0)

    def pick(b, on_p, val, const):
        return jnp.where(b < nbp if on_p else b >= nbp, val, const)

    def main_map(on_p):
        def f(t):
            b, i = front(t)
            bb = pick(b, on_p, b if on_p else b - nbp, nbp - 1 if on_p else 0)
            ii = pick(b, on_p, i, n_i - 1 if on_p else 0)
            return (bb, ii, 0)
        return f

    def halo_map(on_p, nxt):
        def f(t):
            b, i = front(t)
            bb = pick(b, on_p, b if on_p else b - nbp, nbp - 1 if on_p else 0)
            idx = jnp.minimum((i + 1) * hb, n_hb - 1) if nxt else jnp.maximum(i * hb - 1, 0)
            ii = pick(b, on_p, idx, n_hb - 1 if on_p else 0)
            return (bb, ii, 0)
        return f

    tri = (lax.broadcasted_iota(jnp.int32, (ts, ts), 1)
           < lax.broadcasted_iota(jnp.int32, (ts, ts), 0)).astype(BF16)
    consts = list(wts) + [tri]
    in_specs = [
        pl.BlockSpec((1, ts, D), main_map(True)),
        pl.BlockSpec((1, ts, D), main_map(False)),
        pl.BlockSpec((1, HALO, D), halo_map(True, False)),
        pl.BlockSpec((1, HALO, D), halo_map(False, False)),
        pl.BlockSpec((1, HALO, D), halo_map(True, True)),
        pl.BlockSpec((1, HALO, D), halo_map(False, True)),
        pl.BlockSpec((1, N_MOD, D), lambda t: (front(t)[0], 0, 0)),
        pl.BlockSpec((1, N_MOD, D), lambda t: (back(t) // n_i, 0, 0)),
        pl.BlockSpec((ts, ROUTER_LANES), lambda t: (front(t)[1], 0)),
    ] + [_const_spec(w.shape) for w in consts]

    pos = jnp.arange(s_len, dtype=jnp.int32)[:, None]
    win = jnp.asarray(POOL_WINDOWS + (1,) * (ROUTER_LANES - len(POOL_WINDOWS)), jnp.int32)[None, :]
    inv_count = 1.0 / (jnp.minimum(pos + (win - 1 - win // 2), s_len - 1)
                       - jnp.maximum(pos - win // 2, 0) + 1).astype(F32)
    n_tok = nb * s_len
    out_shape = (
        jax.ShapeDtypeStruct((nb, s_len, D), F32),
        jax.ShapeDtypeStruct((n_tok, ROW_W), U32),
        jax.ShapeDtypeStruct((8, n_tok), F32),
        jax.ShapeDtypeStruct((8, ROUTER_LANES), F32),
    )
    out_specs = (
        pl.BlockSpec((1, ts, D), lambda t: (back(t) // n_i, back(t) % n_i, 0)),
        pl.BlockSpec((ts, ROW_W), lambda t: (back(t), 0)),
        pl.BlockSpec((8, ts), lambda t: (0, back(t))),
        pl.BlockSpec((8, ROUTER_LANES), lambda t: (0, 0)),
    )
    scratch = [
        pltpu.VMEM((ts + 2 * HALO, D), BF16),
        pltpu.VMEM((ts, D), F32),
        pltpu.VMEM((ts, D), F32),
        pltpu.VMEM((ts, D), BF16),
        pltpu.VMEM((ts, D), BF16),
        pltpu.VMEM((ts, D), BF16),
        pltpu.VMEM((ts, D), BF16),
        pltpu.VMEM((ts, D), BF16),
        pltpu.VMEM((ts, D), F32),
        pltpu.VMEM((ts, D), F32),
        pltpu.VMEM((ts, D), F32),
        pltpu.VMEM((ts, D), F32),
        pltpu.VMEM((8, ROUTER_LANES), F32),
    ]
    return pl.pallas_call(
        functools.partial(_mixer_kernel, nbp, ts, s_len, n_i, alpha),
        grid=(n_tiles + 1,),
        in_specs=in_specs,
        out_specs=out_specs,
        out_shape=out_shape,
        scratch_shapes=scratch,
        compiler_params=pltpu.CompilerParams(
            dimension_semantics=("arbitrary",), vmem_limit_bytes=VMEM_LIMIT),
        name="mixer",
    )(x_p, x_s, x_p, x_s, x_p, x_s, mod, mod, inv_count, *consts)


def _dest_kernel(seg_ref, cls_ref, rank_ref, o_ref):
    cls = cls_ref[...]
    acc = rank_ref[...].astype(jnp.int32)
    for c in range(N_CLASSES):
        acc = acc + jnp.where(cls == float(c), seg_ref[c], 0)
    o_ref[...] = acc


def _dest(seg_start, cls, rank):
    shape = cls.shape
    return pl.pallas_call(
        _dest_kernel,
        grid_spec=pltpu.PrefetchScalarGridSpec(
            num_scalar_prefetch=1,
            grid=(1,),
            in_specs=[pl.BlockSpec(shape, lambda i, s: (0, 0)), pl.BlockSpec(shape, lambda i, s: (0, 0))],
            out_specs=pl.BlockSpec(shape, lambda i, s: (0, 0)),
        ),
        out_shape=jax.ShapeDtypeStruct(shape, jnp.int32),
        name="dest",
    )(seg_start, cls, rank)


def _sc_mesh():
    return plsc.VectorSubcoreMesh(core_axis_name="core", subcore_axis_name="subcore")


def _sc_scatter_rows(rows, dest, n_rows, win):
    n_tok, row_w = rows.shape
    mesh = _sc_mesh()
    n_workers = mesh.num_cores * mesh.num_subcores
    n_steps = n_tok // win
    assert n_tok % win == 0 and n_steps % n_workers == 0
    per_worker = n_steps // n_workers

    @pl.kernel(out_type=jax.ShapeDtypeStruct((n_rows, row_w), rows.dtype), mesh=mesh,
               scratch_types=[pltpu.VMEM((win,), jnp.int32), pltpu.VMEM((win, row_w), rows.dtype)])
    def scatter(rows_hbm, idx_hbm, out_hbm, idx_vmem, buf):
        worker = lax.axis_index("core") * mesh.num_subcores + lax.axis_index("subcore")

        @pl.loop(0, per_worker)
        def _(k):
            step = worker * per_worker + k
            pltpu.sync_copy(idx_hbm.at[step], idx_vmem)
            pltpu.sync_copy(rows_hbm.at[pl.ds(step * win, win)], buf)
            pltpu.sync_copy(buf, out_hbm.at[idx_vmem])

    return scatter(rows, dest.reshape(n_steps, win))


def _sc_gather_rows(table, idx, win):
    n_tok = idx.shape[0]
    row_w = table.shape[1]
    mesh = _sc_mesh()
    n_workers = mesh.num_cores * mesh.num_subcores
    n_steps = n_tok // win
    assert n_tok % win == 0 and n_steps % n_workers == 0
    per_worker = n_steps // n_workers

    @pl.kernel(out_type=jax.ShapeDtypeStruct((n_tok, row_w), table.dtype), mesh=mesh,
               scratch_types=[pltpu.VMEM((win,), jnp.int32), pltpu.VMEM((win, row_w), table.dtype)])
    def gather(table_hbm, idx_hbm, out_hbm, idx_vmem, buf):
        worker = lax.axis_index("core") * mesh.num_subcores + lax.axis_index("subcore")

        @pl.loop(0, per_worker)
        def _(k):
            step = worker * per_worker + k
            pltpu.sync_copy(idx_hbm.at[step], idx_vmem)
            pltpu.sync_copy(table_hbm.at[idx_vmem], buf)
            pltpu.sync_copy(buf, out_hbm.at[pl.ds(step * win, win)])

    return gather(table, idx.reshape(n_steps, win))


def _sc_pack_rows(w, chunk):
    *lead, k, n = w.shape
    rows = math.prod(lead) * k
    mesh = _sc_mesh()
    lanes = SC_LANES
    n_workers = mesh.num_cores * mesh.num_subcores
    assert k % 2 == 0 and rows % (n_workers * chunk) == 0 and chunk % 2 == 0 and n % (lanes * SC_PACK_UNROLL) == 0
    per_worker = rows // (n_workers * chunk)

    def to_bf16_bits(x):
        bits = lax.bitcast_convert_type(x, U32)
        return (bits + jnp.uint32(0x7FFF) + ((bits >> 16) & jnp.uint32(1))) >> 16

    @pl.kernel(out_type=jax.ShapeDtypeStruct((rows // 2, n), U32), mesh=mesh,
               scratch_types=[pltpu.VMEM((chunk, n), F32), pltpu.VMEM((chunk // 2, n), U32)],
               compiler_params=pltpu.CompilerParams(needs_layout_passes=False))
    def pack(w_hbm, out_hbm, src, dst):
        worker = lax.axis_index("core") * mesh.num_subcores + lax.axis_index("subcore")

        @pl.loop(0, per_worker)
        def _(step):
            blk = worker * per_worker + step
            row0 = pl.multiple_of(blk * chunk, chunk)
            out_row0 = pl.multiple_of(blk * (chunk // 2), chunk // 2)
            pltpu.sync_copy(w_hbm.at[pl.ds(row0, chunk)], src)

            @pl.loop(0, chunk // 2)
            def _(q):
                @pl.loop(0, n, step=lanes * SC_PACK_UNROLL)
                def _(c0):
                    for u in range(SC_PACK_UNROLL):
                        c = c0 + u * lanes
                        lo = to_bf16_bits(src[2 * q, pl.ds(c, lanes)])
                        hi = to_bf16_bits(src[2 * q + 1, pl.ds(c, lanes)])
                        dst[q, pl.ds(c, lanes)] = lo | (hi << 16)

            pltpu.sync_copy(dst, out_hbm.at[pl.ds(out_row0, chunk // 2)])

    return pack(w.reshape(rows, n)).reshape(*lead, k // 2, n)


def _expert_kernel(bm, ea_ref, g0_ref, u0_ref, d0_ref, g1_ref, u1_ref, d1_ref, slot_ref, nused_ref,
                   xs_ref, wga_ref, wua_ref, wda_ref,
                   wgb0_ref, wub0_ref, wdb0_ref, wgb1_ref, wub1_ref, wdb1_ref, ys_ref, act_scr, y_scr):
    j = pl.program_id(0)
    n_used = nused_ref[0]
    half = D // 2
    b_slots = ((wgb0_ref, wub0_ref, wdb0_ref), (wgb1_ref, wub1_ref, wdb1_ref))

    def wmat(ref, k0, k1, cols):
        return pltpu.bitcast(ref[0, k0 // 2:k1 // 2, cols], BF16)

    def store_previous():
        for c0 in range(0, half, COLB):
            lo_bits = lax.bitcast_convert_type(y_scr[:, c0:c0 + COLB].astype(BF16).astype(F32), U32)
            hi_bits = lax.bitcast_convert_type(y_scr[:, half + c0:half + c0 + COLB].astype(BF16).astype(F32), U32)
            ys_ref[:, c0:c0 + COLB] = (lo_bits >> 16) | (hi_bits & jnp.uint32(0xFFFF0000))

    def compute(slot):
        wgb_ref, wub_ref, wdb_ref = b_slots[slot]
        bits = xs_ref[:, 0:half]
        lo = lax.bitcast_convert_type(bits << 16, F32).astype(BF16)
        hi = lax.bitcast_convert_type(bits & jnp.uint32(0xFFFF0000), F32).astype(BF16)
        wts = lax.bitcast_convert_type(xs_ref[:, half:half + ROUTER_LANES], F32)
        for e, (wg_ref, wu_ref) in enumerate(((wga_ref, wua_ref), (wgb_ref, wub_ref))):
            for c in range(D_EXPERT // COLB):
                cols = slice(c * COLB, (c + 1) * COLB)
                g = _dot(lo, wmat(wg_ref, 0, half, cols)) + _dot(hi, wmat(wg_ref, half, D, cols))
                up = _dot(lo, wmat(wu_ref, 0, half, cols)) + _dot(hi, wmat(wu_ref, half, D, cols))
                act_scr[e, :, cols] = (jax.nn.silu(g) * up).astype(BF16)
        w_a = wts[:, 0:1]
        w_b = wts[:, 1:2]

        for c0 in range(0, D, COLB):
            cols = slice(c0, c0 + COLB)
            y_scr[:, cols] = (w_a * _dot(act_scr[0], wmat(wda_ref, 0, D_EXPERT, cols))
                              + w_b * _dot(act_scr[1], wmat(wdb_ref, 0, D_EXPERT, cols)))

    @pl.when(j == 0)
    def _():
        compute(0)

    for slot in (0, 1):
        @pl.when((j > 0) & (j < n_used) & (slot_ref[j] == slot))
        def _(slot=slot):
            store_previous()
            compute(slot)

    @pl.when(j == n_used)
    def _():
        store_previous()


def _experts(xs, blk_a, blk_b_slots, blk_slot, n_used, w_gate, w_up, w_down, bm):
    n_rows = xs.shape[0]
    n_blocks = n_rows // bm
    n_prefetch = 3 + len(blk_b_slots)

    def x_map(j, *pre):
        return (jnp.minimum(j, pre[-1][0] - 1), 0)

    def y_map(j, *pre):
        return (jnp.clip(j - 1, 0, pre[-1][0] - 1), 0)

    def w_map(k):
        return lambda j, *pre: (pre[k][j], 0, 0)

    gate_up = (1, D // 2, D_EXPERT)
    down = (1, D_EXPERT // 2, D)
    return pl.pallas_call(
        functools.partial(_expert_kernel, bm),
        grid_spec=pltpu.PrefetchScalarGridSpec(
            num_scalar_prefetch=n_prefetch,
            grid=(n_blocks,),
            in_specs=[pl.BlockSpec((bm, ROW_W), x_map),
                      pl.BlockSpec(gate_up, w_map(0)), pl.BlockSpec(gate_up, w_map(0)), pl.BlockSpec(down, w_map(0)),
                      pl.BlockSpec(gate_up, w_map(1)), pl.BlockSpec(gate_up, w_map(2)), pl.BlockSpec(down, w_map(3)),
                      pl.BlockSpec(gate_up, w_map(4)), pl.BlockSpec(gate_up, w_map(5)), pl.BlockSpec(down, w_map(6))],
            out_specs=pl.BlockSpec((bm, D // 2), y_map),
            scratch_shapes=[pltpu.VMEM((2, bm, D_EXPERT), BF16), pltpu.VMEM((bm, D), F32)],
        ),
        out_shape=jax.ShapeDtypeStruct((n_rows, D // 2), U32),
        compiler_params=pltpu.CompilerParams(
            dimension_semantics=("arbitrary",), vmem_limit_bytes=VMEM_LIMIT),
        name="experts",
    )(blk_a, *blk_b_slots, blk_slot, n_used, xs, w_gate, w_up, w_down, w_gate, w_up, w_down,
      w_gate, w_up, w_down)


def _final_kernel(alpha, x1_ref, f_ref, mod_ref, l2g_ref, l2b_ref, out_ref):
    half = D // 2
    gate2 = mod_ref[0][5:6]
    bits = f_ref[...]
    f_lo = lax.bitcast_convert_type(bits << 16, F32)
    f_hi = lax.bitcast_convert_type(bits & jnp.uint32(0xFFFF0000), F32)
    y_lo = alpha * x1_ref[0, :, 0:half] + gate2[:, 0:half] * f_lo
    y_hi = alpha * x1_ref[0, :, half:D] + gate2[:, half:D] * f_hi
    mu = (jnp.sum(y_lo, axis=-1, keepdims=True) + jnp.sum(y_hi, axis=-1, keepdims=True)) * (1.0 / D)
    c_lo = y_lo - mu
    c_hi = y_hi - mu
    var = (jnp.sum(c_lo * c_lo, axis=-1, keepdims=True) + jnp.sum(c_hi * c_hi, axis=-1, keepdims=True)) * (1.0 / D)
    rs = lax.rsqrt(var + LN_EPS)
    out_ref[0, :, 0:half] = c_lo * rs * l2g_ref[:, 0:half] + l2b_ref[:, 0:half]
    out_ref[0, :, half:D] = c_hi * rs * l2g_ref[:, half:D] + l2b_ref[:, half:D]


def _final(x1, f, mod, ln2_g, ln2_b, b_off, nbg, alpha, tk):
    s_len = x1.shape[1]
    n_i = s_len // tk
    return pl.pallas_call(
        functools.partial(_final_kernel, alpha),
        grid=(nbg, n_i),
        in_specs=[pl.BlockSpec((1, tk, D), lambda b, i: (b + b_off, i, 0)),
                  pl.BlockSpec((tk, D // 2), lambda b, i: (b * n_i + i, 0)),
                  pl.BlockSpec((1, N_MOD, D), lambda b, i: (b + b_off, 0, 0)),
                  pl.BlockSpec((1, D), lambda b, i: (0, 0)),
                  pl.BlockSpec((1, D), lambda b, i: (0, 0))],
        out_specs=pl.BlockSpec((1, tk, D), lambda b, i: (b, i, 0)),
        out_shape=jax.ShapeDtypeStruct((nbg, s_len, D), F32),
        compiler_params=pltpu.CompilerParams(
            dimension_semantics=("arbitrary", "arbitrary"), vmem_limit_bytes=VMEM_LIMIT),
        name="final",
    )(x1, f, mod, ln2_g, ln2_b)


def _class_experts():
    ea, eb = [], []
    for g in range(N_GROUPS):
        for a in range(EPG):
            for b in range(a + 1, EPG):
                ea.append(g * EPG + a)
                eb.append(g * EPG + b)
    return np.asarray(ea, np.int32), np.asarray(eb, np.int32)


def _layer(x_p, x_s, c_all, p, alpha):
    nbp, s_len, _ = x_p.shape
    nbs = x_s.shape[0]
    nb = nbp + nbs
    n_tok = nb * s_len

    mod = _modulation(c_all, p["w_mod"], p["b_mod"]).reshape(nb, N_MOD, D)

    bsf = jnp.repeat(p["b_spatial"].T, HEAD_DIM, axis=1)
    n_in = p["w_in"].shape[1]
    w_router = jnp.concatenate(
        [p["w_router_group"], p["w_router_expert"],
         jnp.zeros((D, ROUTER_LANES - N_GROUPS - N_EXPERTS), F32)], axis=1)
    b_router = jnp.concatenate(
        [p["b_router_group"], p["b_router_expert"],
         jnp.zeros((ROUTER_LANES - N_GROUPS - N_EXPERTS,), F32)]).reshape(1, ROUTER_LANES)
    wts = [
        _pack_rows(p["w_in"]), p["b_in"].reshape(1, n_in),
        p["w_spatial"].astype(BF16), bsf,
        p["sgu_g"].reshape(1, D), p["sgu_b"].reshape(1, D),
        _pack_rows(p["w_pool"]), p["pool_scale"].reshape(1, D),
        _pack_rows(p["w_branch_a"]), _pack_rows(p["w_branch_b"]), _pack_rows(p["w_out"]),
        p["ln1_g"].reshape(1, D), p["ln1_b"].reshape(1, D),
        _pack_rows(w_router), b_router,
    ]
    expert_w = [_sc_pack_rows(p[name], SC_PACK_CHUNK) for name in ("w_exp_gate", "w_exp_up", "w_exp_down")]
    x1, rows, cr, cnt = _mixer(x_p, x_s, mod, wts, alpha, MIXER_TS)

    bm = EXPERT_BM
    assert n_tok % bm == 0 and n_tok % DEST_LANES == 0
    counts = cnt[0, :N_CLASSES].astype(jnp.int32)
    padded = (counts + bm - 1) // bm * bm
    seg_end = jnp.cumsum(padded).astype(jnp.int32)
    seg_start = seg_end - padded
    n_blocks = n_tok // bm + N_CLASSES
    block_start = jnp.arange(n_blocks, dtype=jnp.int32) * bm
    blk_cls = jnp.minimum(
        jnp.sum((seg_end[None, :] <= block_start[:, None]).astype(jnp.int32), axis=1), N_CLASSES - 1)
    cls_a, cls_b = _class_experts()
    blk_a = jnp.asarray(cls_a)[blk_cls]
    idx = jnp.arange(n_blocks, dtype=jnp.int32)
    changed = jnp.concatenate([jnp.zeros((1,), jnp.int32), (blk_cls[1:] != blk_cls[:-1]).astype(jnp.int32)])
    blk_slot = jnp.cumsum(changed).astype(jnp.int32) % 2
    later = jnp.where(blk_cls[None, :] > blk_cls[:, None], blk_cls[None, :], N_CLASSES)
    next_cls = jnp.minimum(jnp.min(later, axis=1), N_CLASSES - 1)
    earlier = jnp.max(jnp.where(blk_cls[None, :] < blk_cls[:, None], blk_cls[None, :], -1), axis=1)
    prev_cls = jnp.where(earlier >= 0, earlier, blk_cls)
    same = blk_cls[None, :] == blk_cls[:, None]
    pos_in_cls = idx - jnp.min(jnp.where(same, idx[None, :], n_blocks), axis=1)
    last_pos = jnp.sum(same.astype(jnp.int32), axis=1) - 1
    b_now = jnp.asarray(cls_b)[blk_cls]
    b_next = jnp.asarray(cls_b)[next_cls]
    b_prev = jnp.asarray(cls_b)[prev_cls]
    blk_b_slots = [
        jnp.where(blk_slot == s, b_now, jnp.where(pos_in_cls >= jnp.minimum(k, last_pos), b_next, b_prev))
        for s in (0, 1) for k in range(3)]
    n_used = (seg_end[-1:] // bm).astype(jnp.int32)

    lane_shape = (n_tok // DEST_LANES, DEST_LANES)
    dest = _dest(seg_start, cr[0].reshape(lane_shape), cr[1].reshape(lane_shape)).reshape(n_tok)

    xs = _sc_scatter_rows(rows, dest, n_blocks * bm, SC_WIN)
    ys = _experts(xs, blk_a, blk_b_slots, blk_slot, n_used, *expert_w, bm)

    l2g = p["ln2_g"].reshape(1, D)
    l2b = p["ln2_b"].reshape(1, D)
    tp = nbp * s_len
    f_p = _sc_gather_rows(ys, dest[:tp], SC_WIN)
    f_s = _sc_gather_rows(ys, dest[tp:], SC_WIN)
    y_p = _final(x1, f_p, mod, l2g, l2b, 0, nbp, alpha, FINAL_TK)
    y_s = _final(x1, f_s, mod, l2g, l2b, nbp, nbs, alpha, FINAL_TK)
    return y_p, y_s


_PARAM_NAMES = ("w_mod", "b_mod", "w_in", "b_in", "w_spatial", "b_spatial", "sgu_g", "sgu_b", "w_pool",
                "pool_scale", "w_branch_a", "w_branch_b", "w_out", "ln1_g", "ln1_b", "w_router_group",
                "b_router_group", "w_router_expert", "b_router_expert", "w_exp_gate", "w_exp_up",
                "w_exp_down", "ln2_g", "ln2_b")


def kernel(x_prompt, x_sample, c_prompt, c_sample, w_mod, b_mod, w_in, b_in, w_spatial, b_spatial, sgu_g, sgu_b, w_pool, pool_scale, w_branch_a, w_branch_b, w_out, ln1_g, ln1_b, w_router_group, b_router_group, w_router_expert, b_router_expert, w_exp_gate, w_exp_up, w_exp_down, ln2_g, ln2_b):
    params = (w_mod, b_mod, w_in, b_in, w_spatial, b_spatial, sgu_g, sgu_b, w_pool, pool_scale,
              w_branch_a, w_branch_b, w_out, ln1_g, ln1_b, w_router_group, b_router_group,
              w_router_expert, b_router_expert, w_exp_gate, w_exp_up, w_exp_down, ln2_g, ln2_b)
    depth = w_mod.shape[0]
    alpha = (2.0 * depth) ** 0.25
    c_all = jnp.concatenate([c_prompt, c_sample], axis=0)
    y_p, y_s = x_prompt, x_sample
    for l in range(depth):
        p = {name: w[l] for name, w in zip(_PARAM_NAMES, params)}
        y_p, y_s = _layer(y_p, y_s, c_all, p, alpha)
    return (y_p, y_s)
```

```python
import functools
import math

import jax
import jax.numpy as jnp
import numpy as np
from jax import lax
from jax.experimental import pallas as pl
from jax.experimental.pallas import tpu as pltpu
from jax.experimental.pallas import tpu_sc as plsc

F32 = jnp.float32
BF16 = jnp.bfloat16
U32 = jnp.uint32

D = 1024
CHUNK = 128
SGU_HEADS = 8
HEAD_DIM = D // SGU_HEADS
POOL_WINDOWS = (2, 4, 8, 16)
POOL_GROUP_DIM = D // len(POOL_WINDOWS)
N_MOD = 6
N_GROUPS = 4
EPG = 8
N_EXPERTS = N_GROUPS * EPG
D_EXPERT = D // 2
LN_EPS = 1e-5

HALO = 16
COLB = 256
ROUTER_LANES = 128
EXPERT_LANE0 = N_GROUPS
PAIRS = EPG * (EPG - 1) // 2
N_CLASSES = N_GROUPS * PAIRS
ROW_W = D // 2 + ROUTER_LANES
MIXER_TS = 512
EXPERT_BM = 512
FINAL_TK = 2048
DEST_LANES = 512
PACK_BLOCK = 1024
SC_WIN = 128
SC_LANES = 16
SC_PACK_CHUNK = 32
SC_PACK_UNROLL = 8
VMEM_LIMIT = 56 * 1024 * 1024


def _ln(x):
    mu = jnp.mean(x, axis=-1, keepdims=True)
    xc = x - mu
    var = jnp.mean(xc * xc, axis=-1, keepdims=True)
    return xc * lax.rsqrt(var + LN_EPS)


_GELU_A = -2.0 * math.sqrt(2.0 / math.pi) * math.log2(math.e)
_GELU_B = _GELU_A * 0.044715


def _gelu_tanh(x):
    return x / (1.0 + jnp.exp2(x * (_GELU_A + _GELU_B * (x * x))))


def _dot(a, b):
    return jnp.dot(a, b, preferred_element_type=F32)


def _mod_kernel(c_ref, w_ref, b_ref, o_ref):
    a = jax.nn.silu(c_ref[...]).astype(BF16)
    o_ref[...] = _dot(a, w_ref[...].astype(BF16)) + b_ref[...]


def _modulation(c_all, w_mod, b_mod):
    nb = c_all.shape[0]
    n_out = w_mod.shape[1]
    cb = 512
    return pl.pallas_call(
        _mod_kernel,
        grid=(n_out // cb,),
        in_specs=[pl.BlockSpec((nb, D), lambda j: (0, 0)),
                  pl.BlockSpec((D, cb), lambda j: (0, j)),
                  pl.BlockSpec((1, cb), lambda j: (0, j))],
        out_specs=pl.BlockSpec((nb, cb), lambda j: (0, j)),
        out_shape=jax.ShapeDtypeStruct((nb, n_out), F32),
        name="mod",
    )(c_all, w_mod, b_mod.reshape(1, n_out))


def _pack_kernel(w_ref, o_ref):
    o_ref[...] = pltpu.bitcast(w_ref[...].astype(BF16), U32)


def _pack_rows(w):
    *lead, k, n = w.shape
    rows = math.prod(lead) * k
    rb = min(rows, PACK_BLOCK)
    nb = min(n, PACK_BLOCK)
    assert k % 2 == 0 and rows % rb == 0 and n % nb == 0
    packed = pl.pallas_call(
        _pack_kernel,
        grid=(rows // rb, n // nb),
        in_specs=[pl.BlockSpec((rb, nb), lambda i, j: (i, j))],
        out_specs=pl.BlockSpec((rb // 2, nb), lambda i, j: (i, j)),
        out_shape=jax.ShapeDtypeStruct((rows // 2, n), U32),
        name="pack",
    )(w.reshape(rows, n))
    return packed.reshape(*lead, k // 2, n)


def _mixer_kernel(nbp, ts, s_len, n_i, alpha,
                  xp_ref, xs_ref, xpp_ref, xsp_ref, xpn_ref, xsn_ref, mod_ref, modb_ref, icnt_ref,
                  win_ref, bin_ref, ws_ref, bsf_ref, sg_ref, sb_ref, wpool_ref, psc_ref,
                  wa_ref, wb_ref, wo_ref, l1g_ref, l1b_ref, wr_ref, br_ref, tri_ref,
                  x1_ref, h2u_ref, cr_ref, cnt_ref,
                  h_scr, u_scr, v_scr, vb_scr, a_scr, p_scr, m_scr, h2_scr, y_scr, xk_scr, ga_scr, gb_scr,
                  carry_scr):
    t = pl.program_id(0)
    n_tiles = pl.num_programs(0) - 1
    tf = jnp.minimum(t, n_tiles - 1)
    b = tf // n_i
    i = tf % n_i
    is_p = b < nbp
    ncb = D // COLB
    n_ext = ts + 2 * HALO
    half = D // 2
    main = slice(HALO, HALO + ts)

    def wmat(ref, cols=slice(None)):
        return pltpu.bitcast(ref[:, cols], BF16)

    @pl.when(t == 0)
    def _():
        carry_scr[...] = jnp.zeros_like(carry_scr)
        m_scr[...] = jnp.zeros_like(m_scr)
        xk_scr[...] = jnp.zeros_like(xk_scr)

    def row_rstd(read, mu):
        ss = jnp.zeros((ts, 1), F32)
        for j in range(ncb):
            xc = read(slice(j * COLB, (j + 1) * COLB)) - mu
            ss = ss + jnp.sum(xc * xc, axis=-1, keepdims=True)
        return lax.rsqrt(ss * (1.0 / D) + LN_EPS)

    modb = modb_ref[0]
    gate1_b, shift2, scale2 = modb[2:3], modb[3:4], modb[4:5]
    ysum = jnp.zeros((ts, 1), F32)
    for j in range(ncb):
        cols = slice(j * COLB, (j + 1) * COLB)
        tmix = _dot(m_scr[...], wmat(wo_ref, cols))
        y = alpha * xk_scr[:, cols] + gate1_b[:, cols] * tmix
        y_scr[:, cols] = y
        ysum = ysum + jnp.sum(y, axis=-1, keepdims=True)

    mod = mod_ref[0]
    shift1, scale1 = mod[0:1], mod[1:2]

    def adaln1(xv):
        return (_ln(xv) * (1.0 + scale1) + shift1).astype(BF16)

    h_scr[0:HALO, :] = adaln1(jnp.where(is_p, xpp_ref[0], xsp_ref[0]))
    h_scr[main, :] = adaln1(jnp.where(is_p, xp_ref[0], xs_ref[0]))
    h_scr[HALO + ts:, :] = adaln1(jnp.where(is_p, xpn_ref[0], xsn_ref[0]))

    def proj(hrows, c0):
        return _dot(h_scr[hrows, :], wmat(win_ref, slice(c0, c0 + COLB))) + bin_ref[:, c0:c0 + COLB]

    ymu = ysum * (1.0 / D)
    yrs = row_rstd(lambda cols: y_scr[:, cols], ymu)
    xsum = jnp.zeros((ts, 1), F32)
    for j in range(ncb):
        cols = slice(j * COLB, (j + 1) * COLB)
        x1 = (y_scr[:, cols] - ymu) * yrs * l1g_ref[:, cols] + l1b_ref[:, cols]
        x1_ref[0, :, cols] = x1
        xsum = xsum + jnp.sum(x1, axis=-1, keepdims=True)
    xmu = xsum * (1.0 / D)
    xrs = row_rstd(lambda cols: x1_ref[0, :, cols], xmu)
    for j in range(ncb):
        cols = slice(j * COLB, (j + 1) * COLB)
        h2 = (x1_ref[0, :, cols] - xmu) * xrs * (1.0 + scale2[:, cols]) + shift2[:, cols]
        h2_scr[:, cols] = h2.astype(BF16)
    lo_bits = lax.bitcast_convert_type(h2_scr[:, 0:half].astype(F32), U32)
    hi_bits = lax.bitcast_convert_type(h2_scr[:, half:D].astype(F32), U32)
    h2u_ref[:, 0:half] = (lo_bits >> 16) | (hi_bits & jnp.uint32(0xFFFF0000))

    vsum = jnp.zeros((ts, 1), F32)
    for j in range(ncb):
        cols = slice(j * COLB, (j + 1) * COLB)
        u_scr[:, cols] = _gelu_tanh(proj(main, j * COLB))
        gv = _gelu_tanh(proj(main, D + j * COLB))
        v_scr[:, cols] = gv
        vsum = vsum + jnp.sum(gv, axis=-1, keepdims=True)

    logits = _dot(h2_scr[...], wmat(wr_ref)) + br_ref[...]
    lane = lax.broadcasted_iota(jnp.int32, (ts, ROUTER_LANES), 1)
    lane_f = lane.astype(F32)
    neg = -jnp.inf
    is_g = lane < N_GROUPS
    lg = jnp.where(is_g, logits, neg)
    mg = jnp.max(lg, axis=-1, keepdims=True)
    g_sel = jnp.min(jnp.where(lg == mg, lane_f, float(ROUTER_LANES)), axis=-1, keepdims=True)
    p_sel = 1.0 / jnp.sum(jnp.where(is_g, jnp.exp(logits - mg), 0.0), axis=-1, keepdims=True)
    e_lo = EXPERT_LANE0 + g_sel * EPG
    in_grp = (lane_f >= e_lo) & (lane_f < e_lo + EPG)
    le = jnp.where(in_grp, logits, neg)
    m1 = jnp.max(le, axis=-1, keepdims=True)
    i1 = jnp.min(jnp.where(le == m1, lane_f, float(ROUTER_LANES)), axis=-1, keepdims=True)
    le2 = jnp.where(lane_f == i1, neg, le)
    m2 = jnp.max(le2, axis=-1, keepdims=True)
    i2 = jnp.min(jnp.where(le2 == m2, lane_f, float(ROUTER_LANES)), axis=-1, keepdims=True)
    e2x = jnp.exp(m2 - m1)
    den = 1.0 + e2x
    g1 = p_sel / den
    g2 = p_sel * e2x / den

    j1 = i1 - e_lo
    j2 = i2 - e_lo
    first_is_a = j1 < j2
    ja = jnp.minimum(j1, j2)
    jb = jnp.maximum(j1, j2)
    cls = g_sel * PAIRS + (EPG - 1) * ja - ja * (ja - 1.0) * 0.5 + (jb - ja - 1.0)
    w_a = jnp.where(first_is_a, g1, g2)
    w_b = jnp.where(first_is_a, g2, g1)
    extra = jnp.where(lane == 0, w_a, jnp.where(lane == 1, w_b, 0.0))
    h2u_ref[:, half:half + ROUTER_LANES] = lax.bitcast_convert_type(extra, U32)

    hit = lane_f == cls
    onehot = jnp.where(hit, (t > 0).astype(F32), 0.0)

    def finish_ranks():
        pre = _dot(tri_ref[...], onehot.astype(BF16)) + carry_scr[0:1, :]
        rank = jnp.sum(jnp.where(hit, pre, 0.0), axis=-1, keepdims=True)
        carry_scr[0:1, :] = carry_scr[0:1, :] + jnp.sum(onehot, axis=0, keepdims=True)
        cnt_ref[...] = carry_scr[...]
        cr = jnp.where(lane == 0, cls, jnp.where(lane == 1, rank, 0.0))
        cr_ref[...] = cr.T[0:8, :]

    vmu = vsum * (1.0 / D)
    vrs = row_rstd(lambda cols: v_scr[:, cols], vmu)
    for j in range(ncb):
        cols = slice(j * COLB, (j + 1) * COLB)
        vb_scr[:, cols] = ((v_scr[:, cols] - vmu) * vrs * sg_ref[:, cols] + sb_ref[:, cols]).astype(BF16)

    def pool_project(gi):
        ext_pos = lax.broadcasted_iota(jnp.int32, (n_ext, 1), 0) + (i * ts - HALO)
        ext_valid = (ext_pos >= 0) & (ext_pos < s_len)
        return jnp.where(ext_valid, proj(slice(None), 2 * D + gi * POOL_GROUP_DIM), 0.0)

    def pool_group(gi, zp):
        w = POOL_WINDOWS[gi]
        cols = slice(gi * POOL_GROUP_DIM, (gi + 1) * POOL_GROUP_DIM)
        acc = zp + pltpu.roll(zp, 1, 0)
        if w >= 4:
            acc = pltpu.roll(acc, 1, 0) + pltpu.roll(acc, n_ext - 1, 0)
        if w >= 8:
            acc = pltpu.roll(acc, 2, 0) + pltpu.roll(acc, n_ext - 2, 0)
        if w >= 16:
            acc = pltpu.roll(acc, 4, 0) + pltpu.roll(acc, n_ext - 4, 0)
        inv_count = icnt_ref[:, gi:gi + 1]
        dd = (acc[main] * inv_count - zp[main]).astype(BF16)
        p_scr[:, cols] = (_dot(dd, pltpu.bitcast(wpool_ref[gi], BF16)) * psc_ref[:, cols]).astype(BF16)

    def sgu_chunk(c):
        crow = slice(c * CHUNK, (c + 1) * CHUNK)
        for hh in range(SGU_HEADS):
            cols = slice(hh * HEAD_DIM, (hh + 1) * HEAD_DIM)
            mixed = _dot(ws_ref[hh], vb_scr[crow, cols]) + bsf_ref[:, cols]
            a_scr[crow, cols] = (u_scr[crow, cols] * mixed).astype(BF16)

    assert len(POOL_WINDOWS) == ncb
    for j in range(ncb):
        cols = slice(j * COLB, (j + 1) * COLB)
        zp = pool_project(j)
        ga_scr[:, cols] = jax.nn.sigmoid(proj(main, 3 * D + j * COLB))
        gb_scr[:, cols] = jax.nn.sigmoid(proj(main, 4 * D + j * COLB))
        pool_group(j, zp)
    for j in range(ncb):
        cols = slice(j * COLB, (j + 1) * COLB)
        gb_scr[:, cols] = gb_scr[:, cols] * _dot(p_scr[...], wmat(wb_ref, cols))
    for c in range(ts // CHUNK):
        sgu_chunk(c)

    finish_ranks()

    for j in range(ncb):
        cols = slice(j * COLB, (j + 1) * COLB)
        ta = _dot(a_scr[...], wmat(wa_ref, cols))
        m_scr[:, cols] = (ga_scr[:, cols] * ta + gb_scr[:, cols]).astype(BF16)
    xk_scr[...] = jnp.where(is_p, xp_ref[0], xs_ref[0])


def _const_spec(shape):
    nd = len(shape)
    return pl.BlockSpec(shape, lambda t: (0,) * nd, pipeline_mode=pl.Buffered(1))


def _mixer(x_p, x_s, mod, wts, alpha, ts):
    nbp, s_len, _ = x_p.shape
    nbs = x_s.shape[0]
    assert x_s.shape[1] == s_len and s_len % ts == 0 and ts % CHUNK == 0
    nb = nbp + nbs
    n_i = s_len // ts
    hb = ts // HALO
    n_hb = s_len // HALO

    n_tiles = nb * n_i

    def front(t):
        tf = jnp.minimum(t, n_tiles - 1)
        return tf // n_i, tf % n_i

    def back(t):
        return jnp.maximum(t - 1, 0)

    def pick(b, on_p, val, const):
        return jnp.where(b < nbp if on_p else b >= nbp, val, const)

    def main_map(on_p):
        def f(t):
            b, i = front(t)
            bb = pick(b, on_p, b if on_p else b - nbp, nbp - 1 if on_p else 0)
            ii = pick(b, on_p, i, n_i - 1 if on_p else 0)
            return (bb, ii, 0)
        return f

    def halo_map(on_p, nxt):
        def f(t):
            b, i = front(t)
            bb = pick(b, on_p, b if on_p else b - nbp, nbp - 1 if on_p else 0)
            idx = jnp.minimum((i + 1) * hb, n_hb - 1) if nxt else jnp.maximum(i * hb - 1, 0)
            ii = pick(b, on_p, idx, n_hb - 1 if on_p else 0)
            return (bb, ii, 0)
        return f

    tri = (lax.broadcasted_iota(jnp.int32, (ts, ts), 1)
           < lax.broadcasted_iota(jnp.int32, (ts, ts), 0)).astype(BF16)
    consts = list(wts) + [tri]
    in_specs = [
        pl.BlockSpec((1, ts, D), main_map(True)),
        pl.BlockSpec((1, ts, D), main_map(False)),
        pl.BlockSpec((1, HALO, D), halo_map(True, False)),
        pl.BlockSpec((1, HALO, D), halo_map(False, False)),
        pl.BlockSpec((1, HALO, D), halo_map(True, True)),
        pl.BlockSpec((1, HALO, D), halo_map(False, True)),
        pl.BlockSpec((1, N_MOD, D), lambda t: (front(t)[0], 0, 0)),
        pl.BlockSpec((1, N_MOD, D), lambda t: (back(t) // n_i, 0, 0)),
        pl.BlockSpec((ts, ROUTER_LANES), lambda t: (front(t)[1], 0)),
    ] + [_const_spec(w.shape) for w in consts]

    pos = jnp.arange(s_len, dtype=jnp.int32)[:, None]
    win = jnp.asarray(POOL_WINDOWS + (1,) * (ROUTER_LANES - len(POOL_WINDOWS)), jnp.int32)[None, :]
    inv_count = 1.0 / (jnp.minimum(pos + (win - 1 - win // 2), s_len - 1)
                       - jnp.maximum(pos - win // 2, 0) + 1).astype(F32)
    n_tok = nb * s_len
    out_shape = (
        jax.ShapeDtypeStruct((nb, s_len, D), F32),
        jax.ShapeDtypeStruct((n_tok, ROW_W), U32),
        jax.ShapeDtypeStruct((8, n_tok), F32),
        jax.ShapeDtypeStruct((8, ROUTER_LANES), F32),
    )
    out_specs = (
        pl.BlockSpec((1, ts, D), lambda t: (back(t) // n_i, back(t) % n_i, 0)),
        pl.BlockSpec((ts, ROW_W), lambda t: (back(t), 0)),
        pl.BlockSpec((8, ts), lambda t: (0, back(t))),
        pl.BlockSpec((8, ROUTER_LANES), lambda t: (0, 0)),
    )
    scratch = [
        pltpu.VMEM((ts + 2 * HALO, D), BF16),
        pltpu.VMEM((ts, D), F32),
        pltpu.VMEM((ts, D), F32),
        pltpu.VMEM((ts, D), BF16),
        pltpu.VMEM((ts, D), BF16),
        pltpu.VMEM((ts, D), BF16),
        pltpu.VMEM((ts, D), BF16),
        pltpu.VMEM((ts, D), BF16),
        pltpu.VMEM((ts, D), F32),
        pltpu.VMEM((ts, D), F32),
        pltpu.VMEM((ts, D), F32),
        pltpu.VMEM((ts, D), F32),
        pltpu.VMEM((8, ROUTER_LANES), F32),
    ]
    return pl.pallas_call(
        functools.partial(_mixer_kernel, nbp, ts, s_len, n_i, alpha),
        grid=(n_tiles + 1,),
        in_specs=in_specs,
        out_specs=out_specs,
        out_shape=out_shape,
        scratch_shapes=scratch,
        compiler_params=pltpu.CompilerParams(
            dimension_semantics=("arbitrary",), vmem_limit_bytes=VMEM_LIMIT),
        name="mixer",
    )(x_p, x_s, x_p, x_s, x_p, x_s, mod, mod, inv_count, *consts)


def _dest_kernel(seg_ref, cls_ref, rank_ref, o_ref):
    cls = cls_ref[...]
    acc = rank_ref[...].astype(jnp.int32)
    for c in range(N_CLASSES):
        acc = acc + jnp.where(cls == float(c), seg_ref[c], 0)
    o_ref[...] = acc


def _dest(seg_start, cls, rank):
    shape = cls.shape
    return pl.pallas_call(
        _dest_kernel,
        grid_spec=pltpu.PrefetchScalarGridSpec(
            num_scalar_prefetch=1,
            grid=(1,),
            in_specs=[pl.BlockSpec(shape, lambda i, s: (0, 0)), pl.BlockSpec(shape, lambda i, s: (0, 0))],
            out_specs=pl.BlockSpec(shape, lambda i, s: (0, 0)),
        ),
        out_shape=jax.ShapeDtypeStruct(shape, jnp.int32),
        name="dest",
    )(seg_start, cls, rank)


def _sc_mesh():
    return plsc.VectorSubcoreMesh(core_axis_name="core", subcore_axis_name="subcore")


def _sc_scatter_rows(rows, dest, n_rows, win):
    n_tok, row_w = rows.shape
    mesh = _sc_mesh()
    n_workers = mesh.num_cores * mesh.num_subcores
    n_steps = n_tok // win
    assert n_tok % win == 0 and n_steps % n_workers == 0
    per_worker = n_steps // n_workers

    @pl.kernel(out_type=jax.ShapeDtypeStruct((n_rows, row_w), rows.dtype), mesh=mesh,
               scratch_types=[pltpu.VMEM((win,), jnp.int32), pltpu.VMEM((win, row_w), rows.dtype)])
    def scatter(rows_hbm, idx_hbm, out_hbm, idx_vmem, buf):
        worker = lax.axis_index("core") * mesh.num_subcores + lax.axis_index("subcore")

        @pl.loop(0, per_worker)
        def _(k):
            step = worker * per_worker + k
            pltpu.sync_copy(idx_hbm.at[step], idx_vmem)
            pltpu.sync_copy(rows_hbm.at[pl.ds(step * win, win)], buf)
            pltpu.sync_copy(buf, out_hbm.at[idx_vmem])

    return scatter(rows, dest.reshape(n_steps, win))


def _sc_gather_rows(table, idx, win):
    n_tok = idx.shape[0]
    row_w = table.shape[1]
    mesh = _sc_mesh()
    n_workers = mesh.num_cores * mesh.num_subcores
    n_steps = n_tok // win
    assert n_tok % win == 0 and n_steps % n_workers == 0
    per_worker = n_steps // n_workers

    @pl.kernel(out_type=jax.ShapeDtypeStruct((n_tok, row_w), table.dtype), mesh=mesh,
               scratch_types=[pltpu.VMEM((win,), jnp.int32), pltpu.VMEM((win, row_w), table.dtype)])
    def gather(table_hbm, idx_hbm, out_hbm, idx_vmem, buf):
        worker = lax.axis_index("core") * mesh.num_subcores + lax.axis_index("subcore")

        @pl.loop(0, per_worker)
        def _(k):
            step = worker * per_worker + k
            pltpu.sync_copy(idx_hbm.at[step], idx_vmem)
            pltpu.sync_copy(table_hbm.at[idx_vmem], buf)
            pltpu.sync_copy(buf, out_hbm.at[pl.ds(step * win, win)])

    return gather(table, idx.reshape(n_steps, win))


def _sc_pack_rows(w, chunk):
    *lead, k, n = w.shape
    rows = math.prod(lead) * k
    mesh = _sc_mesh()
    lanes = SC_LANES
    n_workers = mesh.num_cores * mesh.num_subcores
    assert k % 2 == 0 and rows % (n_workers * chunk) == 0 and chunk % 2 == 0 and n % (lanes * SC_PACK_UNROLL) == 0
    per_worker = rows // (n_workers * chunk)

    def to_bf16_bits(x):
        bits = lax.bitcast_convert_type(x, U32)
        return (bits + jnp.uint32(0x7FFF) + ((bits >> 16) & jnp.uint32(1))) >> 16

    @pl.kernel(out_type=jax.ShapeDtypeStruct((rows // 2, n), U32), mesh=mesh,
               scratch_types=[pltpu.VMEM((chunk, n), F32), pltpu.VMEM((chunk // 2, n), U32)],
               compiler_params=pltpu.CompilerParams(needs_layout_passes=False))
    def pack(w_hbm, out_hbm, src, dst):
        worker = lax.axis_index("core") * mesh.num_subcores + lax.axis_index("subcore")

        @pl.loop(0, per_worker)
        def _(step):
            blk = worker * per_worker + step
            row0 = pl.multiple_of(blk * chunk, chunk)
            out_row0 = pl.multiple_of(blk * (chunk // 2), chunk // 2)
            pltpu.sync_copy(w_hbm.at[pl.ds(row0, chunk)], src)

            @pl.loop(0, chunk // 2)
            def _(q):
                @pl.loop(0, n, step=lanes * SC_PACK_UNROLL)
                def _(c0):
                    for u in range(SC_PACK_UNROLL):
                        c = c0 + u * lanes
                        lo = to_bf16_bits(src[2 * q, pl.ds(c, lanes)])
                        hi = to_bf16_bits(src[2 * q + 1, pl.ds(c, lanes)])
                        dst[q, pl.ds(c, lanes)] = lo | (hi << 16)

            pltpu.sync_copy(dst, out_hbm.at[pl.ds(out_row0, chunk // 2)])

    return pack(w.reshape(rows, n)).reshape(*lead, k // 2, n)


def _expert_kernel(bm, ea_ref, g0_ref, u0_ref, d0_ref, g1_ref, u1_ref, d1_ref, slot_ref, nused_ref,
                   xs_ref, wga_ref, wua_ref, wda_ref,
                   wgb0_ref, wub0_ref, wdb0_ref, wgb1_ref, wub1_ref, wdb1_ref, ys_ref, act_scr, y_scr):
    j = pl.program_id(0)
    n_used = nused_ref[0]
    half = D // 2
    b_slots = ((wgb0_ref, wub0_ref, wdb0_ref), (wgb1_ref, wub1_ref, wdb1_ref))

    def wmat(ref, k0, k1, cols):
        return pltpu.bitcast(ref[0, k0 // 2:k1 // 2, cols], BF16)

    def store_previous():
        for c0 in range(0, half, COLB):
            lo_bits = lax.bitcast_convert_type(y_scr[:, c0:c0 + COLB].astype(BF16).astype(F32), U32)
            hi_bits = lax.bitcast_convert_type(y_scr[:, half + c0:half + c0 + COLB].astype(BF16).astype(F32), U32)
            ys_ref[:, c0:c0 + COLB] = (lo_bits >> 16) | (hi_bits & jnp.uint32(0xFFFF0000))

    def compute(slot):
        wgb_ref, wub_ref, wdb_ref = b_slots[slot]
        bits = xs_ref[:, 0:half]
        lo = lax.bitcast_convert_type(bits << 16, F32).astype(BF16)
        hi = lax.bitcast_convert_type(bits & jnp.uint32(0xFFFF0000), F32).astype(BF16)
        wts = lax.bitcast_convert_type(xs_ref[:, half:half + ROUTER_LANES], F32)
        for e, (wg_ref, wu_ref) in enumerate(((wga_ref, wua_ref), (wgb_ref, wub_ref))):
            for c in range(D_EXPERT // COLB):
                cols = slice(c * COLB, (c + 1) * COLB)
                g = _dot(lo, wmat(wg_ref, 0, half, cols)) + _dot(hi, wmat(wg_ref, half, D, cols))
                up = _dot(lo, wmat(wu_ref, 0, half, cols)) + _dot(hi, wmat(wu_ref, half, D, cols))
                act_scr[e, :, cols] = (jax.nn.silu(g) * up).astype(BF16)
        w_a = wts[:, 0:1]
        w_b = wts[:, 1:2]

        for c0 in range(0, D, COLB):
            cols = slice(c0, c0 + COLB)
            y_scr[:, cols] = (w_a * _dot(act_scr[0], wmat(wda_ref, 0, D_EXPERT, cols))
                              + w_b * _dot(act_scr[1], wmat(wdb_ref, 0, D_EXPERT, cols)))

    @pl.when(j == 0)
    def _():
        compute(0)

    for slot in (0, 1):
        @pl.when((j > 0) & (j < n_used) & (slot_ref[j] == slot))
        def _(slot=slot):
            store_previous()
            compute(slot)

    @pl.when(j == n_used)
    def _():
        store_previous()


def _experts(xs, blk_a, blk_b_slots, blk_slot, n_used, w_gate, w_up, w_down, bm):
    n_rows = xs.shape[0]
    n_blocks = n_rows // bm
    n_prefetch = 3 + len(blk_b_slots)

    def x_map(j, *pre):
        return (jnp.minimum(j, pre[-1][0] - 1), 0)

    def y_map(j, *pre):
        return (jnp.clip(j - 1, 0, pre[-1][0] - 1), 0)

    def w_map(k):
        return lambda j, *pre: (pre[k][j], 0, 0)

    gate_up = (1, D // 2, D_EXPERT)
    down = (1, D_EXPERT // 2, D)
    return pl.pallas_call(
        functools.partial(_expert_kernel, bm),
        grid_spec=pltpu.PrefetchScalarGridSpec(
            num_scalar_prefetch=n_prefetch,
            grid=(n_used[0] + 1,),
            in_specs=[pl.BlockSpec((bm, ROW_W), x_map),
                      pl.BlockSpec(gate_up, w_map(0)), pl.BlockSpec(gate_up, w_map(0)), pl.BlockSpec(down, w_map(0)),
                      pl.BlockSpec(gate_up, w_map(1)), pl.BlockSpec(gate_up, w_map(2)), pl.BlockSpec(down, w_map(3)),
                      pl.BlockSpec(gate_up, w_map(4)), pl.BlockSpec(gate_up, w_map(5)), pl.BlockSpec(down, w_map(6))],
            out_specs=pl.BlockSpec((bm, D // 2), y_map),
            scratch_shapes=[pltpu.VMEM((2, bm, D_EXPERT), BF16), pltpu.VMEM((bm, D), F32)],
        ),
        out_shape=jax.ShapeDtypeStruct((n_rows, D // 2), U32),
        compiler_params=pltpu.CompilerParams(
            dimension_semantics=("arbitrary",), vmem_limit_bytes=VMEM_LIMIT),
        name="experts",
    )(blk_a, *blk_b_slots, blk_slot, n_used, xs, w_gate, w_up, w_down, w_gate, w_up, w_down,
      w_gate, w_up, w_down)


def _final_kernel(alpha, x1_ref, f_ref, mod_ref, l2g_ref, l2b_ref, out_ref):
    half = D // 2
    gate2 = mod_ref[0][5:6]
    bits = f_ref[...]
    f_lo = lax.bitcast_convert_type(bits << 16, F32)
    f_hi = lax.bitcast_convert_type(bits & jnp.uint32(0xFFFF0000), F32)
    y_lo = alpha * x1_ref[0, :, 0:half] + gate2[:, 0:half] * f_lo
    y_hi = alpha * x1_ref[0, :, half:D] + gate2[:, half:D] * f_hi
    mu = (jnp.sum(y_lo, axis=-1, keepdims=True) + jnp.sum(y_hi, axis=-1, keepdims=True)) * (1.0 / D)
    c_lo = y_lo - mu
    c_hi = y_hi - mu
    var = (jnp.sum(c_lo * c_lo, axis=-1, keepdims=True) + jnp.sum(c_hi * c_hi, axis=-1, keepdims=True)) * (1.0 / D)
    rs = lax.rsqrt(var + LN_EPS)
    out_ref[0, :, 0:half] = c_lo * rs * l2g_ref[:, 0:half] + l2b_ref[:, 0:half]
    out_ref[0, :, half:D] = c_hi * rs * l2g_ref[:, half:D] + l2b_ref[:, half:D]


def _final(x1, f, mod, ln2_g, ln2_b, b_off, nbg, alpha, tk):
    s_len = x1.shape[1]
    n_i = s_len // tk
    return pl.pallas_call(
        functools.partial(_final_kernel, alpha),
        grid=(nbg, n_i),
        in_specs=[pl.BlockSpec((1, tk, D), lambda b, i: (b + b_off, i, 0)),
                  pl.BlockSpec((tk, D // 2), lambda b, i: (b * n_i + i, 0)),
                  pl.BlockSpec((1, N_MOD, D), lambda b, i: (b + b_off, 0, 0)),
                  pl.BlockSpec((1, D), lambda b, i: (0, 0)),
                  pl.BlockSpec((1, D), lambda b, i: (0, 0))],
        out_specs=pl.BlockSpec((1, tk, D), lambda b, i: (b, i, 0)),
        out_shape=jax.ShapeDtypeStruct((nbg, s_len, D), F32),
        compiler_params=pltpu.CompilerParams(
            dimension_semantics=("arbitrary", "arbitrary"), vmem_limit_bytes=VMEM_LIMIT),
        name="final",
    )(x1, f, mod, ln2_g, ln2_b)


def _class_experts():
    ea, eb = [], []
    for g in range(N_GROUPS):
        for a in range(EPG):
            for b in range(a + 1, EPG):
                ea.append(g * EPG + a)
                eb.append(g * EPG + b)
    return np.asarray(ea, np.int32), np.asarray(eb, np.int32)


def _layer(x_p, x_s, c_all, p, alpha):
    nbp, s_len, _ = x_p.shape
    nbs = x_s.shape[0]
    nb = nbp + nbs
    n_tok = nb * s_len

    mod = _modulation(c_all, p["w_mod"], p["b_mod"]).reshape(nb, N_MOD, D)

    bsf = jnp.repeat(p["b_spatial"].T, HEAD_DIM, axis=1)
    n_in = p["w_in"].shape[1]
    w_router = jnp.concatenate(
        [p["w_router_group"], p["w_router_expert"],
         jnp.zeros((D, ROUTER_LANES - N_GROUPS - N_EXPERTS), F32)], axis=1)
    b_router = jnp.concatenate(
        [p["b_router_group"], p["b_router_expert"],
         jnp.zeros((ROUTER_LANES - N_GROUPS - N_EXPERTS,), F32)]).reshape(1, ROUTER_LANES)
    wts = [
        _pack_rows(p["w_in"]), p["b_in"].reshape(1, n_in),
        p["w_spatial"].astype(BF16), bsf,
        p["sgu_g"].reshape(1, D), p["sgu_b"].reshape(1, D),
        _pack_rows(p["w_pool"]), p["pool_scale"].reshape(1, D),
        _pack_rows(p["w_branch_a"]), _pack_rows(p["w_branch_b"]), _pack_rows(p["w_out"]),
        p["ln1_g"].reshape(1, D), p["ln1_b"].reshape(1, D),
        _pack_rows(w_router), b_router,
    ]
    expert_w = [_sc_pack_rows(p[name], SC_PACK_CHUNK) for name in ("w_exp_gate", "w_exp_up", "w_exp_down")]
    x1, rows, cr, cnt = _mixer(x_p, x_s, mod, wts, alpha, MIXER_TS)

    bm = EXPERT_BM
    assert n_tok % bm == 0 and n_tok % DEST_LANES == 0
    counts = cnt[0, :N_CLASSES].astype(jnp.int32)
    padded = (counts + bm - 1) // bm * bm
    seg_end = jnp.cumsum(padded).astype(jnp.int32)
    seg_start = seg_end - padded
    n_blocks = n_tok // bm + N_CLASSES
    block_start = jnp.arange(n_blocks, dtype=jnp.int32) * bm
    blk_cls = jnp.minimum(
        jnp.sum((seg_end[None, :] <= block_start[:, None]).astype(jnp.int32), axis=1), N_CLASSES - 1)
    cls_a, cls_b = _class_experts()
    blk_a = jnp.asarray(cls_a)[blk_cls]
    idx = jnp.arange(n_blocks, dtype=jnp.int32)
    changed = jnp.concatenate([jnp.zeros((1,), jnp.int32), (blk_cls[1:] != blk_cls[:-1]).astype(jnp.int32)])
    blk_slot = jnp.cumsum(changed).astype(jnp.int32) % 2
    later = jnp.where(blk_cls[None, :] > blk_cls[:, None], blk_cls[None, :], N_CLASSES)
    next_cls = jnp.minimum(jnp.min(later, axis=1), N_CLASSES - 1)
    earlier = jnp.max(jnp.where(blk_cls[None, :] < blk_cls[:, None], blk_cls[None, :], -1), axis=1)
    prev_cls = jnp.where(earlier >= 0, earlier, blk_cls)
    same = blk_cls[None, :] == blk_cls[:, None]
    pos_in_cls = idx - jnp.min(jnp.where(same, idx[None, :], n_blocks), axis=1)
    last_pos = jnp.sum(same.astype(jnp.int32), axis=1) - 1
    b_now = jnp.asarray(cls_b)[blk_cls]
    b_next = jnp.asarray(cls_b)[next_cls]
    b_prev = jnp.asarray(cls_b)[prev_cls]
    blk_b_slots = [
        jnp.where(blk_slot == s, b_now, jnp.where(pos_in_cls >= jnp.minimum(k, last_pos), b_next, b_prev))
        for s in (0, 1) for k in range(3)]
    n_used = (seg_end[-1:] // bm).astype(jnp.int32)

    lane_shape = (n_tok // DEST_LANES, DEST_LANES)
    dest = _dest(seg_start, cr[0].reshape(lane_shape), cr[1].reshape(lane_shape)).reshape(n_tok)

    xs = _sc_scatter_rows(rows, dest, n_blocks * bm, SC_WIN)
    ys = _experts(xs, blk_a, blk_b_slots, blk_slot, n_used, *expert_w, bm)

    l2g = p["ln2_g"].reshape(1, D)
    l2b = p["ln2_b"].reshape(1, D)
    tp = nbp * s_len
    f_p = _sc_gather_rows(ys, dest[:tp], SC_WIN)
    f_s = _sc_gather_rows(ys, dest[tp:], SC_WIN)
    y_p = _final(x1, f_p, mod, l2g, l2b, 0, nbp, alpha, FINAL_TK)
    y_s = _final(x1, f_s, mod, l2g, l2b, nbp, nbs, alpha, FINAL_TK)
    return y_p, y_s


_PARAM_NAMES = ("w_mod", "b_mod", "w_in", "b_in", "w_spatial", "b_spatial", "sgu_g", "sgu_b", "w_pool",
                "pool_scale", "w_branch_a", "w_branch_b", "w_out", "ln1_g", "ln1_b", "w_router_group",
                "b_router_group", "w_router_expert", "b_router_expert", "w_exp_gate", "w_exp_up",
                "w_exp_down", "ln2_g", "ln2_b")


def kernel(x_prompt, x_sample, c_prompt, c_sample, w_mod, b_mod, w_in, b_in, w_spatial, b_spatial, sgu_g, sgu_b, w_pool, pool_scale, w_branch_a, w_branch_b, w_out, ln1_g, ln1_b, w_router_group, b_router_group, w_router_expert, b_router_expert, w_exp_gate, w_exp_up, w_exp_down, ln2_g, ln2_b):
    params = (w_mod, b_mod, w_in, b_in, w_spatial, b_spatial, sgu_g, sgu_b, w_pool, pool_scale,
              w_branch_a, w_branch_b, w_out, ln1_g, ln1_b, w_router_group, b_router_group,
              w_router_expert, b_router_expert, w_exp_gate, w_exp_up, w_exp_down, ln2_g, ln2_b)
    depth = w_mod.shape[0]
    alpha = (2.0 * depth) ** 0.25
    c_all = jnp.concatenate([c_prompt, c_sample], axis=0)
    y_p, y_s = x_prompt, x_sample
    for l in range(depth):
        p = {name: w[l] for name, w in zip(_PARAM_NAMES, params)}
        y_p, y_s = _layer(y_p, y_s, c_all, p, alpha)
    return (y_p, y_s)
```

```python
import functools
import math

import jax
import jax.numpy as jnp
import numpy as np
from jax import lax
from jax.experimental import pallas as pl
from jax.experimental.pallas import tpu as pltpu
from jax.experimental.pallas import tpu_sc as plsc

F32 = jnp.float32
BF16 = jnp.bfloat16
U32 = jnp.uint32

D = 1024
CHUNK = 128
SGU_HEADS = 8
HEAD_DIM = D // SGU_HEADS
POOL_WINDOWS = (2, 4, 8, 16)
POOL_GROUP_DIM = D // len(POOL_WINDOWS)
N_MOD = 6
N_GROUPS = 4
EPG = 8
N_EXPERTS = N_GROUPS * EPG
D_EXPERT = D // 2
LN_EPS = 1e-5

HALO = 16
COLB = 256
ROUTER_LANES = 128
EXPERT_LANE0 = N_GROUPS
PAIRS = EPG * (EPG - 1) // 2
N_CLASSES = N_GROUPS * PAIRS
ROW_W = D // 2 + ROUTER_LANES
MIXER_TS = 512
EXPERT_BM = 384
FINAL_TK = 2048
DEST_LANES = 512
PACK_BLOCK = 1024
SC_WIN = 128
SC_LANES = 16
SC_PACK_CHUNK = 32
SC_PACK_UNROLL = 8
VMEM_LIMIT = 56 * 1024 * 1024


def _ln(x):
    mu = jnp.mean(x, axis=-1, keepdims=True)
    xc = x - mu
    var = jnp.mean(xc * xc, axis=-1, keepdims=True)
    return xc * lax.rsqrt(var + LN_EPS)


_GELU_A = -2.0 * math.sqrt(2.0 / math.pi) * math.log2(math.e)
_GELU_B = _GELU_A * 0.044715


def _gelu_tanh(x):
    return x / (1.0 + jnp.exp2(x * (_GELU_A + _GELU_B * (x * x))))


def _dot(a, b):
    return jnp.dot(a, b, preferred_element_type=F32)


def _mod_kernel(c_ref, w_ref, b_ref, o_ref):
    a = jax.nn.silu(c_ref[...]).astype(BF16)
    o_ref[...] = _dot(a, w_ref[...].astype(BF16)) + b_ref[...]


def _modulation(c_all, w_mod, b_mod):
    nb = c_all.shape[0]
    n_out = w_mod.shape[1]
    cb = 512
    return pl.pallas_call(
        _mod_kernel,
        grid=(n_out // cb,),
        in_specs=[pl.BlockSpec((nb, D), lambda j: (0, 0)),
                  pl.BlockSpec((D, cb), lambda j: (0, j)),
                  pl.BlockSpec((1, cb), lambda j: (0, j))],
        out_specs=pl.BlockSpec((nb, cb), lambda j: (0, j)),
        out_shape=jax.ShapeDtypeStruct((nb, n_out), F32),
        name="mod",
    )(c_all, w_mod, b_mod.reshape(1, n_out))


def _pack_kernel(w_ref, o_ref):
    o_ref[...] = pltpu.bitcast(w_ref[...].astype(BF16), U32)


def _pack_rows(w):
    *lead, k, n = w.shape
    rows = math.prod(lead) * k
    rb = min(rows, PACK_BLOCK)
    nb = min(n, PACK_BLOCK)
    assert k % 2 == 0 and rows % rb == 0 and n % nb == 0
    packed = pl.pallas_call(
        _pack_kernel,
        grid=(rows // rb, n // nb),
        in_specs=[pl.BlockSpec((rb, nb), lambda i, j: (i, j))],
        out_specs=pl.BlockSpec((rb // 2, nb), lambda i, j: (i, j)),
        out_shape=jax.ShapeDtypeStruct((rows // 2, n), U32),
        name="pack",
    )(w.reshape(rows, n))
    return packed.reshape(*lead, k // 2, n)


def _mixer_kernel(nbp, ts, s_len, n_i, alpha,
                  xp_ref, xs_ref, xpp_ref, xsp_ref, xpn_ref, xsn_ref, mod_ref, modb_ref, icnt_ref,
                  win_ref, bin_ref, ws_ref, bsf_ref, sg_ref, sb_ref, wpool_ref, psc_ref,
                  wa_ref, wb_ref, wo_ref, l1g_ref, l1b_ref, wr_ref, br_ref, tri_ref,
                  x1_ref, h2u_ref, cr_ref, cnt_ref,
                  h_scr, u_scr, v_scr, vb_scr, a_scr, p_scr, m_scr, h2_scr, y_scr, xk_scr, ga_scr, gb_scr,
                  carry_scr):
    t = pl.program_id(0)
    n_tiles = pl.num_programs(0) - 1
    tf = jnp.minimum(t, n_tiles - 1)
    b = tf // n_i
    i = tf % n_i
    is_p = b < nbp
    ncb = D // COLB
    n_ext = ts + 2 * HALO
    half = D // 2
    main = slice(HALO, HALO + ts)

    def wmat(ref, cols=slice(None)):
        return pltpu.bitcast(ref[:, cols], BF16)

    @pl.when(t == 0)
    def _():
        carry_scr[...] = jnp.zeros_like(carry_scr)
        m_scr[...] = jnp.zeros_like(m_scr)
        xk_scr[...] = jnp.zeros_like(xk_scr)

    def row_rstd(read, mu):
        ss = jnp.zeros((ts, 1), F32)
        for j in range(ncb):
            xc = read(slice(j * COLB, (j + 1) * COLB)) - mu
            ss = ss + jnp.sum(xc * xc, axis=-1, keepdims=True)
        return lax.rsqrt(ss * (1.0 / D) + LN_EPS)

    modb = modb_ref[0]
    gate1_b, shift2, scale2 = modb[2:3], modb[3:4], modb[4:5]
    ysum = jnp.zeros((ts, 1), F32)
    for j in range(ncb):
        cols = slice(j * COLB, (j + 1) * COLB)
        tmix = _dot(m_scr[...], wmat(wo_ref, cols))
        y = alpha * xk_scr[:, cols] + gate1_b[:, cols] * tmix
        y_scr[:, cols] = y
        ysum = ysum + jnp.sum(y, axis=-1, keepdims=True)

    mod = mod_ref[0]
    shift1, scale1 = mod[0:1], mod[1:2]

    def adaln1(xv):
        return (_ln(xv) * (1.0 + scale1) + shift1).astype(BF16)

    h_scr[0:HALO, :] = adaln1(jnp.where(is_p, xpp_ref[0], xsp_ref[0]))
    h_scr[main, :] = adaln1(jnp.where(is_p, xp_ref[0], xs_ref[0]))
    h_scr[HALO + ts:, :] = adaln1(jnp.where(is_p, xpn_ref[0], xsn_ref[0]))

    def proj(hrows, c0):
        return _dot(h_scr[hrows, :], wmat(win_ref, slice(c0, c0 + COLB))) + bin_ref[:, c0:c0 + COLB]

    ymu = ysum * (1.0 / D)
    yrs = row_rstd(lambda cols: y_scr[:, cols], ymu)
    xsum = jnp.zeros((ts, 1), F32)
    for j in range(ncb):
        cols = slice(j * COLB, (j + 1) * COLB)
        x1 = (y_scr[:, cols] - ymu) * yrs * l1g_ref[:, cols] + l1b_ref[:, cols]
        x1_ref[0, :, cols] = x1
        xsum = xsum + jnp.sum(x1, axis=-1, keepdims=True)
    xmu = xsum * (1.0 / D)
    xrs = row_rstd(lambda cols: x1_ref[0, :, cols], xmu)
    for j in range(ncb):
        cols = slice(j * COLB, (j + 1) * COLB)
        h2 = (x1_ref[0, :, cols] - xmu) * xrs * (1.0 + scale2[:, cols]) + shift2[:, cols]
        h2_scr[:, cols] = h2.astype(BF16)
    lo_bits = lax.bitcast_convert_type(h2_scr[:, 0:half].astype(F32), U32)
    hi_bits = lax.bitcast_convert_type(h2_scr[:, half:D].astype(F32), U32)
    h2u_ref[:, 0:half] = (lo_bits >> 16) | (hi_bits & jnp.uint32(0xFFFF0000))

    vsum = jnp.zeros((ts, 1), F32)
    for j in range(ncb):
        cols = slice(j * COLB, (j + 1) * COLB)
        u_scr[:, cols] = _gelu_tanh(proj(main, j * COLB))
        gv = _gelu_tanh(proj(main, D + j * COLB))
        v_scr[:, cols] = gv
        vsum = vsum + jnp.sum(gv, axis=-1, keepdims=True)

    logits = _dot(h2_scr[...], wmat(wr_ref)) + br_ref[...]
    lane = lax.broadcasted_iota(jnp.int32, (ts, ROUTER_LANES), 1)
    lane_f = lane.astype(F32)
    neg = -jnp.inf
    is_g = lane < N_GROUPS
    lg = jnp.where(is_g, logits, neg)
    mg = jnp.max(lg, axis=-1, keepdims=True)
    g_sel = jnp.min(jnp.where(lg == mg, lane_f, float(ROUTER_LANES)), axis=-1, keepdims=True)
    p_sel = 1.0 / jnp.sum(jnp.where(is_g, jnp.exp(logits - mg), 0.0), axis=-1, keepdims=True)
    e_lo = EXPERT_LANE0 + g_sel * EPG
    in_grp = (lane_f >= e_lo) & (lane_f < e_lo + EPG)
    le = jnp.where(in_grp, logits, neg)
    m1 = jnp.max(le, axis=-1, keepdims=True)
    i1 = jnp.min(jnp.where(le == m1, lane_f, float(ROUTER_LANES)), axis=-1, keepdims=True)
    le2 = jnp.where(lane_f == i1, neg, le)
    m2 = jnp.max(le2, axis=-1, keepdims=True)
    i2 = jnp.min(jnp.where(le2 == m2, lane_f, float(ROUTER_LANES)), axis=-1, keepdims=True)
    e2x = jnp.exp(m2 - m1)
    den = 1.0 + e2x
    g1 = p_sel / den
    g2 = p_sel * e2x / den

    j1 = i1 - e_lo
    j2 = i2 - e_lo
    first_is_a = j1 < j2
    ja = jnp.minimum(j1, j2)
    jb = jnp.maximum(j1, j2)
    cls = g_sel * PAIRS + (EPG - 1) * ja - ja * (ja - 1.0) * 0.5 + (jb - ja - 1.0)
    w_a = jnp.where(first_is_a, g1, g2)
    w_b = jnp.where(first_is_a, g2, g1)
    extra = jnp.where(lane == 0, w_a, jnp.where(lane == 1, w_b, 0.0))
    h2u_ref[:, half:half + ROUTER_LANES] = lax.bitcast_convert_type(extra, U32)

    hit = lane_f == cls
    onehot = jnp.where(hit, (t > 0).astype(F32), 0.0)

    def finish_ranks():
        pre = _dot(tri_ref[...], onehot.astype(BF16)) + carry_scr[0:1, :]
        rank = jnp.sum(jnp.where(hit, pre, 0.0), axis=-1, keepdims=True)
        carry_scr[0:1, :] = carry_scr[0:1, :] + jnp.sum(onehot, axis=0, keepdims=True)
        cnt_ref[...] = carry_scr[...]
        cr = jnp.where(lane == 0, cls, jnp.where(lane == 1, rank, 0.0))
        cr_ref[...] = cr.T[0:8, :]

    vmu = vsum * (1.0 / D)
    vrs = row_rstd(lambda cols: v_scr[:, cols], vmu)
    for j in range(ncb):
        cols = slice(j * COLB, (j + 1) * COLB)
        vb_scr[:, cols] = ((v_scr[:, cols] - vmu) * vrs * sg_ref[:, cols] + sb_ref[:, cols]).astype(BF16)

    def pool_project(gi):
        ext_pos = lax.broadcasted_iota(jnp.int32, (n_ext, 1), 0) + (i * ts - HALO)
        ext_valid = (ext_pos >= 0) & (ext_pos < s_len)
        return jnp.where(ext_valid, proj(slice(None), 2 * D + gi * POOL_GROUP_DIM), 0.0)

    def pool_group(gi, zp):
        w = POOL_WINDOWS[gi]
        cols = slice(gi * POOL_GROUP_DIM, (gi + 1) * POOL_GROUP_DIM)
        acc = zp + pltpu.roll(zp, 1, 0)
        if w >= 4:
            acc = pltpu.roll(acc, 1, 0) + pltpu.roll(acc, n_ext - 1, 0)
        if w >= 8:
            acc = pltpu.roll(acc, 2, 0) + pltpu.roll(acc, n_ext - 2, 0)
        if w >= 16:
            acc = pltpu.roll(acc, 4, 0) + pltpu.roll(acc, n_ext - 4, 0)
        inv_count = icnt_ref[:, gi:gi + 1]
        dd = (acc[main] * inv_count - zp[main]).astype(BF16)
        p_scr[:, cols] = (_dot(dd, pltpu.bitcast(wpool_ref[gi], BF16)) * psc_ref[:, cols]).astype(BF16)

    def sgu_chunk(c):
        crow = slice(c * CHUNK, (c + 1) * CHUNK)
        for hh in range(SGU_HEADS):
            cols = slice(hh * HEAD_DIM, (hh + 1) * HEAD_DIM)
            mixed = _dot(ws_ref[hh], vb_scr[crow, cols]) + bsf_ref[:, cols]
            a_scr[crow, cols] = (u_scr[crow, cols] * mixed).astype(BF16)

    assert len(POOL_WINDOWS) == ncb
    for j in range(ncb):
        cols = slice(j * COLB, (j + 1) * COLB)
        zp = pool_project(j)
        ga_scr[:, cols] = jax.nn.sigmoid(proj(main, 3 * D + j * COLB))
        gb_scr[:, cols] = jax.nn.sigmoid(proj(main, 4 * D + j * COLB))
        pool_group(j, zp)
    for j in range(ncb):
        cols = slice(j * COLB, (j + 1) * COLB)
        gb_scr[:, cols] = gb_scr[:, cols] * _dot(p_scr[...], wmat(wb_ref, cols))
    for c in range(ts // CHUNK):
        sgu_chunk(c)

    finish_ranks()

    for j in range(ncb):
        cols = slice(j * COLB, (j + 1) * COLB)
        ta = _dot(a_scr[...], wmat(wa_ref, cols))
        m_scr[:, cols] = (ga_scr[:, cols] * ta + gb_scr[:, cols]).astype(BF16)
    xk_scr[...] = jnp.where(is_p, xp_ref[0], xs_ref[0])


def _const_spec(shape):
    nd = len(shape)
    return pl.BlockSpec(shape, lambda t: (0,) * nd, pipeline_mode=pl.Buffered(1))


def _mixer(x_p, x_s, mod, wts, alpha, ts):
    nbp, s_len, _ = x_p.shape
    nbs = x_s.shape[0]
    assert x_s.shape[1] == s_len and s_len % ts == 0 and ts % CHUNK == 0
    nb = nbp + nbs
    n_i = s_len // ts
    hb = ts // HALO
    n_hb = s_len // HALO

    n_tiles = nb * n_i

    def front(t):
        tf = jnp.minimum(t, n_tiles - 1)
        return tf // n_i, tf % n_i

    def back(t):
        return jnp.maximum(t - 1, 0)

    def pick(b, on_p, val, const):
        return jnp.where(b < nbp if on_p else b >= nbp, val, const)

    def main_map(on_p):
        def f(t):
            b, i = front(t)
            bb = pick(b, on_p, b if on_p else b - nbp, nbp - 1 if on_p else 0)
            ii = pick(b, on_p, i, n_i - 1 if on_p else 0)
            return (bb, ii, 0)
        return f

    def halo_map(on_p, nxt):
        def f(t):
            b, i = front(t)
            bb = pick(b, on_p, b if on_p else b - nbp, nbp - 1 if on_p else 0)
            idx = jnp.minimum((i + 1) * hb, n_hb - 1) if nxt else jnp.maximum(i * hb - 1, 0)
            ii = pick(b, on_p, idx, n_hb - 1 if on_p else 0)
            return (bb, ii, 0)
        return f

    tri = (lax.broadcasted_iota(jnp.int32, (ts, ts), 1)
           < lax.broadcasted_iota(jnp.int32, (ts, ts), 0)).astype(BF16)
    consts = list(wts) + [tri]
    in_specs = [
        pl.BlockSpec((1, ts, D), main_map(True)),
        pl.BlockSpec((1, ts, D), main_map(False)),
        pl.BlockSpec((1, HALO, D), halo_map(True, False)),
        pl.BlockSpec((1, HALO, D), halo_map(False, False)),
        pl.BlockSpec((1, HALO, D), halo_map(True, True)),
        pl.BlockSpec((1, HALO, D), halo_map(False, True)),
        pl.BlockSpec((1, N_MOD, D), lambda t: (front(t)[0], 0, 0)),
        pl.BlockSpec((1, N_MOD, D), lambda t: (back(t) // n_i, 0, 0)),
        pl.BlockSpec((ts, ROUTER_LANES), lambda t: (front(t)[1], 0)),
    ] + [_const_spec(w.shape) for w in consts]

    pos = jnp.arange(s_len, dtype=jnp.int32)[:, None]
    win = jnp.asarray(POOL_WINDOWS + (1,) * (ROUTER_LANES - len(POOL_WINDOWS)), jnp.int32)[None, :]
    inv_count = 1.0 / (jnp.minimum(pos + (win - 1 - win // 2), s_len - 1)
                       - jnp.maximum(pos - win // 2, 0) + 1).astype(F32)
    n_tok = nb * s_len
    out_shape = (
        jax.ShapeDtypeStruct((nb, s_len, D), F32),
        jax.ShapeDtypeStruct((n_tok, ROW_W), U32),
        jax.ShapeDtypeStruct((8, n_tok), F32),
        jax.ShapeDtypeStruct((8, ROUTER_LANES), F32),
    )
    out_specs = (
        pl.BlockSpec((1, ts, D), lambda t: (back(t) // n_i, back(t) % n_i, 0)),
        pl.BlockSpec((ts, ROW_W), lambda t: (back(t), 0)),
        pl.BlockSpec((8, ts), lambda t: (0, back(t))),
        pl.BlockSpec((8, ROUTER_LANES), lambda t: (0, 0)),
    )
    scratch = [
        pltpu.VMEM((ts + 2 * HALO, D), BF16),
        pltpu.VMEM((ts, D), F32),
        pltpu.VMEM((ts, D), F32),
        pltpu.VMEM((ts, D), BF16),
        pltpu.VMEM((ts, D), BF16),
        pltpu.VMEM((ts, D), BF16),
        pltpu.VMEM((ts, D), BF16),
        pltpu.VMEM((ts, D), BF16),
        pltpu.VMEM((ts, D), F32),
        pltpu.VMEM((ts, D), F32),
        pltpu.VMEM((ts, D), F32),
        pltpu.VMEM((ts, D), F32),
        pltpu.VMEM((8, ROUTER_LANES), F32),
    ]
    return pl.pallas_call(
        functools.partial(_mixer_kernel, nbp, ts, s_len, n_i, alpha),
        grid=(n_tiles + 1,),
        in_specs=in_specs,
        out_specs=out_specs,
        out_shape=out_shape,
        scratch_shapes=scratch,
        compiler_params=pltpu.CompilerParams(
            dimension_semantics=("arbitrary",), vmem_limit_bytes=VMEM_LIMIT),
        name="mixer",
    )(x_p, x_s, x_p, x_s, x_p, x_s, mod, mod, inv_count, *consts)


def _dest_kernel(seg_ref, cls_ref, rank_ref, o_ref):
    cls = cls_ref[...]
    acc = rank_ref[...].astype(jnp.int32)
    for c in range(N_CLASSES):
        acc = acc + jnp.where(cls == float(c), seg_ref[c], 0)
    o_ref[...] = acc


def _dest(seg_start, cls, rank):
    shape = cls.shape
    return pl.pallas_call(
        _dest_kernel,
        grid_spec=pltpu.PrefetchScalarGridSpec(
            num_scalar_prefetch=1,
            grid=(1,),
            in_specs=[pl.BlockSpec(shape, lambda i, s: (0, 0)), pl.BlockSpec(shape, lambda i, s: (0, 0))],
            out_specs=pl.BlockSpec(shape, lambda i, s: (0, 0)),
        ),
        out_shape=jax.ShapeDtypeStruct(shape, jnp.int32),
        name="dest",
    )(seg_start, cls, rank)


def _sc_mesh():
    return plsc.VectorSubcoreMesh(core_axis_name="core", subcore_axis_name="subcore")


def _sc_scatter_rows(rows, dest, n_rows, win):
    n_tok, row_w = rows.shape
    mesh = _sc_mesh()
    n_workers = mesh.num_cores * mesh.num_subcores
    n_steps = n_tok // win
    assert n_tok % win == 0 and n_steps % n_workers == 0
    per_worker = n_steps // n_workers

    @pl.kernel(out_type=jax.ShapeDtypeStruct((n_rows, row_w), rows.dtype), mesh=mesh,
               scratch_types=[pltpu.VMEM((win,), jnp.int32), pltpu.VMEM((win, row_w), rows.dtype)])
    def scatter(rows_hbm, idx_hbm, out_hbm, idx_vmem, buf):
        worker = lax.axis_index("core") * mesh.num_subcores + lax.axis_index("subcore")

        @pl.loop(0, per_worker)
        def _(k):
            step = worker * per_worker + k
            pltpu.sync_copy(idx_hbm.at[step], idx_vmem)
            pltpu.sync_copy(rows_hbm.at[pl.ds(step * win, win)], buf)
            pltpu.sync_copy(buf, out_hbm.at[idx_vmem])

    return scatter(rows, dest.reshape(n_steps, win))


def _sc_gather_rows(table, idx, win):
    n_tok = idx.shape[0]
    row_w = table.shape[1]
    mesh = _sc_mesh()
    n_workers = mesh.num_cores * mesh.num_subcores
    n_steps = n_tok // win
    assert n_tok % win == 0 and n_steps % n_workers == 0
    per_worker = n_steps // n_workers

    @pl.kernel(out_type=jax.ShapeDtypeStruct((n_tok, row_w), table.dtype), mesh=mesh,
               scratch_types=[pltpu.VMEM((win,), jnp.int32), pltpu.VMEM((win, row_w), table.dtype)])
    def gather(table_hbm, idx_hbm, out_hbm, idx_vmem, buf):
        worker = lax.axis_index("core") * mesh.num_subcores + lax.axis_index("subcore")

        @pl.loop(0, per_worker)
        def _(k):
            step = worker * per_worker + k
            pltpu.sync_copy(idx_hbm.at[step], idx_vmem)
            pltpu.sync_copy(table_hbm.at[idx_vmem], buf)
            pltpu.sync_copy(buf, out_hbm.at[pl.ds(step * win, win)])

    return gather(table, idx.reshape(n_steps, win))


def _sc_pack_rows(w, chunk):
    *lead, k, n = w.shape
    rows = math.prod(lead) * k
    mesh = _sc_mesh()
    lanes = SC_LANES
    n_workers = mesh.num_cores * mesh.num_subcores
    assert k % 2 == 0 and rows % (n_workers * chunk) == 0 and chunk % 2 == 0 and n % (lanes * SC_PACK_UNROLL) == 0
    per_worker = rows // (n_workers * chunk)

    def to_bf16_bits(x):
        bits = lax.bitcast_convert_type(x, U32)
        return (bits + jnp.uint32(0x7FFF) + ((bits >> 16) & jnp.uint32(1))) >> 16

    @pl.kernel(out_type=jax.ShapeDtypeStruct((rows // 2, n), U32), mesh=mesh,
               scratch_types=[pltpu.VMEM((chunk, n), F32), pltpu.VMEM((chunk // 2, n), U32)],
               compiler_params=pltpu.CompilerParams(needs_layout_passes=False))
    def pack(w_hbm, out_hbm, src, dst):
        worker = lax.axis_index("core") * mesh.num_subcores + lax.axis_index("subcore")

        @pl.loop(0, per_worker)
        def _(step):
            blk = worker * per_worker + step
            row0 = pl.multiple_of(blk * chunk, chunk)
            out_row0 = pl.multiple_of(blk * (chunk // 2), chunk // 2)
            pltpu.sync_copy(w_hbm.at[pl.ds(row0, chunk)], src)

            @pl.loop(0, chunk // 2)
            def _(q):
                @pl.loop(0, n, step=lanes * SC_PACK_UNROLL)
                def _(c0):
                    for u in range(SC_PACK_UNROLL):
                        c = c0 + u * lanes
                        lo = to_bf16_bits(src[2 * q, pl.ds(c, lanes)])
                        hi = to_bf16_bits(src[2 * q + 1, pl.ds(c, lanes)])
                        dst[q, pl.ds(c, lanes)] = lo | (hi << 16)

            pltpu.sync_copy(dst, out_hbm.at[pl.ds(out_row0, chunk // 2)])

    return pack(w.reshape(rows, n)).reshape(*lead, k // 2, n)


def _expert_kernel(bm, ea_ref, g0_ref, u0_ref, d0_ref, g1_ref, u1_ref, d1_ref, slot_ref, nused_ref,
                   xs_ref, wga_ref, wua_ref, wda_ref,
                   wgb0_ref, wub0_ref, wdb0_ref, wgb1_ref, wub1_ref, wdb1_ref, ys_ref, act_scr, y_scr):
    j = pl.program_id(0)
    n_used = nused_ref[0]
    half = D // 2
    b_slots = ((wgb0_ref, wub0_ref, wdb0_ref), (wgb1_ref, wub1_ref, wdb1_ref))

    def wmat(ref, k0, k1, cols):
        return pltpu.bitcast(ref[0, k0 // 2:k1 // 2, cols], BF16)

    def store_previous():
        for c0 in range(0, half, COLB):
            lo_bits = lax.bitcast_convert_type(y_scr[:, c0:c0 + COLB].astype(BF16).astype(F32), U32)
            hi_bits = lax.bitcast_convert_type(y_scr[:, half + c0:half + c0 + COLB].astype(BF16).astype(F32), U32)
            ys_ref[:, c0:c0 + COLB] = (lo_bits >> 16) | (hi_bits & jnp.uint32(0xFFFF0000))

    def compute(slot):
        wgb_ref, wub_ref, wdb_ref = b_slots[slot]
        bits = xs_ref[:, 0:half]
        lo = lax.bitcast_convert_type(bits << 16, F32).astype(BF16)
        hi = lax.bitcast_convert_type(bits & jnp.uint32(0xFFFF0000), F32).astype(BF16)
        wts = lax.bitcast_convert_type(xs_ref[:, half:half + ROUTER_LANES], F32)
        for e, (wg_ref, wu_ref) in enumerate(((wga_ref, wua_ref), (wgb_ref, wub_ref))):
            for c in range(D_EXPERT // COLB):
                cols = slice(c * COLB, (c + 1) * COLB)
                g = _dot(lo, wmat(wg_ref, 0, half, cols)) + _dot(hi, wmat(wg_ref, half, D, cols))
                up = _dot(lo, wmat(wu_ref, 0, half, cols)) + _dot(hi, wmat(wu_ref, half, D, cols))
                act_scr[e, :, cols] = (jax.nn.silu(g) * up).astype(BF16)
        w_a = wts[:, 0:1]
        w_b = wts[:, 1:2]

        for c0 in range(0, D, COLB):
            cols = slice(c0, c0 + COLB)
            y_scr[:, cols] = (w_a * _dot(act_scr[0], wmat(wda_ref, 0, D_EXPERT, cols))
                              + w_b * _dot(act_scr[1], wmat(wdb_ref, 0, D_EXPERT, cols)))

    @pl.when(j == 0)
    def _():
        compute(0)

    for slot in (0, 1):
        @pl.when((j > 0) & (j < n_used) & (slot_ref[j] == slot))
        def _(slot=slot):
            store_previous()
            compute(slot)

    @pl.when(j == n_used)
    def _():
        store_previous()


def _experts(xs, blk_a, blk_b_slots, blk_slot, n_used, w_gate, w_up, w_down, bm):
    n_rows = xs.shape[0]
    n_blocks = n_rows // bm
    n_prefetch = 3 + len(blk_b_slots)

    def x_map(j, *pre):
        return (jnp.minimum(j, pre[-1][0] - 1), 0)

    def y_map(j, *pre):
        return (jnp.clip(j - 1, 0, pre[-1][0] - 1), 0)

    def w_map(k):
        return lambda j, *pre: (pre[k][j], 0, 0)

    gate_up = (1, D // 2, D_EXPERT)
    down = (1, D_EXPERT // 2, D)
    return pl.pallas_call(
        functools.partial(_expert_kernel, bm),
        grid_spec=pltpu.PrefetchScalarGridSpec(
            num_scalar_prefetch=n_prefetch,
            grid=(n_used[0] + 1,),
            in_specs=[pl.BlockSpec((bm, ROW_W), x_map),
                      pl.BlockSpec(gate_up, w_map(0)), pl.BlockSpec(gate_up, w_map(0)), pl.BlockSpec(down, w_map(0)),
                      pl.BlockSpec(gate_up, w_map(1)), pl.BlockSpec(gate_up, w_map(2)), pl.BlockSpec(down, w_map(3)),
                      pl.BlockSpec(gate_up, w_map(4)), pl.BlockSpec(gate_up, w_map(5)), pl.BlockSpec(down, w_map(6))],
            out_specs=pl.BlockSpec((bm, D // 2), y_map),
            scratch_shapes=[pltpu.VMEM((2, bm, D_EXPERT), BF16), pltpu.VMEM((bm, D), F32)],
        ),
        out_shape=jax.ShapeDtypeStruct((n_rows, D // 2), U32),
        compiler_params=pltpu.CompilerParams(
            dimension_semantics=("arbitrary",), vmem_limit_bytes=VMEM_LIMIT),
        name="experts",
    )(blk_a, *blk_b_slots, blk_slot, n_used, xs, w_gate, w_up, w_down, w_gate, w_up, w_down,
      w_gate, w_up, w_down)


def _final_kernel(alpha, x1_ref, f_ref, mod_ref, l2g_ref, l2b_ref, out_ref):
    half = D // 2
    gate2 = mod_ref[0][5:6]
    bits = f_ref[...]
    f_lo = lax.bitcast_convert_type(bits << 16, F32)
    f_hi = lax.bitcast_convert_type(bits & jnp.uint32(0xFFFF0000), F32)
    y_lo = alpha * x1_ref[0, :, 0:half] + gate2[:, 0:half] * f_lo
    y_hi = alpha * x1_ref[0, :, half:D] + gate2[:, half:D] * f_hi
    mu = (jnp.sum(y_lo, axis=-1, keepdims=True) + jnp.sum(y_hi, axis=-1, keepdims=True)) * (1.0 / D)
    c_lo = y_lo - mu
    c_hi = y_hi - mu
    var = (jnp.sum(c_lo * c_lo, axis=-1, keepdims=True) + jnp.sum(c_hi * c_hi, axis=-1, keepdims=True)) * (1.0 / D)
    rs = lax.rsqrt(var + LN_EPS)
    out_ref[0, :, 0:half] = c_lo * rs * l2g_ref[:, 0:half] + l2b_ref[:, 0:half]
    out_ref[0, :, half:D] = c_hi * rs * l2g_ref[:, half:D] + l2b_ref[:, half:D]


def _final(x1, f, mod, ln2_g, ln2_b, b_off, nbg, alpha, tk):
    s_len = x1.shape[1]
    n_i = s_len // tk
    return pl.pallas_call(
        functools.partial(_final_kernel, alpha),
        grid=(nbg, n_i),
        in_specs=[pl.BlockSpec((1, tk, D), lambda b, i: (b + b_off, i, 0)),
                  pl.BlockSpec((tk, D // 2), lambda b, i: (b * n_i + i, 0)),
                  pl.BlockSpec((1, N_MOD, D), lambda b, i: (b + b_off, 0, 0)),
                  pl.BlockSpec((1, D), lambda b, i: (0, 0)),
                  pl.BlockSpec((1, D), lambda b, i: (0, 0))],
        out_specs=pl.BlockSpec((1, tk, D), lambda b, i: (b, i, 0)),
        out_shape=jax.ShapeDtypeStruct((nbg, s_len, D), F32),
        compiler_params=pltpu.CompilerParams(
            dimension_semantics=("arbitrary", "arbitrary"), vmem_limit_bytes=VMEM_LIMIT),
        name="final",
    )(x1, f, mod, ln2_g, ln2_b)


def _class_experts():
    ea, eb = [], []
    for g in range(N_GROUPS):
        for a in range(EPG):
            for b in range(a + 1, EPG):
                ea.append(g * EPG + a)
                eb.append(g * EPG + b)
    return np.asarray(ea, np.int32), np.asarray(eb, np.int32)


def _layer(x_p, x_s, c_all, p, alpha):
    nbp, s_len, _ = x_p.shape
    nbs = x_s.shape[0]
    nb = nbp + nbs
    n_tok = nb * s_len

    mod = _modulation(c_all, p["w_mod"], p["b_mod"]).reshape(nb, N_MOD, D)

    bsf = jnp.repeat(p["b_spatial"].T, HEAD_DIM, axis=1)
    n_in = p["w_in"].shape[1]
    w_router = jnp.concatenate(
        [p["w_router_group"], p["w_router_expert"],
         jnp.zeros((D, ROUTER_LANES - N_GROUPS - N_EXPERTS), F32)], axis=1)
    b_router = jnp.concatenate(
        [p["b_router_group"], p["b_router_expert"],
         jnp.zeros((ROUTER_LANES - N_GROUPS - N_EXPERTS,), F32)]).reshape(1, ROUTER_LANES)
    wts = [
        _pack_rows(p["w_in"]), p["b_in"].reshape(1, n_in),
        p["w_spatial"].astype(BF16), bsf,
        p["sgu_g"].reshape(1, D), p["sgu_b"].reshape(1, D),
        _pack_rows(p["w_pool"]), p["pool_scale"].reshape(1, D),
        _pack_rows(p["w_branch_a"]), _pack_rows(p["w_branch_b"]), _pack_rows(p["w_out"]),
        p["ln1_g"].reshape(1, D), p["ln1_b"].reshape(1, D),
        _pack_rows(w_router), b_router,
    ]
    expert_w = [_sc_pack_rows(p[name], SC_PACK_CHUNK) for name in ("w_exp_gate", "w_exp_up", "w_exp_down")]
    x1, rows, cr, cnt = _mixer(x_p, x_s, mod, wts, alpha, MIXER_TS)

    bm = EXPERT_BM
    assert n_tok % bm == 0 and n_tok % DEST_LANES == 0
    counts = cnt[0, :N_CLASSES].astype(jnp.int32)
    padded = (counts + bm - 1) // bm * bm
    seg_end = jnp.cumsum(padded).astype(jnp.int32)
    seg_start = seg_end - padded
    n_blocks = n_tok // bm + N_CLASSES
    block_start = jnp.arange(n_blocks, dtype=jnp.int32) * bm
    blk_cls = jnp.minimum(
        jnp.sum((seg_end[None, :] <= block_start[:, None]).astype(jnp.int32), axis=1), N_CLASSES - 1)
    cls_a, cls_b = _class_experts()
    blk_a = jnp.asarray(cls_a)[blk_cls]
    idx = jnp.arange(n_blocks, dtype=jnp.int32)
    changed = jnp.concatenate([jnp.zeros((1,), jnp.int32), (blk_cls[1:] != blk_cls[:-1]).astype(jnp.int32)])
    blk_slot = jnp.cumsum(changed).astype(jnp.int32) % 2
    later = jnp.where(blk_cls[None, :] > blk_cls[:, None], blk_cls[None, :], N_CLASSES)
    next_cls = jnp.minimum(jnp.min(later, axis=1), N_CLASSES - 1)
    earlier = jnp.max(jnp.where(blk_cls[None, :] < blk_cls[:, None], blk_cls[None, :], -1), axis=1)
    prev_cls = jnp.where(earlier >= 0, earlier, blk_cls)
    same = blk_cls[None, :] == blk_cls[:, None]
    pos_in_cls = idx - jnp.min(jnp.where(same, idx[None, :], n_blocks), axis=1)
    last_pos = jnp.sum(same.astype(jnp.int32), axis=1) - 1
    b_now = jnp.asarray(cls_b)[blk_cls]
    b_next = jnp.asarray(cls_b)[next_cls]
    b_prev = jnp.asarray(cls_b)[prev_cls]
    blk_b_slots = [
        jnp.where(blk_slot == s, b_now, jnp.where(pos_in_cls >= jnp.minimum(k, last_pos), b_next, b_prev))
        for s in (0, 1) for k in range(3)]
    n_used = (seg_end[-1:] // bm).astype(jnp.int32)

    lane_shape = (n_tok // DEST_LANES, DEST_LANES)
    dest = _dest(seg_start, cr[0].reshape(lane_shape), cr[1].reshape(lane_shape)).reshape(n_tok)

    xs = _sc_scatter_rows(rows, dest, n_blocks * bm, SC_WIN)
    ys = _experts(xs, blk_a, blk_b_slots, blk_slot, n_used, *expert_w, bm)

    l2g = p["ln2_g"].reshape(1, D)
    l2b = p["ln2_b"].reshape(1, D)
    tp = nbp * s_len
    f_p = _sc_gather_rows(ys, dest[:tp], SC_WIN)
    f_s = _sc_gather_rows(ys, dest[tp:], SC_WIN)
    y_p = _final(x1, f_p, mod, l2g, l2b, 0, nbp, alpha, FINAL_TK)
    y_s = _final(x1, f_s, mod, l2g, l2b, nbp, nbs, alpha, FINAL_TK)
    return y_p, y_s


_PARAM_NAMES = ("w_mod", "b_mod", "w_in", "b_in", "w_spatial", "b_spatial", "sgu_g", "sgu_b", "w_pool",
                "pool_scale", "w_branch_a", "w_branch_b", "w_out", "ln1_g", "ln1_b", "w_router_group",
                "b_router_group", "w_router_expert", "b_router_expert", "w_exp_gate", "w_exp_up",
                "w_exp_down", "ln2_g", "ln2_b")


def kernel(x_prompt, x_sample, c_prompt, c_sample, w_mod, b_mod, w_in, b_in, w_spatial, b_spatial, sgu_g, sgu_b, w_pool, pool_scale, w_branch_a, w_branch_b, w_out, ln1_g, ln1_b, w_router_group, b_router_group, w_router_expert, b_router_expert, w_exp_gate, w_exp_up, w_exp_down, ln2_g, ln2_b):
    params = (w_mod, b_mod, w_in, b_in, w_spatial, b_spatial, sgu_g, sgu_b, w_pool, pool_scale,
              w_branch_a, w_branch_b, w_out, ln1_g, ln1_b, w_router_group, b_router_group,
              w_router_expert, b_router_expert, w_exp_gate, w_exp_up, w_exp_down, ln2_g, ln2_b)
    depth = w_mod.shape[0]
    alpha = (2.0 * depth) ** 0.25
    c_all = jnp.concatenate([c_prompt, c_sample], axis=0)
    y_p, y_s = x_prompt, x_sample
    for l in range(depth):
        p = {name: w[l] for name, w in zip(_PARAM_NAMES, params)}
        y_p, y_s = _layer(y_p, y_s, c_all, p, alpha)
    return (y_p, y_s)
```

```python
import functools
import math

import jax
import jax.numpy as jnp
import numpy as np
from jax import lax
from jax.experimental import pallas as pl
from jax.experimental.pallas import tpu as pltpu
from jax.experimental.pallas import tpu_sc as plsc

F32 = jnp.float32
BF16 = jnp.bfloat16
U32 = jnp.uint32

D = 1024
CHUNK = 128
SGU_HEADS = 8
HEAD_DIM = D // SGU_HEADS
POOL_WINDOWS = (2, 4, 8, 16)
POOL_GROUP_DIM = D // len(POOL_WINDOWS)
N_MOD = 6
N_GROUPS = 4
EPG = 8
N_EXPERTS = N_GROUPS * EPG
D_EXPERT = D // 2
LN_EPS = 1e-5

HALO = 16
COLB = 256
ROUTER_LANES = 128
EXPERT_LANE0 = N_GROUPS
PAIRS = EPG * (EPG - 1) // 2
N_CLASSES = N_GROUPS * PAIRS
ROW_W = D // 2 + ROUTER_LANES
MIXER_TS = 512
EXPERT_BM = 512
FINAL_TK = 512
FINAL_NBUF = 3
DEST_LANES = 512
PACK_BLOCK = 1024
SC_WIN = 128
SC_LANES = 16
SC_PACK_CHUNK = 32
SC_PACK_UNROLL = 8
VMEM_LIMIT = 56 * 1024 * 1024


def _ln(x):
    mu = jnp.mean(x, axis=-1, keepdims=True)
    xc = x - mu
    var = jnp.mean(xc * xc, axis=-1, keepdims=True)
    return xc * lax.rsqrt(var + LN_EPS)


_GELU_A = -2.0 * math.sqrt(2.0 / math.pi) * math.log2(math.e)
_GELU_B = _GELU_A * 0.044715


def _gelu_tanh(x):
    return x / (1.0 + jnp.exp2(x * (_GELU_A + _GELU_B * (x * x))))


def _dot(a, b):
    return jnp.dot(a, b, preferred_element_type=F32)


def _mod_kernel(c_ref, w_ref, b_ref, o_ref):
    a = jax.nn.silu(c_ref[...]).astype(BF16)
    o_ref[...] = _dot(a, w_ref[...].astype(BF16)) + b_ref[...]


def _modulation(c_all, w_mod, b_mod):
    nb = c_all.shape[0]
    n_out = w_mod.shape[1]
    cb = 512
    return pl.pallas_call(
        _mod_kernel,
        grid=(n_out // cb,),
        in_specs=[pl.BlockSpec((nb, D), lambda j: (0, 0)),
                  pl.BlockSpec((D, cb), lambda j: (0, j)),
                  pl.BlockSpec((1, cb), lambda j: (0, j))],
        out_specs=pl.BlockSpec((nb, cb), lambda j: (0, j)),
        out_shape=jax.ShapeDtypeStruct((nb, n_out), F32),
        name="mod",
    )(c_all, w_mod, b_mod.reshape(1, n_out))


def _pack_kernel(w_ref, o_ref):
    o_ref[...] = pltpu.bitcast(w_ref[...].astype(BF16), U32)


def _pack_rows(w):
    *lead, k, n = w.shape
    rows = math.prod(lead) * k
    rb = min(rows, PACK_BLOCK)
    nb = min(n, PACK_BLOCK)
    assert k % 2 == 0 and rows % rb == 0 and n % nb == 0
    packed = pl.pallas_call(
        _pack_kernel,
        grid=(rows // rb, n // nb),
        in_specs=[pl.BlockSpec((rb, nb), lambda i, j: (i, j))],
        out_specs=pl.BlockSpec((rb // 2, nb), lambda i, j: (i, j)),
        out_shape=jax.ShapeDtypeStruct((rows // 2, n), U32),
        name="pack",
    )(w.reshape(rows, n))
    return packed.reshape(*lead, k // 2, n)


def _mixer_kernel(nbp, ts, s_len, n_i, alpha,
                  xp_ref, xs_ref, xpp_ref, xsp_ref, xpn_ref, xsn_ref, mod_ref, modb_ref, icnt_ref,
                  win_ref, bin_ref, ws_ref, bsf_ref, sg_ref, sb_ref, wpool_ref, psc_ref,
                  wa_ref, wb_ref, wo_ref, l1g_ref, l1b_ref, wr_ref, br_ref, tri_ref,
                  x1_ref, h2u_ref, cr_ref, cnt_ref,
                  h_scr, u_scr, v_scr, vb_scr, a_scr, p_scr, m_scr, h2_scr, y_scr, xk_scr, ga_scr, gb_scr,
                  carry_scr):
    t = pl.program_id(0)
    n_tiles = pl.num_programs(0) - 1
    tf = jnp.minimum(t, n_tiles - 1)
    b = tf // n_i
    i = tf % n_i
    is_p = b < nbp
    ncb = D // COLB
    n_ext = ts + 2 * HALO
    half = D // 2
    main = slice(HALO, HALO + ts)

    def wmat(ref, cols=slice(None)):
        return pltpu.bitcast(ref[:, cols], BF16)

    @pl.when(t == 0)
    def _():
        carry_scr[...] = jnp.zeros_like(carry_scr)
        m_scr[...] = jnp.zeros_like(m_scr)
        xk_scr[...] = jnp.zeros_like(xk_scr)

    def row_rstd(read, mu):
        ss = jnp.zeros((ts, 1), F32)
        for j in range(ncb):
            xc = read(slice(j * COLB, (j + 1) * COLB)) - mu
            ss = ss + jnp.sum(xc * xc, axis=-1, keepdims=True)
        return lax.rsqrt(ss * (1.0 / D) + LN_EPS)

    modb = modb_ref[0]
    gate1_b, shift2, scale2 = modb[2:3], modb[3:4], modb[4:5]
    ysum = jnp.zeros((ts, 1), F32)
    for j in range(ncb):
        cols = slice(j * COLB, (j + 1) * COLB)
        tmix = _dot(m_scr[...], wmat(wo_ref, cols))
        y = alpha * xk_scr[:, cols] + gate1_b[:, cols] * tmix
        y_scr[:, cols] = y
        ysum = ysum + jnp.sum(y, axis=-1, keepdims=True)

    mod = mod_ref[0]
    shift1, scale1 = mod[0:1], mod[1:2]

    def adaln1(xv):
        return (_ln(xv) * (1.0 + scale1) + shift1).astype(BF16)

    h_scr[0:HALO, :] = adaln1(jnp.where(is_p, xpp_ref[0], xsp_ref[0]))
    h_scr[main, :] = adaln1(jnp.where(is_p, xp_ref[0], xs_ref[0]))
    h_scr[HALO + ts:, :] = adaln1(jnp.where(is_p, xpn_ref[0], xsn_ref[0]))

    def proj(hrows, c0):
        return _dot(h_scr[hrows, :], wmat(win_ref, slice(c0, c0 + COLB))) + bin_ref[:, c0:c0 + COLB]

    ymu = ysum * (1.0 / D)
    yrs = row_rstd(lambda cols: y_scr[:, cols], ymu)
    xsum = jnp.zeros((ts, 1), F32)
    for j in range(ncb):
        cols = slice(j * COLB, (j + 1) * COLB)
        x1 = (y_scr[:, cols] - ymu) * yrs * l1g_ref[:, cols] + l1b_ref[:, cols]
        x1_ref[0, :, cols] = x1
        xsum = xsum + jnp.sum(x1, axis=-1, keepdims=True)
    xmu = xsum * (1.0 / D)
    xrs = row_rstd(lambda cols: x1_ref[0, :, cols], xmu)
    for j in range(ncb):
        cols = slice(j * COLB, (j + 1) * COLB)
        h2 = (x1_ref[0, :, cols] - xmu) * xrs * (1.0 + scale2[:, cols]) + shift2[:, cols]
        h2_scr[:, cols] = h2.astype(BF16)
    lo_bits = lax.bitcast_convert_type(h2_scr[:, 0:half].astype(F32), U32)
    hi_bits = lax.bitcast_convert_type(h2_scr[:, half:D].astype(F32), U32)
    h2u_ref[:, 0:half] = (lo_bits >> 16) | (hi_bits & jnp.uint32(0xFFFF0000))

    vsum = jnp.zeros((ts, 1), F32)
    for j in range(ncb):
        cols = slice(j * COLB, (j + 1) * COLB)
        u_scr[:, cols] = _gelu_tanh(proj(main, j * COLB))
        gv = _gelu_tanh(proj(main, D + j * COLB))
        v_scr[:, cols] = gv
        vsum = vsum + jnp.sum(gv, axis=-1, keepdims=True)

    logits = _dot(h2_scr[...], wmat(wr_ref)) + br_ref[...]
    lane = lax.broadcasted_iota(jnp.int32, (ts, ROUTER_LANES), 1)
    lane_f = lane.astype(F32)
    neg = -jnp.inf
    is_g = lane < N_GROUPS
    lg = jnp.where(is_g, logits, neg)
    mg = jnp.max(lg, axis=-1, keepdims=True)
    g_sel = jnp.min(jnp.where(lg == mg, lane_f, float(ROUTER_LANES)), axis=-1, keepdims=True)
    p_sel = 1.0 / jnp.sum(jnp.where(is_g, jnp.exp(logits - mg), 0.0), axis=-1, keepdims=True)
    e_lo = EXPERT_LANE0 + g_sel * EPG
    in_grp = (lane_f >= e_lo) & (lane_f < e_lo + EPG)
    le = jnp.where(in_grp, logits, neg)
    m1 = jnp.max(le, axis=-1, keepdims=True)
    i1 = jnp.min(jnp.where(le == m1, lane_f, float(ROUTER_LANES)), axis=-1, keepdims=True)
    le2 = jnp.where(lane_f == i1, neg, le)
    m2 = jnp.max(le2, axis=-1, keepdims=True)
    i2 = jnp.min(jnp.where(le2 == m2, lane_f, float(ROUTER_LANES)), axis=-1, keepdims=True)
    e2x = jnp.exp(m2 - m1)
    den = 1.0 + e2x
    g1 = p_sel / den
    g2 = p_sel * e2x / den

    j1 = i1 - e_lo
    j2 = i2 - e_lo
    first_is_a = j1 < j2
    ja = jnp.minimum(j1, j2)
    jb = jnp.maximum(j1, j2)
    cls = g_sel * PAIRS + (EPG - 1) * ja - ja * (ja - 1.0) * 0.5 + (jb - ja - 1.0)
    w_a = jnp.where(first_is_a, g1, g2)
    w_b = jnp.where(first_is_a, g2, g1)
    extra = jnp.where(lane == 0, w_a, jnp.where(lane == 1, w_b, 0.0))
    h2u_ref[:, half:half + ROUTER_LANES] = lax.bitcast_convert_type(extra, U32)

    hit = lane_f == cls
    onehot = jnp.where(hit, (t > 0).astype(F32), 0.0)

    def finish_ranks():
        pre = _dot(tri_ref[...], onehot.astype(BF16)) + carry_scr[0:1, :]
        rank = jnp.sum(jnp.where(hit, pre, 0.0), axis=-1, keepdims=True)
        carry_scr[0:1, :] = carry_scr[0:1, :] + jnp.sum(onehot, axis=0, keepdims=True)
        cnt_ref[...] = carry_scr[...]
        cr = jnp.where(lane == 0, cls, jnp.where(lane == 1, rank, 0.0))
        cr_ref[...] = cr.T[0:8, :]

    vmu = vsum * (1.0 / D)
    vrs = row_rstd(lambda cols: v_scr[:, cols], vmu)
    for j in range(ncb):
        cols = slice(j * COLB, (j + 1) * COLB)
        vb_scr[:, cols] = ((v_scr[:, cols] - vmu) * vrs * sg_ref[:, cols] + sb_ref[:, cols]).astype(BF16)

    def pool_project(gi):
        ext_pos = lax.broadcasted_iota(jnp.int32, (n_ext, 1), 0) + (i * ts - HALO)
        ext_valid = (ext_pos >= 0) & (ext_pos < s_len)
        return jnp.where(ext_valid, proj(slice(None), 2 * D + gi * POOL_GROUP_DIM), 0.0)

    def pool_group(gi, zp):
        w = POOL_WINDOWS[gi]
        cols = slice(gi * POOL_GROUP_DIM, (gi + 1) * POOL_GROUP_DIM)
        acc = zp + pltpu.roll(zp, 1, 0)
        if w >= 4:
            acc = pltpu.roll(acc, 1, 0) + pltpu.roll(acc, n_ext - 1, 0)
        if w >= 8:
            acc = pltpu.roll(acc, 2, 0) + pltpu.roll(acc, n_ext - 2, 0)
        if w >= 16:
            acc = pltpu.roll(acc, 4, 0) + pltpu.roll(acc, n_ext - 4, 0)
        inv_count = icnt_ref[:, gi:gi + 1]
        dd = (acc[main] * inv_count - zp[main]).astype(BF16)
        p_scr[:, cols] = (_dot(dd, pltpu.bitcast(wpool_ref[gi], BF16)) * psc_ref[:, cols]).astype(BF16)

    def sgu_chunk(c):
        crow = slice(c * CHUNK, (c + 1) * CHUNK)
        for hh in range(SGU_HEADS):
            cols = slice(hh * HEAD_DIM, (hh + 1) * HEAD_DIM)
            mixed = _dot(ws_ref[hh], vb_scr[crow, cols]) + bsf_ref[:, cols]
            a_scr[crow, cols] = (u_scr[crow, cols] * mixed).astype(BF16)

    assert len(POOL_WINDOWS) == ncb
    for j in range(ncb):
        cols = slice(j * COLB, (j + 1) * COLB)
        zp = pool_project(j)
        ga_scr[:, cols] = jax.nn.sigmoid(proj(main, 3 * D + j * COLB))
        gb_scr[:, cols] = jax.nn.sigmoid(proj(main, 4 * D + j * COLB))
        pool_group(j, zp)
    for j in range(ncb):
        cols = slice(j * COLB, (j + 1) * COLB)
        gb_scr[:, cols] = gb_scr[:, cols] * _dot(p_scr[...], wmat(wb_ref, cols))
    for c in range(ts // CHUNK):
        sgu_chunk(c)

    finish_ranks()

    for j in range(ncb):
        cols = slice(j * COLB, (j + 1) * COLB)
        ta = _dot(a_scr[...], wmat(wa_ref, cols))
        m_scr[:, cols] = (ga_scr[:, cols] * ta + gb_scr[:, cols]).astype(BF16)
    xk_scr[...] = jnp.where(is_p, xp_ref[0], xs_ref[0])


def _const_spec(shape):
    nd = len(shape)
    return pl.BlockSpec(shape, lambda t: (0,) * nd, pipeline_mode=pl.Buffered(1))


def _mixer(x_p, x_s, mod, wts, alpha, ts):
    nbp, s_len, _ = x_p.shape
    nbs = x_s.shape[0]
    assert x_s.shape[1] == s_len and s_len % ts == 0 and ts % CHUNK == 0
    nb = nbp + nbs
    n_i = s_len // ts
    hb = ts // HALO
    n_hb = s_len // HALO

    n_tiles = nb * n_i

    def front(t):
        tf = jnp.minimum(t, n_tiles - 1)
        return tf // n_i, tf % n_i

    def back(t):
        return jnp.maximum(t - 1, 0)

    def pick(b, on_p, val, const):
        return jnp.where(b < nbp if on_p else b >= nbp, val, const)

    def main_map(on_p):
        def f(t):
            b, i = front(t)
            bb = pick(b, on_p, b if on_p else b - nbp, nbp - 1 if on_p else 0)
            ii = pick(b, on_p, i, n_i - 1 if on_p else 0)
            return (bb, ii, 0)
        return f

    def halo_map(on_p, nxt):
        def f(t):
            b, i = front(t)
            bb = pick(b, on_p, b if on_p else b - nbp, nbp - 1 if on_p else 0)
            idx = jnp.minimum((i + 1) * hb, n_hb - 1) if nxt else jnp.maximum(i * hb - 1, 0)
            ii = pick(b, on_p, idx, n_hb - 1 if on_p else 0)
            return (bb, ii, 0)
        return f

    tri = (lax.broadcasted_iota(jnp.int32, (ts, ts), 1)
           < lax.broadcasted_iota(jnp.int32, (ts, ts), 0)).astype(BF16)
    consts = list(wts) + [tri]
    in_specs = [
        pl.BlockSpec((1, ts, D), main_map(True)),
        pl.BlockSpec((1, ts, D), main_map(False)),
        pl.BlockSpec((1, HALO, D), halo_map(True, False)),
        pl.BlockSpec((1, HALO, D), halo_map(False, False)),
        pl.BlockSpec((1, HALO, D), halo_map(True, True)),
        pl.BlockSpec((1, HALO, D), halo_map(False, True)),
        pl.BlockSpec((1, N_MOD, D), lambda t: (front(t)[0], 0, 0)),
        pl.BlockSpec((1, N_MOD, D), lambda t: (back(t) // n_i, 0, 0)),
        pl.BlockSpec((ts, ROUTER_LANES), lambda t: (front(t)[1], 0)),
    ] + [_const_spec(w.shape) for w in consts]

    pos = jnp.arange(s_len, dtype=jnp.int32)[:, None]
    win = jnp.asarray(POOL_WINDOWS + (1,) * (ROUTER_LANES - len(POOL_WINDOWS)), jnp.int32)[None, :]
    inv_count = 1.0 / (jnp.minimum(pos + (win - 1 - win // 2), s_len - 1)
                       - jnp.maximum(pos - win // 2, 0) + 1).astype(F32)
    n_tok = nb * s_len
    out_shape = (
        jax.ShapeDtypeStruct((nb, s_len, D), F32),
        jax.ShapeDtypeStruct((n_tok, ROW_W), U32),
        jax.ShapeDtypeStruct((8, n_tok), F32),
        jax.ShapeDtypeStruct((8, ROUTER_LANES), F32),
    )
    out_specs = (
        pl.BlockSpec((1, ts, D), lambda t: (back(t) // n_i, back(t) % n_i, 0)),
        pl.BlockSpec((ts, ROW_W), lambda t: (back(t), 0)),
        pl.BlockSpec((8, ts), lambda t: (0, back(t))),
        pl.BlockSpec((8, ROUTER_LANES), lambda t: (0, 0)),
    )
    scratch = [
        pltpu.VMEM((ts + 2 * HALO, D), BF16),
        pltpu.VMEM((ts, D), F32),
        pltpu.VMEM((ts, D), F32),
        pltpu.VMEM((ts, D), BF16),
        pltpu.VMEM((ts, D), BF16),
        pltpu.VMEM((ts, D), BF16),
        pltpu.VMEM((ts, D), BF16),
        pltpu.VMEM((ts, D), BF16),
        pltpu.VMEM((ts, D), F32),
        pltpu.VMEM((ts, D), F32),
        pltpu.VMEM((ts, D), F32),
        pltpu.VMEM((ts, D), F32),
        pltpu.VMEM((8, ROUTER_LANES), F32),
    ]
    return pl.pallas_call(
        functools.partial(_mixer_kernel, nbp, ts, s_len, n_i, alpha),
        grid=(n_tiles + 1,),
        in_specs=in_specs,
        out_specs=out_specs,
        out_shape=out_shape,
        scratch_shapes=scratch,
        compiler_params=pltpu.CompilerParams(
            dimension_semantics=("arbitrary",), vmem_limit_bytes=VMEM_LIMIT),
        name="mixer",
    )(x_p, x_s, x_p, x_s, x_p, x_s, mod, mod, inv_count, *consts)


def _dest_kernel(seg_ref, cls_ref, rank_ref, o_ref):
    cls = cls_ref[...]
    acc = rank_ref[...].astype(jnp.int32)
    for c in range(N_CLASSES):
        acc = acc + jnp.where(cls == float(c), seg_ref[c], 0)
    o_ref[...] = acc


def _dest(seg_start, cls, rank):
    shape = cls.shape
    return pl.pallas_call(
        _dest_kernel,
        grid_spec=pltpu.PrefetchScalarGridSpec(
            num_scalar_prefetch=1,
            grid=(1,),
            in_specs=[pl.BlockSpec(shape, lambda i, s: (0, 0)), pl.BlockSpec(shape, lambda i, s: (0, 0))],
            out_specs=pl.BlockSpec(shape, lambda i, s: (0, 0)),
        ),
        out_shape=jax.ShapeDtypeStruct(shape, jnp.int32),
        name="dest",
    )(seg_start, cls, rank)


def _sc_mesh():
    return plsc.VectorSubcoreMesh(core_axis_name="core", subcore_axis_name="subcore")


def _sc_scatter_rows(rows, dest, n_rows, win):
    n_tok, row_w = rows.shape
    mesh = _sc_mesh()
    n_workers = mesh.num_cores * mesh.num_subcores
    n_steps = n_tok // win
    assert n_tok % win == 0 and n_steps % n_workers == 0
    per_worker = n_steps // n_workers

    @pl.kernel(out_type=jax.ShapeDtypeStruct((n_rows, row_w), rows.dtype), mesh=mesh,
               scratch_types=[pltpu.VMEM((win,), jnp.int32), pltpu.VMEM((win, row_w), rows.dtype)])
    def scatter(rows_hbm, idx_hbm, out_hbm, idx_vmem, buf):
        worker = lax.axis_index("core") * mesh.num_subcores + lax.axis_index("subcore")

        @pl.loop(0, per_worker)
        def _(k):
            step = worker * per_worker + k
            pltpu.sync_copy(idx_hbm.at[step], idx_vmem)
            pltpu.sync_copy(rows_hbm.at[pl.ds(step * win, win)], buf)
            pltpu.sync_copy(buf, out_hbm.at[idx_vmem])

    return scatter(rows, dest.reshape(n_steps, win))


def _sc_gather_rows(table, idx, win):
    n_tok = idx.shape[0]
    row_w = table.shape[1]
    mesh = _sc_mesh()
    n_workers = mesh.num_cores * mesh.num_subcores
    n_steps = n_tok // win
    assert n_tok % win == 0 and n_steps % n_workers == 0
    per_worker = n_steps // n_workers

    @pl.kernel(out_type=jax.ShapeDtypeStruct((n_tok, row_w), table.dtype), mesh=mesh,
               scratch_types=[pltpu.VMEM((win,), jnp.int32), pltpu.VMEM((win, row_w), table.dtype)])
    def gather(table_hbm, idx_hbm, out_hbm, idx_vmem, buf):
        worker = lax.axis_index("core") * mesh.num_subcores + lax.axis_index("subcore")

        @pl.loop(0, per_worker)
        def _(k):
            step = worker * per_worker + k
            pltpu.sync_copy(idx_hbm.at[step], idx_vmem)
            pltpu.sync_copy(table_hbm.at[idx_vmem], buf)
            pltpu.sync_copy(buf, out_hbm.at[pl.ds(step * win, win)])

    return gather(table, idx.reshape(n_steps, win))


def _sc_pack_rows(w, chunk):
    *lead, k, n = w.shape
    rows = math.prod(lead) * k
    mesh = _sc_mesh()
    lanes = SC_LANES
    n_workers = mesh.num_cores * mesh.num_subcores
    assert k % 2 == 0 and rows % (n_workers * chunk) == 0 and chunk % 2 == 0 and n % (lanes * SC_PACK_UNROLL) == 0
    per_worker = rows // (n_workers * chunk)

    def to_bf16_bits(x):
        bits = lax.bitcast_convert_type(x, U32)
        return (bits + jnp.uint32(0x7FFF) + ((bits >> 16) & jnp.uint32(1))) >> 16

    @pl.kernel(out_type=jax.ShapeDtypeStruct((rows // 2, n), U32), mesh=mesh,
               scratch_types=[pltpu.VMEM((chunk, n), F32), pltpu.VMEM((chunk // 2, n), U32)],
               compiler_params=pltpu.CompilerParams(needs_layout_passes=False))
    def pack(w_hbm, out_hbm, src, dst):
        worker = lax.axis_index("core") * mesh.num_subcores + lax.axis_index("subcore")

        @pl.loop(0, per_worker)
        def _(step):
            blk = worker * per_worker + step
            row0 = pl.multiple_of(blk * chunk, chunk)
            out_row0 = pl.multiple_of(blk * (chunk // 2), chunk // 2)
            pltpu.sync_copy(w_hbm.at[pl.ds(row0, chunk)], src)

            @pl.loop(0, chunk // 2)
            def _(q):
                @pl.loop(0, n, step=lanes * SC_PACK_UNROLL)
                def _(c0):
                    for u in range(SC_PACK_UNROLL):
                        c = c0 + u * lanes
                        lo = to_bf16_bits(src[2 * q, pl.ds(c, lanes)])
                        hi = to_bf16_bits(src[2 * q + 1, pl.ds(c, lanes)])
                        dst[q, pl.ds(c, lanes)] = lo | (hi << 16)

            pltpu.sync_copy(dst, out_hbm.at[pl.ds(out_row0, chunk // 2)])

    return pack(w.reshape(rows, n)).reshape(*lead, k // 2, n)


def _expert_kernel(bm, ea_ref, g0_ref, u0_ref, d0_ref, g1_ref, u1_ref, d1_ref, slot_ref, nused_ref,
                   xs_ref, wga_ref, wua_ref, wda_ref,
                   wgb0_ref, wub0_ref, wdb0_ref, wgb1_ref, wub1_ref, wdb1_ref, ys_ref, act_scr, y_scr):
    j = pl.program_id(0)
    n_used = nused_ref[0]
    half = D // 2
    b_slots = ((wgb0_ref, wub0_ref, wdb0_ref), (wgb1_ref, wub1_ref, wdb1_ref))

    def wmat(ref, k0, k1, cols):
        return pltpu.bitcast(ref[0, k0 // 2:k1 // 2, cols], BF16)

    def store_previous():
        for c0 in range(0, half, COLB):
            lo_bits = lax.bitcast_convert_type(y_scr[:, c0:c0 + COLB].astype(BF16).astype(F32), U32)
            hi_bits = lax.bitcast_convert_type(y_scr[:, half + c0:half + c0 + COLB].astype(BF16).astype(F32), U32)
            ys_ref[:, c0:c0 + COLB] = (lo_bits >> 16) | (hi_bits & jnp.uint32(0xFFFF0000))

    def compute(slot):
        wgb_ref, wub_ref, wdb_ref = b_slots[slot]
        bits = xs_ref[:, 0:half]
        lo = lax.bitcast_convert_type(bits << 16, F32).astype(BF16)
        hi = lax.bitcast_convert_type(bits & jnp.uint32(0xFFFF0000), F32).astype(BF16)
        wts = lax.bitcast_convert_type(xs_ref[:, half:half + ROUTER_LANES], F32)
        for e, (wg_ref, wu_ref) in enumerate(((wga_ref, wua_ref), (wgb_ref, wub_ref))):
            for c in range(D_EXPERT // COLB):
                cols = slice(c * COLB, (c + 1) * COLB)
                g = _dot(lo, wmat(wg_ref, 0, half, cols)) + _dot(hi, wmat(wg_ref, half, D, cols))
                up = _dot(lo, wmat(wu_ref, 0, half, cols)) + _dot(hi, wmat(wu_ref, half, D, cols))
                act_scr[e, :, cols] = (jax.nn.silu(g) * up).astype(BF16)
        w_a = wts[:, 0:1]
        w_b = wts[:, 1:2]

        for c0 in range(0, D, COLB):
            cols = slice(c0, c0 + COLB)
            y_scr[:, cols] = (w_a * _dot(act_scr[0], wmat(wda_ref, 0, D_EXPERT, cols))
                              + w_b * _dot(act_scr[1], wmat(wdb_ref, 0, D_EXPERT, cols)))

    @pl.when(j == 0)
    def _():
        compute(0)

    for slot in (0, 1):
        @pl.when((j > 0) & (j < n_used) & (slot_ref[j] == slot))
        def _(slot=slot):
            store_previous()
            compute(slot)

    @pl.when(j == n_used)
    def _():
        store_previous()


def _experts(xs, blk_a, blk_b_slots, blk_slot, n_used, w_gate, w_up, w_down, bm):
    n_rows = xs.shape[0]
    n_blocks = n_rows // bm
    n_prefetch = 3 + len(blk_b_slots)

    def x_map(j, *pre):
        return (jnp.minimum(j, pre[-1][0] - 1), 0)

    def y_map(j, *pre):
        return (jnp.clip(j - 1, 0, pre[-1][0] - 1), 0)

    def w_map(k):
        return lambda j, *pre: (pre[k][j], 0, 0)

    gate_up = (1, D // 2, D_EXPERT)
    down = (1, D_EXPERT // 2, D)
    return pl.pallas_call(
        functools.partial(_expert_kernel, bm),
        grid_spec=pltpu.PrefetchScalarGridSpec(
            num_scalar_prefetch=n_prefetch,
            grid=(n_used[0] + 1,),
            in_specs=[pl.BlockSpec((bm, ROW_W), x_map),
                      pl.BlockSpec(gate_up, w_map(0)), pl.BlockSpec(gate_up, w_map(0)), pl.BlockSpec(down, w_map(0)),
                      pl.BlockSpec(gate_up, w_map(1)), pl.BlockSpec(gate_up, w_map(2)), pl.BlockSpec(down, w_map(3)),
                      pl.BlockSpec(gate_up, w_map(4)), pl.BlockSpec(gate_up, w_map(5)), pl.BlockSpec(down, w_map(6))],
            out_specs=pl.BlockSpec((bm, D // 2), y_map),
            scratch_shapes=[pltpu.VMEM((2, bm, D_EXPERT), BF16), pltpu.VMEM((bm, D), F32)],
        ),
        out_shape=jax.ShapeDtypeStruct((n_rows, D // 2), U32),
        compiler_params=pltpu.CompilerParams(
            dimension_semantics=("arbitrary",), vmem_limit_bytes=VMEM_LIMIT),
        name="experts",
    )(blk_a, *blk_b_slots, blk_slot, n_used, xs, w_gate, w_up, w_down, w_gate, w_up, w_down,
      w_gate, w_up, w_down)


def _final_kernel(alpha, b_off, nbg, s_len, ch, x1_hbm, f_hbm, mod_ref, l2g_ref, l2b_ref, out_hbm,
                  xbuf, fbuf, obuf, in_sem, out_sem):
    half = D // 2
    per_b = s_len // ch
    n_chunks = nbg * per_b
    sub = 128

    def in_copies(c, slot):
        b = c // per_b
        i = c % per_b
        return (pltpu.make_async_copy(x1_hbm.at[b + b_off, pl.ds(i * ch, ch)], xbuf.at[slot], in_sem.at[0, slot]),
                pltpu.make_async_copy(f_hbm.at[pl.ds(c * ch, ch)], fbuf.at[slot], in_sem.at[1, slot]))

    def out_copy(c, slot):
        b = c // per_b
        i = c % per_b
        return pltpu.make_async_copy(obuf.at[slot], out_hbm.at[b, pl.ds(i * ch, ch)], out_sem.at[slot])

    for k in range(FINAL_NBUF - 1):
        for cp in in_copies(k, k):
            cp.start()

    def step(c, carry):
        slot = c % FINAL_NBUF
        oslot = c % 2
        ahead = c + FINAL_NBUF - 1

        @pl.when(ahead < n_chunks)
        def _():
            for cp in in_copies(ahead, ahead % FINAL_NBUF):
                cp.start()

        for cp in in_copies(c, slot):
            cp.wait()

        @pl.when(c >= 2)
        def _():
            out_copy(c - 2, oslot).wait()

        gate2 = mod_ref[c // per_b + b_off][5:6]
        for r0 in range(0, ch, sub):
            rows = pl.ds(r0, sub)
            bits = fbuf[slot, rows, :]
            f_lo = lax.bitcast_convert_type(bits << 16, F32)
            f_hi = lax.bitcast_convert_type(bits & jnp.uint32(0xFFFF0000), F32)
            y_lo = alpha * xbuf[slot, rows, 0:half] + gate2[:, 0:half] * f_lo
            y_hi = alpha * xbuf[slot, rows, half:D] + gate2[:, half:D] * f_hi
            mu = (jnp.sum(y_lo, axis=-1, keepdims=True) + jnp.sum(y_hi, axis=-1, keepdims=True)) * (1.0 / D)
            c_lo = y_lo - mu
            c_hi = y_hi - mu
            var = (jnp.sum(c_lo * c_lo, axis=-1, keepdims=True)
                   + jnp.sum(c_hi * c_hi, axis=-1, keepdims=True)) * (1.0 / D)
            rs = lax.rsqrt(var + LN_EPS)
            obuf[oslot, rows, 0:half] = c_lo * rs * l2g_ref[:, 0:half] + l2b_ref[:, 0:half]
            obuf[oslot, rows, half:D] = c_hi * rs * l2g_ref[:, half:D] + l2b_ref[:, half:D]

        out_copy(c, oslot).start()
        return carry

    lax.fori_loop(0, n_chunks, step, 0)
    out_copy(n_chunks - 2, n_chunks % 2).wait()
    out_copy(n_chunks - 1, (n_chunks - 1) % 2).wait()


def _final(x1, f, mod, ln2_g, ln2_b, b_off, nbg, alpha, ch):
    s_len = x1.shape[1]
    assert s_len % ch == 0 and nbg * (s_len // ch) >= FINAL_NBUF
    vmem = pl.BlockSpec(memory_space=pltpu.VMEM)
    hbm = pl.BlockSpec(memory_space=pl.ANY)
    return pl.pallas_call(
        functools.partial(_final_kernel, alpha, b_off, nbg, s_len, ch),
        in_specs=[hbm, hbm, vmem, vmem, vmem],
        out_specs=hbm,
        out_shape=jax.ShapeDtypeStruct((nbg, s_len, D), F32),
        scratch_shapes=[pltpu.VMEM((FINAL_NBUF, ch, D), F32), pltpu.VMEM((FINAL_NBUF, ch, D // 2), U32),
                        pltpu.VMEM((2, ch, D), F32), pltpu.SemaphoreType.DMA((2, FINAL_NBUF)),
                        pltpu.SemaphoreType.DMA((2,))],
        compiler_params=pltpu.CompilerParams(vmem_limit_bytes=VMEM_LIMIT),
        name="final",
    )(x1, f, mod, ln2_g, ln2_b)


def _class_experts():
    ea, eb = [], []
    for g in range(N_GROUPS):
        for a in range(EPG):
            for b in range(a + 1, EPG):
                ea.append(g * EPG + a)
                eb.append(g * EPG + b)
    return np.asarray(ea, np.int32), np.asarray(eb, np.int32)


def _layer(x_p, x_s, c_all, p, alpha):
    nbp, s_len, _ = x_p.shape
    nbs = x_s.shape[0]
    nb = nbp + nbs
    n_tok = nb * s_len

    mod = _modulation(c_all, p["w_mod"], p["b_mod"]).reshape(nb, N_MOD, D)

    bsf = jnp.repeat(p["b_spatial"].T, HEAD_DIM, axis=1)
    n_in = p["w_in"].shape[1]
    w_router = jnp.concatenate(
        [p["w_router_group"], p["w_router_expert"],
         jnp.zeros((D, ROUTER_LANES - N_GROUPS - N_EXPERTS), F32)], axis=1)
    b_router = jnp.concatenate(
        [p["b_router_group"], p["b_router_expert"],
         jnp.zeros((ROUTER_LANES - N_GROUPS - N_EXPERTS,), F32)]).reshape(1, ROUTER_LANES)
    wts = [
        _pack_rows(p["w_in"]), p["b_in"].reshape(1, n_in),
        p["w_spatial"].astype(BF16), bsf,
        p["sgu_g"].reshape(1, D), p["sgu_b"].reshape(1, D),
        _pack_rows(p["w_pool"]), p["pool_scale"].reshape(1, D),
        _pack_rows(p["w_branch_a"]), _pack_rows(p["w_branch_b"]), _pack_rows(p["w_out"]),
        p["ln1_g"].reshape(1, D), p["ln1_b"].reshape(1, D),
        _pack_rows(w_router), b_router,
    ]
    expert_w = [_sc_pack_rows(p[name], SC_PACK_CHUNK) for name in ("w_exp_gate", "w_exp_up", "w_exp_down")]
    x1, rows, cr, cnt = _mixer(x_p, x_s, mod, wts, alpha, MIXER_TS)

    bm = EXPERT_BM
    assert n_tok % bm == 0 and n_tok % DEST_LANES == 0
    counts = cnt[0, :N_CLASSES].astype(jnp.int32)
    padded = (counts + bm - 1) // bm * bm
    seg_end = jnp.cumsum(padded).astype(jnp.int32)
    seg_start = seg_end - padded
    n_blocks = n_tok // bm + N_CLASSES
    block_start = jnp.arange(n_blocks, dtype=jnp.int32) * bm
    blk_cls = jnp.minimum(
        jnp.sum((seg_end[None, :] <= block_start[:, None]).astype(jnp.int32), axis=1), N_CLASSES - 1)
    cls_a, cls_b = _class_experts()
    blk_a = jnp.asarray(cls_a)[blk_cls]
    idx = jnp.arange(n_blocks, dtype=jnp.int32)
    changed = jnp.concatenate([jnp.zeros((1,), jnp.int32), (blk_cls[1:] != blk_cls[:-1]).astype(jnp.int32)])
    blk_slot = jnp.cumsum(changed).astype(jnp.int32) % 2
    later = jnp.where(blk_cls[None, :] > blk_cls[:, None], blk_cls[None, :], N_CLASSES)
    next_cls = jnp.minimum(jnp.min(later, axis=1), N_CLASSES - 1)
    earlier = jnp.max(jnp.where(blk_cls[None, :] < blk_cls[:, None], blk_cls[None, :], -1), axis=1)
    prev_cls = jnp.where(earlier >= 0, earlier, blk_cls)
    same = blk_cls[None, :] == blk_cls[:, None]
    pos_in_cls = idx - jnp.min(jnp.where(same, idx[None, :], n_blocks), axis=1)
    last_pos = jnp.sum(same.astype(jnp.int32), axis=1) - 1
    b_now = jnp.asarray(cls_b)[blk_cls]
    b_next = jnp.asarray(cls_b)[next_cls]
    b_prev = jnp.asarray(cls_b)[prev_cls]
    blk_b_slots = [
        jnp.where(blk_slot == s, b_now, jnp.where(pos_in_cls >= jnp.minimum(k, last_pos), b_next, b_prev))
        for s in (0, 1) for k in range(3)]
    n_used = (seg_end[-1:] // bm).astype(jnp.int32)

    lane_shape = (n_tok // DEST_LANES, DEST_LANES)
    dest = _dest(seg_start, cr[0].reshape(lane_shape), cr[1].reshape(lane_shape)).reshape(n_tok)

    xs = _sc_scatter_rows(rows, dest, n_blocks * bm, SC_WIN)
    ys = _experts(xs, blk_a, blk_b_slots, blk_slot, n_used, *expert_w, bm)

    l2g = p["ln2_g"].reshape(1, D)
    l2b = p["ln2_b"].reshape(1, D)
    tp = nbp * s_len
    f_p = _sc_gather_rows(ys, dest[:tp], SC_WIN)
    f_s = _sc_gather_rows(ys, dest[tp:], SC_WIN)
    y_p = _final(x1, f_p, mod, l2g, l2b, 0, nbp, alpha, FINAL_TK)
    y_s = _final(x1, f_s, mod, l2g, l2b, nbp, nbs, alpha, FINAL_TK)
    return y_p, y_s


_PARAM_NAMES = ("w_mod", "b_mod", "w_in", "b_in", "w_spatial", "b_spatial", "sgu_g", "sgu_b", "w_pool",
                "pool_scale", "w_branch_a", "w_branch_b", "w_out", "ln1_g", "ln1_b", "w_router_group",
                "b_router_group", "w_router_expert", "b_router_expert", "w_exp_gate", "w_exp_up",
                "w_exp_down", "ln2_g", "ln2_b")


def kernel(x_prompt, x_sample, c_prompt, c_sample, w_mod, b_mod, w_in, b_in, w_spatial, b_spatial, sgu_g, sgu_b, w_pool, pool_scale, w_branch_a, w_branch_b, w_out, ln1_g, ln1_b, w_router_group, b_router_group, w_router_expert, b_router_expert, w_exp_gate, w_exp_up, w_exp_down, ln2_g, ln2_b):
    params = (w_mod, b_mod, w_in, b_in, w_spatial, b_spatial, sgu_g, sgu_b, w_pool, pool_scale,
              w_branch_a, w_branch_b, w_out, ln1_g, ln1_b, w_router_group, b_router_group,
              w_router_expert, b_router_expert, w_exp_gate, w_exp_up, w_exp_down, ln2_g, ln2_b)
    depth = w_mod.shape[0]
    alpha = (2.0 * depth) ** 0.25
    c_all = jnp.concatenate([c_prompt, c_sample], axis=0)
    y_p, y_s = x_prompt, x_sample
    for l in range(depth):
        p = {name: w[l] for name, w in zip(_PARAM_NAMES, params)}
        y_p, y_s = _layer(y_p, y_s, c_all, p, alpha)
    return (y_p, y_s)
```

```python
import functools
import math

import jax
import jax.numpy as jnp
import numpy as np
from jax import lax
from jax.experimental import pallas as pl
from jax.experimental.pallas import tpu as pltpu
from jax.experimental.pallas import tpu_sc as plsc

F32 = jnp.float32
BF16 = jnp.bfloat16
U32 = jnp.uint32

D = 1024
CHUNK = 128
SGU_HEADS = 8
HEAD_DIM = D // SGU_HEADS
POOL_WINDOWS = (2, 4, 8, 16)
POOL_GROUP_DIM = D // len(POOL_WINDOWS)
N_MOD = 6
N_GROUPS = 4
EPG = 8
N_EXPERTS = N_GROUPS * EPG
D_EXPERT = D // 2
LN_EPS = 1e-5

HALO = 16
COLB = 256
ROUTER_LANES = 128
EXPERT_LANE0 = N_GROUPS
PAIRS = EPG * (EPG - 1) // 2
N_CLASSES = N_GROUPS * PAIRS
ROW_W = D // 2 + ROUTER_LANES
MIXER_TS = 512
EXPERT_BM = 512
FINAL_TK = 256
FINAL_NBUF = 4
DEST_LANES = 512
PACK_BLOCK = 1024
SC_WIN = 128
SC_LANES = 16
SC_PACK_CHUNK = 32
SC_PACK_UNROLL = 8
VMEM_LIMIT = 56 * 1024 * 1024


def _ln(x):
    mu = jnp.mean(x, axis=-1, keepdims=True)
    xc = x - mu
    var = jnp.mean(xc * xc, axis=-1, keepdims=True)
    return xc * lax.rsqrt(var + LN_EPS)


_GELU_A = -2.0 * math.sqrt(2.0 / math.pi) * math.log2(math.e)
_GELU_B = _GELU_A * 0.044715


def _gelu_tanh(x):
    return x / (1.0 + jnp.exp2(x * (_GELU_A + _GELU_B * (x * x))))


def _dot(a, b):
    return jnp.dot(a, b, preferred_element_type=F32)


def _mod_kernel(c_ref, w_ref, b_ref, o_ref):
    a = jax.nn.silu(c_ref[...]).astype(BF16)
    o_ref[...] = _dot(a, w_ref[...].astype(BF16)) + b_ref[...]


def _modulation(c_all, w_mod, b_mod):
    nb = c_all.shape[0]
    n_out = w_mod.shape[1]
    cb = 512
    return pl.pallas_call(
        _mod_kernel,
        grid=(n_out // cb,),
        in_specs=[pl.BlockSpec((nb, D), lambda j: (0, 0)),
                  pl.BlockSpec((D, cb), lambda j: (0, j)),
                  pl.BlockSpec((1, cb), lambda j: (0, j))],
        out_specs=pl.BlockSpec((nb, cb), lambda j: (0, j)),
        out_shape=jax.ShapeDtypeStruct((nb, n_out), F32),
        name="mod",
    )(c_all, w_mod, b_mod.reshape(1, n_out))


def _pack_kernel(w_ref, o_ref):
    o_ref[...] = pltpu.bitcast(w_ref[...].astype(BF16), U32)


def _pack_rows(w):
    *lead, k, n = w.shape
    rows = math.prod(lead) * k
    rb = min(rows, PACK_BLOCK)
    nb = min(n, PACK_BLOCK)
    assert k % 2 == 0 and rows % rb == 0 and n % nb == 0
    packed = pl.pallas_call(
        _pack_kernel,
        grid=(rows // rb, n // nb),
        in_specs=[pl.BlockSpec((rb, nb), lambda i, j: (i, j))],
        out_specs=pl.BlockSpec((rb // 2, nb), lambda i, j: (i, j)),
        out_shape=jax.ShapeDtypeStruct((rows // 2, n), U32),
        name="pack",
    )(w.reshape(rows, n))
    return packed.reshape(*lead, k // 2, n)


def _mixer_kernel(nbp, ts, s_len, n_i, alpha,
                  xp_ref, xs_ref, xpp_ref, xsp_ref, xpn_ref, xsn_ref, mod_ref, modb_ref, icnt_ref,
                  win_ref, bin_ref, ws_ref, bsf_ref, sg_ref, sb_ref, wpool_ref, psc_ref,
                  wa_ref, wb_ref, wo_ref, l1g_ref, l1b_ref, wr_ref, br_ref, tri_ref,
                  x1_ref, h2u_ref, cr_ref, cnt_ref,
                  h_scr, u_scr, v_scr, vb_scr, a_scr, p_scr, m_scr, h2_scr, y_scr, xk_scr, ga_scr, gb_scr,
                  carry_scr):
    t = pl.program_id(0)
    n_tiles = pl.num_programs(0) - 1
    tf = jnp.minimum(t, n_tiles - 1)
    b = tf // n_i
    i = tf % n_i
    is_p = b < nbp
    ncb = D // COLB
    n_ext = ts + 2 * HALO
    half = D // 2
    main = slice(HALO, HALO + ts)

    def wmat(ref, cols=slice(None)):
        return pltpu.bitcast(ref[:, cols], BF16)

    @pl.when(t == 0)
    def _():
        carry_scr[...] = jnp.zeros_like(carry_scr)
        m_scr[...] = jnp.zeros_like(m_scr)
        xk_scr[...] = jnp.zeros_like(xk_scr)

    def row_rstd(read, mu):
        ss = jnp.zeros((ts, 1), F32)
        for j in range(ncb):
            xc = read(slice(j * COLB, (j + 1) * COLB)) - mu
            ss = ss + jnp.sum(xc * xc, axis=-1, keepdims=True)
        return lax.rsqrt(ss * (1.0 / D) + LN_EPS)

    modb = modb_ref[0]
    gate1_b, shift2, scale2 = modb[2:3], modb[3:4], modb[4:5]
    ysum = jnp.zeros((ts, 1), F32)
    for j in range(ncb):
        cols = slice(j * COLB, (j + 1) * COLB)
        tmix = _dot(m_scr[...], wmat(wo_ref, cols))
        y = alpha * xk_scr[:, cols] + gate1_b[:, cols] * tmix
        y_scr[:, cols] = y
        ysum = ysum + jnp.sum(y, axis=-1, keepdims=True)

    mod = mod_ref[0]
    shift1, scale1 = mod[0:1], mod[1:2]

    def adaln1(xv):
        return (_ln(xv) * (1.0 + scale1) + shift1).astype(BF16)

    h_scr[0:HALO, :] = adaln1(jnp.where(is_p, xpp_ref[0], xsp_ref[0]))
    h_scr[main, :] = adaln1(jnp.where(is_p, xp_ref[0], xs_ref[0]))
    h_scr[HALO + ts:, :] = adaln1(jnp.where(is_p, xpn_ref[0], xsn_ref[0]))

    def proj(hrows, c0):
        return _dot(h_scr[hrows, :], wmat(win_ref, slice(c0, c0 + COLB))) + bin_ref[:, c0:c0 + COLB]

    ymu = ysum * (1.0 / D)
    yrs = row_rstd(lambda cols: y_scr[:, cols], ymu)
    xsum = jnp.zeros((ts, 1), F32)
    for j in range(ncb):
        cols = slice(j * COLB, (j + 1) * COLB)
        x1 = (y_scr[:, cols] - ymu) * yrs * l1g_ref[:, cols] + l1b_ref[:, cols]
        x1_ref[0, :, cols] = x1
        xsum = xsum + jnp.sum(x1, axis=-1, keepdims=True)
    xmu = xsum * (1.0 / D)
    xrs = row_rstd(lambda cols: x1_ref[0, :, cols], xmu)
    for j in range(ncb):
        cols = slice(j * COLB, (j + 1) * COLB)
        h2 = (x1_ref[0, :, cols] - xmu) * xrs * (1.0 + scale2[:, cols]) + shift2[:, cols]
        h2_scr[:, cols] = h2.astype(BF16)
    lo_bits = lax.bitcast_convert_type(h2_scr[:, 0:half].astype(F32), U32)
    hi_bits = lax.bitcast_convert_type(h2_scr[:, half:D].astype(F32), U32)
    h2u_ref[:, 0:half] = (lo_bits >> 16) | (hi_bits & jnp.uint32(0xFFFF0000))

    vsum = jnp.zeros((ts, 1), F32)
    for j in range(ncb):
        cols = slice(j * COLB, (j + 1) * COLB)
        u_scr[:, cols] = _gelu_tanh(proj(main, j * COLB))
        gv = _gelu_tanh(proj(main, D + j * COLB))
        v_scr[:, cols] = gv
        vsum = vsum + jnp.sum(gv, axis=-1, keepdims=True)

    logits = _dot(h2_scr[...], wmat(wr_ref)) + br_ref[...]
    lane = lax.broadcasted_iota(jnp.int32, (ts, ROUTER_LANES), 1)
    lane_f = lane.astype(F32)
    neg = -jnp.inf
    is_g = lane < N_GROUPS
    lg = jnp.where(is_g, logits, neg)
    mg = jnp.max(lg, axis=-1, keepdims=True)
    g_sel = jnp.min(jnp.where(lg == mg, lane_f, float(ROUTER_LANES)), axis=-1, keepdims=True)
    p_sel = 1.0 / jnp.sum(jnp.where(is_g, jnp.exp(logits - mg), 0.0), axis=-1, keepdims=True)
    e_lo = EXPERT_LANE0 + g_sel * EPG
    in_grp = (lane_f >= e_lo) & (lane_f < e_lo + EPG)
    le = jnp.where(in_grp, logits, neg)
    m1 = jnp.max(le, axis=-1, keepdims=True)
    i1 = jnp.min(jnp.where(le == m1, lane_f, float(ROUTER_LANES)), axis=-1, keepdims=True)
    le2 = jnp.where(lane_f == i1, neg, le)
    m2 = jnp.max(le2, axis=-1, keepdims=True)
    i2 = jnp.min(jnp.where(le2 == m2, lane_f, float(ROUTER_LANES)), axis=-1, keepdims=True)
    e2x = jnp.exp(m2 - m1)
    den = 1.0 + e2x
    g1 = p_sel / den
    g2 = p_sel * e2x / den

    j1 = i1 - e_lo
    j2 = i2 - e_lo
    first_is_a = j1 < j2
    ja = jnp.minimum(j1, j2)
    jb = jnp.maximum(j1, j2)
    cls = g_sel * PAIRS + (EPG - 1) * ja - ja * (ja - 1.0) * 0.5 + (jb - ja - 1.0)
    w_a = jnp.where(first_is_a, g1, g2)
    w_b = jnp.where(first_is_a, g2, g1)
    extra = jnp.where(lane == 0, w_a, jnp.where(lane == 1, w_b, 0.0))
    h2u_ref[:, half:half + ROUTER_LANES] = lax.bitcast_convert_type(extra, U32)

    hit = lane_f == cls
    onehot = jnp.where(hit, (t > 0).astype(F32), 0.0)

    def finish_ranks():
        pre = _dot(tri_ref[...], onehot.astype(BF16)) + carry_scr[0:1, :]
        rank = jnp.sum(jnp.where(hit, pre, 0.0), axis=-1, keepdims=True)
        carry_scr[0:1, :] = carry_scr[0:1, :] + jnp.sum(onehot, axis=0, keepdims=True)
        cnt_ref[...] = carry_scr[...]
        cr = jnp.where(lane == 0, cls, jnp.where(lane == 1, rank, 0.0))
        cr_ref[...] = cr.T[0:8, :]

    vmu = vsum * (1.0 / D)
    vrs = row_rstd(lambda cols: v_scr[:, cols], vmu)
    for j in range(ncb):
        cols = slice(j * COLB, (j + 1) * COLB)
        vb_scr[:, cols] = ((v_scr[:, cols] - vmu) * vrs * sg_ref[:, cols] + sb_ref[:, cols]).astype(BF16)

    def pool_project(gi):
        ext_pos = lax.broadcasted_iota(jnp.int32, (n_ext, 1), 0) + (i * ts - HALO)
        ext_valid = (ext_pos >= 0) & (ext_pos < s_len)
        return jnp.where(ext_valid, proj(slice(None), 2 * D + gi * POOL_GROUP_DIM), 0.0)

    def pool_group(gi, zp):
        w = POOL_WINDOWS[gi]
        cols = slice(gi * POOL_GROUP_DIM, (gi + 1) * POOL_GROUP_DIM)
        acc = zp + pltpu.roll(zp, 1, 0)
        if w >= 4:
            acc = pltpu.roll(acc, 1, 0) + pltpu.roll(acc, n_ext - 1, 0)
        if w >= 8:
            acc = pltpu.roll(acc, 2, 0) + pltpu.roll(acc, n_ext - 2, 0)
        if w >= 16:
            acc = pltpu.roll(acc, 4, 0) + pltpu.roll(acc, n_ext - 4, 0)
        inv_count = icnt_ref[:, gi:gi + 1]
        dd = (acc[main] * inv_count - zp[main]).astype(BF16)
        p_scr[:, cols] = (_dot(dd, pltpu.bitcast(wpool_ref[gi], BF16)) * psc_ref[:, cols]).astype(BF16)

    def sgu_chunk(c):
        crow = slice(c * CHUNK, (c + 1) * CHUNK)
        for hh in range(SGU_HEADS):
            cols = slice(hh * HEAD_DIM, (hh + 1) * HEAD_DIM)
            mixed = _dot(ws_ref[hh], vb_scr[crow, cols]) + bsf_ref[:, cols]
            a_scr[crow, cols] = (u_scr[crow, cols] * mixed).astype(BF16)

    assert len(POOL_WINDOWS) == ncb
    for j in range(ncb):
        cols = slice(j * COLB, (j + 1) * COLB)
        zp = pool_project(j)
        ga_scr[:, cols] = jax.nn.sigmoid(proj(main, 3 * D + j * COLB))
        gb_scr[:, cols] = jax.nn.sigmoid(proj(main, 4 * D + j * COLB))
        pool_group(j, zp)
    for j in range(ncb):
        cols = slice(j * COLB, (j + 1) * COLB)
        gb_scr[:, cols] = gb_scr[:, cols] * _dot(p_scr[...], wmat(wb_ref, cols))
    for c in range(ts // CHUNK):
        sgu_chunk(c)

    finish_ranks()

    for j in range(ncb):
        cols = slice(j * COLB, (j + 1) * COLB)
        ta = _dot(a_scr[...], wmat(wa_ref, cols))
        m_scr[:, cols] = (ga_scr[:, cols] * ta + gb_scr[:, cols]).astype(BF16)
    xk_scr[...] = jnp.where(is_p, xp_ref[0], xs_ref[0])


def _const_spec(shape):
    nd = len(shape)
    return pl.BlockSpec(shape, lambda t: (0,) * nd, pipeline_mode=pl.Buffered(1))


def _mixer(x_p, x_s, mod, wts, alpha, ts):
    nbp, s_len, _ = x_p.shape
    nbs = x_s.shape[0]
    assert x_s.shape[1] == s_len and s_len % ts == 0 and ts % CHUNK == 0
    nb = nbp + nbs
    n_i = s_len // ts
    hb = ts // HALO
    n_hb = s_len // HALO

    n_tiles = nb * n_i

    def front(t):
        tf = jnp.minimum(t, n_tiles - 1)
        return tf // n_i, tf % n_i

    def back(t):
        return jnp.maximum(t - 1, 0)

    def pick(b, on_p, val, const):
        return jnp.where(b < nbp if on_p else b >= nbp, val, const)

    def main_map(on_p):
        def f(t):
            b, i = front(t)
            bb = pick(b, on_p, b if on_p else b - nbp, nbp - 1 if on_p else 0)
            ii = pick(b, on_p, i, n_i - 1 if on_p else 0)
            return (bb, ii, 0)
        return f

    def halo_map(on_p, nxt):
        def f(t):
            b, i = front(t)
            bb = pick(b, on_p, b if on_p else b - nbp, nbp - 1 if on_p else 0)
            idx = jnp.minimum((i + 1) * hb, n_hb - 1) if nxt else jnp.maximum(i * hb - 1, 0)
            ii = pick(b, on_p, idx, n_hb - 1 if on_p else 0)
            return (bb, ii, 0)
        return f

    tri = (lax.broadcasted_iota(jnp.int32, (ts, ts), 1)
           < lax.broadcasted_iota(jnp.int32, (ts, ts), 0)).astype(BF16)
    consts = list(wts) + [tri]
    in_specs = [
        pl.BlockSpec((1, ts, D), main_map(True)),
        pl.BlockSpec((1, ts, D), main_map(False)),
        pl.BlockSpec((1, HALO, D), halo_map(True, False)),
        pl.BlockSpec((1, HALO, D), halo_map(False, False)),
        pl.BlockSpec((1, HALO, D), halo_map(True, True)),
        pl.BlockSpec((1, HALO, D), halo_map(False, True)),
        pl.BlockSpec((1, N_MOD, D), lambda t: (front(t)[0], 0, 0)),
        pl.BlockSpec((1, N_MOD, D), lambda t: (back(t) // n_i, 0, 0)),
        pl.BlockSpec((ts, ROUTER_LANES), lambda t: (front(t)[1], 0)),
    ] + [_const_spec(w.shape) for w in consts]

    pos = jnp.arange(s_len, dtype=jnp.int32)[:, None]
    win = jnp.asarray(POOL_WINDOWS + (1,) * (ROUTER_LANES - len(POOL_WINDOWS)), jnp.int32)[None, :]
    inv_count = 1.0 / (jnp.minimum(pos + (win - 1 - win // 2), s_len - 1)
                       - jnp.maximum(pos - win // 2, 0) + 1).astype(F32)
    n_tok = nb * s_len
    out_shape = (
        jax.ShapeDtypeStruct((nb, s_len, D), F32),
        jax.ShapeDtypeStruct((n_tok, ROW_W), U32),
        jax.ShapeDtypeStruct((8, n_tok), F32),
        jax.ShapeDtypeStruct((8, ROUTER_LANES), F32),
    )
    out_specs = (
        pl.BlockSpec((1, ts, D), lambda t: (back(t) // n_i, back(t) % n_i, 0)),
        pl.BlockSpec((ts, ROW_W), lambda t: (back(t), 0)),
        pl.BlockSpec((8, ts), lambda t: (0, back(t))),
        pl.BlockSpec((8, ROUTER_LANES), lambda t: (0, 0)),
    )
    scratch = [
        pltpu.VMEM((ts + 2 * HALO, D), BF16),
        pltpu.VMEM((ts, D), F32),
        pltpu.VMEM((ts, D), F32),
        pltpu.VMEM((ts, D), BF16),
        pltpu.VMEM((ts, D), BF16),
        pltpu.VMEM((ts, D), BF16),
        pltpu.VMEM((ts, D), BF16),
        pltpu.VMEM((ts, D), BF16),
        pltpu.VMEM((ts, D), F32),
        pltpu.VMEM((ts, D), F32),
        pltpu.VMEM((ts, D), F32),
        pltpu.VMEM((ts, D), F32),
        pltpu.VMEM((8, ROUTER_LANES), F32),
    ]
    return pl.pallas_call(
        functools.partial(_mixer_kernel, nbp, ts, s_len, n_i, alpha),
        grid=(n_tiles + 1,),
        in_specs=in_specs,
        out_specs=out_specs,
        out_shape=out_shape,
        scratch_shapes=scratch,
        compiler_params=pltpu.CompilerParams(
            dimension_semantics=("arbitrary",), vmem_limit_bytes=VMEM_LIMIT),
        name="mixer",
    )(x_p, x_s, x_p, x_s, x_p, x_s, mod, mod, inv_count, *consts)


def _dest_kernel(seg_ref, cls_ref, rank_ref, o_ref):
    cls = cls_ref[...]
    acc = rank_ref[...].astype(jnp.int32)
    for c in range(N_CLASSES):
        acc = acc + jnp.where(cls == float(c), seg_ref[c], 0)
    o_ref[...] = acc


def _dest(seg_start, cls, rank):
    shape = cls.shape
    return pl.pallas_call(
        _dest_kernel,
        grid_spec=pltpu.PrefetchScalarGridSpec(
            num_scalar_prefetch=1,
            grid=(1,),
            in_specs=[pl.BlockSpec(shape, lambda i, s: (0, 0)), pl.BlockSpec(shape, lambda i, s: (0, 0))],
            out_specs=pl.BlockSpec(shape, lambda i, s: (0, 0)),
        ),
        out_shape=jax.ShapeDtypeStruct(shape, jnp.int32),
        name="dest",
    )(seg_start, cls, rank)


def _sc_mesh():
    return plsc.VectorSubcoreMesh(core_axis_name="core", subcore_axis_name="subcore")


def _sc_scatter_rows(rows, dest, n_rows, win):
    n_tok, row_w = rows.shape
    mesh = _sc_mesh()
    n_workers = mesh.num_cores * mesh.num_subcores
    n_steps = n_tok // win
    assert n_tok % win == 0 and n_steps % n_workers == 0
    per_worker = n_steps // n_workers

    @pl.kernel(out_type=jax.ShapeDtypeStruct((n_rows, row_w), rows.dtype), mesh=mesh,
               scratch_types=[pltpu.VMEM((win,), jnp.int32), pltpu.VMEM((win, row_w), rows.dtype)])
    def scatter(rows_hbm, idx_hbm, out_hbm, idx_vmem, buf):
        worker = lax.axis_index("core") * mesh.num_subcores + lax.axis_index("subcore")

        @pl.loop(0, per_worker)
        def _(k):
            step = worker * per_worker + k
            pltpu.sync_copy(idx_hbm.at[step], idx_vmem)
            pltpu.sync_copy(rows_hbm.at[pl.ds(step * win, win)], buf)
            pltpu.sync_copy(buf, out_hbm.at[idx_vmem])

    return scatter(rows, dest.reshape(n_steps, win))


def _sc_gather_rows(table, idx, win):
    n_tok = idx.shape[0]
    row_w = table.shape[1]
    mesh = _sc_mesh()
    n_workers = mesh.num_cores * mesh.num_subcores
    n_steps = n_tok // win
    assert n_tok % win == 0 and n_steps % n_workers == 0
    per_worker = n_steps // n_workers

    @pl.kernel(out_type=jax.ShapeDtypeStruct((n_tok, row_w), table.dtype), mesh=mesh,
               scratch_types=[pltpu.VMEM((win,), jnp.int32), pltpu.VMEM((win, row_w), table.dtype)])
    def gather(table_hbm, idx_hbm, out_hbm, idx_vmem, buf):
        worker = lax.axis_index("core") * mesh.num_subcores + lax.axis_index("subcore")

        @pl.loop(0, per_worker)
        def _(k):
            step = worker * per_worker + k
            pltpu.sync_copy(idx_hbm.at[step], idx_vmem)
            pltpu.sync_copy(table_hbm.at[idx_vmem], buf)
            pltpu.sync_copy(buf, out_hbm.at[pl.ds(step * win, win)])

    return gather(table, idx.reshape(n_steps, win))


def _sc_pack_rows(w, chunk):
    *lead, k, n = w.shape
    rows = math.prod(lead) * k
    mesh = _sc_mesh()
    lanes = SC_LANES
    n_workers = mesh.num_cores * mesh.num_subcores
    assert k % 2 == 0 and rows % (n_workers * chunk) == 0 and chunk % 2 == 0 and n % (lanes * SC_PACK_UNROLL) == 0
    per_worker = rows // (n_workers * chunk)

    def to_bf16_bits(x):
        bits = lax.bitcast_convert_type(x, U32)
        return (bits + jnp.uint32(0x7FFF) + ((bits >> 16) & jnp.uint32(1))) >> 16

    @pl.kernel(out_type=jax.ShapeDtypeStruct((rows // 2, n), U32), mesh=mesh,
               scratch_types=[pltpu.VMEM((chunk, n), F32), pltpu.VMEM((chunk // 2, n), U32)],
               compiler_params=pltpu.CompilerParams(needs_layout_passes=False))
    def pack(w_hbm, out_hbm, src, dst):
        worker = lax.axis_index("core") * mesh.num_subcores + lax.axis_index("subcore")

        @pl.loop(0, per_worker)
        def _(step):
            blk = worker * per_worker + step
            row0 = pl.multiple_of(blk * chunk, chunk)
            out_row0 = pl.multiple_of(blk * (chunk // 2), chunk // 2)
            pltpu.sync_copy(w_hbm.at[pl.ds(row0, chunk)], src)

            @pl.loop(0, chunk // 2)
            def _(q):
                @pl.loop(0, n, step=lanes * SC_PACK_UNROLL)
                def _(c0):
                    for u in range(SC_PACK_UNROLL):
                        c = c0 + u * lanes
                        lo = to_bf16_bits(src[2 * q, pl.ds(c, lanes)])
                        hi = to_bf16_bits(src[2 * q + 1, pl.ds(c, lanes)])
                        dst[q, pl.ds(c, lanes)] = lo | (hi << 16)

            pltpu.sync_copy(dst, out_hbm.at[pl.ds(out_row0, chunk // 2)])

    return pack(w.reshape(rows, n)).reshape(*lead, k // 2, n)


def _expert_kernel(bm, ea_ref, g0_ref, u0_ref, d0_ref, g1_ref, u1_ref, d1_ref, slot_ref, nused_ref,
                   xs_ref, wga_ref, wua_ref, wda_ref,
                   wgb0_ref, wub0_ref, wdb0_ref, wgb1_ref, wub1_ref, wdb1_ref, ys_ref, act_scr, y_scr):
    j = pl.program_id(0)
    n_used = nused_ref[0]
    half = D // 2
    b_slots = ((wgb0_ref, wub0_ref, wdb0_ref), (wgb1_ref, wub1_ref, wdb1_ref))

    def wmat(ref, k0, k1, cols):
        return pltpu.bitcast(ref[0, k0 // 2:k1 // 2, cols], BF16)

    def store_previous():
        for c0 in range(0, half, COLB):
            lo_bits = lax.bitcast_convert_type(y_scr[:, c0:c0 + COLB].astype(BF16).astype(F32), U32)
            hi_bits = lax.bitcast_convert_type(y_scr[:, half + c0:half + c0 + COLB].astype(BF16).astype(F32), U32)
            ys_ref[:, c0:c0 + COLB] = (lo_bits >> 16) | (hi_bits & jnp.uint32(0xFFFF0000))

    def compute(slot):
        wgb_ref, wub_ref, wdb_ref = b_slots[slot]
        bits = xs_ref[:, 0:half]
        lo = lax.bitcast_convert_type(bits << 16, F32).astype(BF16)
        hi = lax.bitcast_convert_type(bits & jnp.uint32(0xFFFF0000), F32).astype(BF16)
        wts = lax.bitcast_convert_type(xs_ref[:, half:half + ROUTER_LANES], F32)
        for e, (wg_ref, wu_ref) in enumerate(((wga_ref, wua_ref), (wgb_ref, wub_ref))):
            for c in range(D_EXPERT // COLB):
                cols = slice(c * COLB, (c + 1) * COLB)
                g = _dot(lo, wmat(wg_ref, 0, half, cols)) + _dot(hi, wmat(wg_ref, half, D, cols))
                up = _dot(lo, wmat(wu_ref, 0, half, cols)) + _dot(hi, wmat(wu_ref, half, D, cols))
                act_scr[e, :, cols] = (jax.nn.silu(g) * up).astype(BF16)
        w_a = wts[:, 0:1]
        w_b = wts[:, 1:2]

        for c0 in range(0, D, COLB):
            cols = slice(c0, c0 + COLB)
            y_scr[:, cols] = (w_a * _dot(act_scr[0], wmat(wda_ref, 0, D_EXPERT, cols))
                              + w_b * _dot(act_scr[1], wmat(wdb_ref, 0, D_EXPERT, cols)))

    @pl.when(j == 0)
    def _():
        compute(0)

    for slot in (0, 1):
        @pl.when((j > 0) & (j < n_used) & (slot_ref[j] == slot))
        def _(slot=slot):
            store_previous()
            compute(slot)

    @pl.when(j == n_used)
    def _():
        store_previous()


def _experts(xs, blk_a, blk_b_slots, blk_slot, n_used, w_gate, w_up, w_down, bm):
    n_rows = xs.shape[0]
    n_blocks = n_rows // bm
    n_prefetch = 3 + len(blk_b_slots)

    def x_map(j, *pre):
        return (jnp.minimum(j, pre[-1][0] - 1), 0)

    def y_map(j, *pre):
        return (jnp.clip(j - 1, 0, pre[-1][0] - 1), 0)

    def w_map(k):
        return lambda j, *pre: (pre[k][j], 0, 0)

    gate_up = (1, D // 2, D_EXPERT)
    down = (1, D_EXPERT // 2, D)
    return pl.pallas_call(
        functools.partial(_expert_kernel, bm),
        grid_spec=pltpu.PrefetchScalarGridSpec(
            num_scalar_prefetch=n_prefetch,
            grid=(n_used[0] + 1,),
            in_specs=[pl.BlockSpec((bm, ROW_W), x_map),
                      pl.BlockSpec(gate_up, w_map(0)), pl.BlockSpec(gate_up, w_map(0)), pl.BlockSpec(down, w_map(0)),
                      pl.BlockSpec(gate_up, w_map(1)), pl.BlockSpec(gate_up, w_map(2)), pl.BlockSpec(down, w_map(3)),
                      pl.BlockSpec(gate_up, w_map(4)), pl.BlockSpec(gate_up, w_map(5)), pl.BlockSpec(down, w_map(6))],
            out_specs=pl.BlockSpec((bm, D // 2), y_map),
            scratch_shapes=[pltpu.VMEM((2, bm, D_EXPERT), BF16), pltpu.VMEM((bm, D), F32)],
        ),
        out_shape=jax.ShapeDtypeStruct((n_rows, D // 2), U32),
        compiler_params=pltpu.CompilerParams(
            dimension_semantics=("arbitrary",), vmem_limit_bytes=VMEM_LIMIT),
        name="experts",
    )(blk_a, *blk_b_slots, blk_slot, n_used, xs, w_gate, w_up, w_down, w_gate, w_up, w_down,
      w_gate, w_up, w_down)


def _final_kernel(alpha, b_off, nbg, s_len, ch, x1_hbm, f_hbm, mod_ref, l2g_ref, l2b_ref, out_hbm,
                  xbuf, fbuf, obuf, in_sem, out_sem):
    half = D // 2
    per_b = s_len // ch
    n_chunks = nbg * per_b
    sub = 128

    def in_copies(c, slot):
        b = c // per_b
        i = c % per_b
        return (pltpu.make_async_copy(x1_hbm.at[b + b_off, pl.ds(i * ch, ch)], xbuf.at[slot], in_sem.at[0, slot]),
                pltpu.make_async_copy(f_hbm.at[pl.ds(c * ch, ch)], fbuf.at[slot], in_sem.at[1, slot]))

    def out_copy(c, slot):
        b = c // per_b
        i = c % per_b
        return pltpu.make_async_copy(obuf.at[slot], out_hbm.at[b, pl.ds(i * ch, ch)], out_sem.at[slot])

    for k in range(FINAL_NBUF - 1):
        for cp in in_copies(k, k):
            cp.start()

    def step(c, carry):
        slot = c % FINAL_NBUF
        oslot = c % 2
        ahead = c + FINAL_NBUF - 1

        @pl.when(ahead < n_chunks)
        def _():
            for cp in in_copies(ahead, ahead % FINAL_NBUF):
                cp.start()

        for cp in in_copies(c, slot):
            cp.wait()

        @pl.when(c >= 2)
        def _():
            out_copy(c - 2, oslot).wait()

        gate2 = mod_ref[c // per_b + b_off][5:6]
        for r0 in range(0, ch, sub):
            rows = pl.ds(r0, sub)
            bits = fbuf[slot, rows, :]
            f_lo = lax.bitcast_convert_type(bits << 16, F32)
            f_hi = lax.bitcast_convert_type(bits & jnp.uint32(0xFFFF0000), F32)
            y_lo = alpha * xbuf[slot, rows, 0:half] + gate2[:, 0:half] * f_lo
            y_hi = alpha * xbuf[slot, rows, half:D] + gate2[:, half:D] * f_hi
            mu = (jnp.sum(y_lo, axis=-1, keepdims=True) + jnp.sum(y_hi, axis=-1, keepdims=True)) * (1.0 / D)
            c_lo = y_lo - mu
            c_hi = y_hi - mu
            var = (jnp.sum(c_lo * c_lo, axis=-1, keepdims=True)
                   + jnp.sum(c_hi * c_hi, axis=-1, keepdims=True)) * (1.0 / D)
            rs = lax.rsqrt(var + LN_EPS)
            obuf[oslot, rows, 0:half] = c_lo * rs * l2g_ref[:, 0:half] + l2b_ref[:, 0:half]
            obuf[oslot, rows, half:D] = c_hi * rs * l2g_ref[:, half:D] + l2b_ref[:, half:D]

        out_copy(c, oslot).start()
        return carry

    lax.fori_loop(0, n_chunks, step, 0)
    out_copy(n_chunks - 2, n_chunks % 2).wait()
    out_copy(n_chunks - 1, (n_chunks - 1) % 2).wait()


def _final(x1, f, mod, ln2_g, ln2_b, b_off, nbg, alpha, ch):
    s_len = x1.shape[1]
    assert s_len % ch == 0 and nbg * (s_len // ch) >= FINAL_NBUF
    vmem = pl.BlockSpec(memory_space=pltpu.VMEM)
    hbm = pl.BlockSpec(memory_space=pl.ANY)
    return pl.pallas_call(
        functools.partial(_final_kernel, alpha, b_off, nbg, s_len, ch),
        in_specs=[hbm, hbm, vmem, vmem, vmem],
        out_specs=hbm,
        out_shape=jax.ShapeDtypeStruct((nbg, s_len, D), F32),
        scratch_shapes=[pltpu.VMEM((FINAL_NBUF, ch, D), F32), pltpu.VMEM((FINAL_NBUF, ch, D // 2), U32),
                        pltpu.VMEM((2, ch, D), F32), pltpu.SemaphoreType.DMA((2, FINAL_NBUF)),
                        pltpu.SemaphoreType.DMA((2,))],
        compiler_params=pltpu.CompilerParams(vmem_limit_bytes=VMEM_LIMIT),
        name="final",
    )(x1, f, mod, ln2_g, ln2_b)


def _class_experts():
    ea, eb = [], []
    for g in range(N_GROUPS):
        for a in range(EPG):
            for b in range(a + 1, EPG):
                ea.append(g * EPG + a)
                eb.append(g * EPG + b)
    return np.asarray(ea, np.int32), np.asarray(eb, np.int32)


def _layer(x_p, x_s, c_all, p, alpha):
    nbp, s_len, _ = x_p.shape
    nbs = x_s.shape[0]
    nb = nbp + nbs
    n_tok = nb * s_len

    mod = _modulation(c_all, p["w_mod"], p["b_mod"]).reshape(nb, N_MOD, D)

    bsf = jnp.repeat(p["b_spatial"].T, HEAD_DIM, axis=1)
    n_in = p["w_in"].shape[1]
    w_router = jnp.concatenate(
        [p["w_router_group"], p["w_router_expert"],
         jnp.zeros((D, ROUTER_LANES - N_GROUPS - N_EXPERTS), F32)], axis=1)
    b_router = jnp.concatenate(
        [p["b_router_group"], p["b_router_expert"],
         jnp.zeros((ROUTER_LANES - N_GROUPS - N_EXPERTS,), F32)]).reshape(1, ROUTER_LANES)
    wts = [
        _pack_rows(p["w_in"]), p["b_in"].reshape(1, n_in),
        p["w_spatial"].astype(BF16), bsf,
        p["sgu_g"].reshape(1, D), p["sgu_b"].reshape(1, D),
        _pack_rows(p["w_pool"]), p["pool_scale"].reshape(1, D),
        _pack_rows(p["w_branch_a"]), _pack_rows(p["w_branch_b"]), _pack_rows(p["w_out"]),
        p["ln1_g"].reshape(1, D), p["ln1_b"].reshape(1, D),
        _pack_rows(w_router), b_router,
    ]
    expert_w = [_sc_pack_rows(p[name], SC_PACK_CHUNK) for name in ("w_exp_gate", "w_exp_up", "w_exp_down")]
    x1, rows, cr, cnt = _mixer(x_p, x_s, mod, wts, alpha, MIXER_TS)

    bm = EXPERT_BM
    assert n_tok % bm == 0 and n_tok % DEST_LANES == 0
    counts = cnt[0, :N_CLASSES].astype(jnp.int32)
    padded = (counts + bm - 1) // bm * bm
    seg_end = jnp.cumsum(padded).astype(jnp.int32)
    seg_start = seg_end - padded
    n_blocks = n_tok // bm + N_CLASSES
    block_start = jnp.arange(n_blocks, dtype=jnp.int32) * bm
    blk_cls = jnp.minimum(
        jnp.sum((seg_end[None, :] <= block_start[:, None]).astype(jnp.int32), axis=1), N_CLASSES - 1)
    cls_a, cls_b = _class_experts()
    blk_a = jnp.asarray(cls_a)[blk_cls]
    idx = jnp.arange(n_blocks, dtype=jnp.int32)
    changed = jnp.concatenate([jnp.zeros((1,), jnp.int32), (blk_cls[1:] != blk_cls[:-1]).astype(jnp.int32)])
    blk_slot = jnp.cumsum(changed).astype(jnp.int32) % 2
    later = jnp.where(blk_cls[None, :] > blk_cls[:, None], blk_cls[None, :], N_CLASSES)
    next_cls = jnp.minimum(jnp.min(later, axis=1), N_CLASSES - 1)
    earlier = jnp.max(jnp.where(blk_cls[None, :] < blk_cls[:, None], blk_cls[None, :], -1), axis=1)
    prev_cls = jnp.where(earlier >= 0, earlier, blk_cls)
    same = blk_cls[None, :] == blk_cls[:, None]
    pos_in_cls = idx - jnp.min(jnp.where(same, idx[None, :], n_blocks), axis=1)
    last_pos = jnp.sum(same.astype(jnp.int32), axis=1) - 1
    b_now = jnp.asarray(cls_b)[blk_cls]
    b_next = jnp.asarray(cls_b)[next_cls]
    b_prev = jnp.asarray(cls_b)[prev_cls]
    blk_b_slots = [
        jnp.where(blk_slot == s, b_now, jnp.where(pos_in_cls >= jnp.minimum(k, last_pos), b_next, b_prev))
        for s in (0, 1) for k in range(3)]
    n_used = (seg_end[-1:] // bm).astype(jnp.int32)

    lane_shape = (n_tok // DEST_LANES, DEST_LANES)
    dest = _dest(seg_start, cr[0].reshape(lane_shape), cr[1].reshape(lane_shape)).reshape(n_tok)

    xs = _sc_scatter_rows(rows, dest, n_blocks * bm, SC_WIN)
    ys = _experts(xs, blk_a, blk_b_slots, blk_slot, n_used, *expert_w, bm)

    l2g = p["ln2_g"].reshape(1, D)
    l2b = p["ln2_b"].reshape(1, D)
    tp = nbp * s_len
    f_p = _sc_gather_rows(ys, dest[:tp], SC_WIN)
    f_s = _sc_gather_rows(ys, dest[tp:], SC_WIN)
    y_p = _final(x1, f_p, mod, l2g, l2b, 0, nbp, alpha, FINAL_TK)
    y_s = _final(x1, f_s, mod, l2g, l2b, nbp, nbs, alpha, FINAL_TK)
    return y_p, y_s


_PARAM_NAMES = ("w_mod", "b_mod", "w_in", "b_in", "w_spatial", "b_spatial", "sgu_g", "sgu_b", "w_pool",
                "pool_scale", "w_branch_a", "w_branch_b", "w_out", "ln1_g", "ln1_b", "w_router_group",
                "b_router_group", "w_router_expert", "b_router_expert", "w_exp_gate", "w_exp_up",
                "w_exp_down", "ln2_g", "ln2_b")


def kernel(x_prompt, x_sample, c_prompt, c_sample, w_mod, b_mod, w_in, b_in, w_spatial, b_spatial, sgu_g, sgu_b, w_pool, pool_scale, w_branch_a, w_branch_b, w_out, ln1_g, ln1_b, w_router_group, b_router_group, w_router_expert, b_router_expert, w_exp_gate, w_exp_up, w_exp_down, ln2_g, ln2_b):
    params = (w_mod, b_mod, w_in, b_in, w_spatial, b_spatial, sgu_g, sgu_b, w_pool, pool_scale,
              w_branch_a, w_branch_b, w_out, ln1_g, ln1_b, w_router_group, b_router_group,
              w_router_expert, b_router_expert, w_exp_gate, w_exp_up, w_exp_down, ln2_g, ln2_b)
    depth = w_mod.shape[0]
    alpha = (2.0 * depth) ** 0.25
    c_all = jnp.concatenate([c_prompt, c_sample], axis=0)
    y_p, y_s = x_prompt, x_sample
    for l in range(depth):
        p = {name: w[l] for name, w in zip(_PARAM_NAMES, params)}
        y_p, y_s = _layer(y_p, y_s, c_all, p, alpha)
    return (y_p, y_s)
```
